```python
import jax, jax.numpy as jnp
from jax import lax
import numpy as np

D_MODEL = 1024
BATCH = 8
SEQ = 4096
DEPTH = 2

EXPAND = 2
D_MIX = EXPAND * D_MODEL
HEAD_DIM = 64
D_SSM = D_MIX // 2
D_ATT = D_MIX - D_SSM
H_SSM = D_SSM // HEAD_DIM
H_ATT = D_ATT // HEAD_DIM
SSM_GROUPS = 2
SSM_HPG = H_SSM // SSM_GROUPS
D_STATE = 128
CONV_K = 4
CONV_DIM = D_SSM + 2 * SSM_GROUPS * D_STATE
SSD_CHUNK = 128
DT_MIN = 0.001
DT_MAX = 0.1
ATT_BLOCK = 128
N_EXPERT_GROUPS = 4
EXPERTS_PER_GROUP = 8
N_EXPERTS = N_EXPERT_GROUPS * EXPERTS_PER_GROUP
D_EXPERT = D_MODEL // 4
TOP_K_IN_GROUP = 2
EPS = 1e-6
OFF_Z = 0
OFF_XBC = OFF_Z + D_SSM
OFF_DT = OFF_XBC + CONV_DIM
OFF_Q = OFF_DT + H_SSM
OFF_K = OFF_Q + D_ATT
OFF_V = OFF_K + D_ATT
OFF_F = OFF_V + D_ATT
IN_COLS = OFF_F + H_ATT

kernel_name = "hybrid_ssd_fox_hmoe"


def rmsnorm(x, w):
    xf = x.astype(jnp.float32)
    y = xf * lax.rsqrt(jnp.mean(xf * xf, axis=-1, keepdims=True) + EPS)
    return (y * w.astype(jnp.float32)).astype(x.dtype)


def causal_dwconv(u, w, b):
    out = lax.conv_general_dilated(
        u, w[:, None, :].astype(u.dtype), window_strides=(1,), padding=[(CONV_K - 1, 0)],
        dimension_numbers=('NWC', 'WIO', 'NWC'), feature_group_count=u.shape[-1])
    return out + b.astype(u.dtype)


def ssd_chunked(xs, dt, A, Bm, Cm, d_skip):
    b, s = xs.shape[0], xs.shape[1]
    nc, L = s // SSD_CHUNK, SSD_CHUNK
    G, R, P, N = SSM_GROUPS, SSM_HPG, HEAD_DIM, D_STATE
    X = xs.astype(jnp.float32).reshape(b, nc, L, G, R, P)
    dtc = dt.astype(jnp.float32).reshape(b, nc, L, G, R)
    Bc = Bm.astype(jnp.float32).reshape(b, nc, L, G, N)
    Cc = Cm.astype(jnp.float32).reshape(b, nc, L, G, N)
    a = dtc * A.astype(jnp.float32).reshape(G, R)
    cum = jnp.cumsum(a, axis=2)
    Xdt = X * dtc[..., None]
    cum_t = jnp.moveaxis(cum, 2, -1)
    causal = jnp.tril(jnp.ones((L, L), dtype=bool))
    seg = cum_t[..., :, None] - cum_t[..., None, :]
    decay = jnp.exp(jnp.where(causal, seg, -jnp.inf))
    CB = jnp.einsum('bclgn,bcsgn->bcgls', Cc, Bc)
    y_diag = jnp.einsum('bcgls,bcgrls,bcsgrp->bclgrp', CB, decay, Xdt)
    decay_to_end = jnp.exp(cum[:, :, -1:] - cum)
    states = jnp.einsum('bclgn,bclgr,bclgrp->bcgrpn', Bc, decay_to_end, Xdt)
    chunk_decay = jnp.exp(cum[:, :, -1])

    def step(h, inp):
        dec, st = inp
        return dec[..., None, None] * h + st, h

    h0 = jnp.zeros((b, G, R, P, N), jnp.float32)
    _, h_in = lax.scan(step, h0, (jnp.moveaxis(chunk_decay, 1, 0), jnp.moveaxis(states, 1, 0)))
    h_in = jnp.moveaxis(h_in, 0, 1)
    y_off = jnp.einsum('bclgn,bcgrpn,bclgr->bclgrp', Cc, h_in, jnp.exp(cum))
    y = y_diag + y_off + X * d_skip.astype(jnp.float32).reshape(G, R)[:, :, None]
    return y.reshape(b, s, G * R * P).astype(xs.dtype)


def forgetting_attention(q, k, v, log_f):
    b, s, h, d = q.shape
    nb = s // ATT_BLOCK
    scale = 1.0 / np.sqrt(d).astype(np.float32)
    qt = jnp.transpose(q, (0, 2, 1, 3))
    kt = jnp.transpose(k, (0, 2, 1, 3))
    vt = jnp.transpose(v, (0, 2, 1, 3))
    c = jnp.transpose(jnp.cumsum(log_f, axis=1), (0, 2, 1))
    qb = jnp.moveaxis(qt.reshape(b, h, nb, ATT_BLOCK, d), 2, 0)
    cqb = jnp.moveaxis(c.reshape(b, h, nb, ATT_BLOCK), 2, 0)
    k_pos = jnp.arange(s)

    def one_block(args):
        qi, cq, blk = args
        q_pos = blk * ATT_BLOCK + jnp.arange(ATT_BLOCK)
        logits = jnp.einsum('bhqd,bhkd->bhqk', qi, kt).astype(jnp.float32) * scale
        logits = logits + cq[..., :, None] - c[:, :, None, :]
        logits = jnp.where(k_pos[None, :] <= q_pos[:, None], logits, -jnp.inf)
        p = jax.nn.softmax(logits, axis=-1)
        return jnp.einsum('bhqk,bhkd->bhqd', p.astype(vt.dtype), vt)

    out = lax.map(one_block, (qb, cqb, jnp.arange(nb)))
    out = jnp.moveaxis(out, 0, 2).reshape(b, h, s, d)
    return jnp.transpose(out, (0, 2, 1, 3)).reshape(b, s, h * d)


def hierarchical_moe(x, w_rg, b_rg, w_re, b_re, w_gate, w_up, w_down):
    xf = x.astype(jnp.float32)
    grp_logits = xf @ w_rg.astype(jnp.float32) + b_rg.astype(jnp.float32)
    grp_prob = jax.nn.softmax(grp_logits, axis=-1)
    g_star = jnp.argmax(grp_logits, axis=-1)
    p_g = jnp.take_along_axis(grp_prob, g_star[..., None], axis=-1)
    exp_logits = (xf @ w_re.astype(jnp.float32) + b_re.astype(jnp.float32)).reshape(
        x.shape[0], x.shape[1], N_EXPERT_GROUPS, EXPERTS_PER_GROUP)
    sel = jnp.take_along_axis(exp_logits, g_star[..., None, None], axis=2)[..., 0, :]
    top_val, top_idx = lax.top_k(sel, TOP_K_IN_GROUP)
    w2 = jax.nn.softmax(top_val, axis=-1) * p_g
    expert_id = g_star[..., None] * EXPERTS_PER_GROUP + top_idx
    gates = jnp.sum(jax.nn.one_hot(expert_id, N_EXPERTS, dtype=jnp.float32) * w2[..., None],
                    axis=-2).astype(x.dtype)

    def per_seq(args):
        xi, gi = args
        hid = jax.nn.silu(jnp.einsum('sd,edf->sef', xi, w_gate)) * jnp.einsum('sd,edf->sef', xi, w_up)
        return jnp.einsum('sef,se,efd->sd', hid, gi, w_down)

    return lax.map(per_seq, (x, gates))


def setup_inputs(seed: int = 0) -> dict:
    key = jax.random.key(seed)
    ks = jax.random.split(key, 24)
    f32 = jnp.float32
    nrm = lambda k, shp, sc: (jax.random.normal(k, shp, f32) * sc).astype(f32)
    gain = lambda k, shp: 1.0 + 0.05 * jax.random.normal(k, shp, f32)
    dt0 = jnp.exp(jax.random.uniform(ks[5], (DEPTH, H_SSM), f32, np.log(DT_MIN), np.log(DT_MAX)))
    return {
        "x": jax.random.normal(ks[0], (BATCH, SEQ, D_MODEL), f32),
        "norm_mix_w": gain(ks[1], (DEPTH, D_MODEL)),
        "w_in": nrm(ks[2], (DEPTH, D_MODEL, IN_COLS), D_MODEL ** -0.5),
        "conv_w": nrm(ks[3], (DEPTH, CONV_K, CONV_DIM), CONV_K ** -0.5),
        "conv_b": nrm(ks[4], (DEPTH, CONV_DIM), 0.01),
        "dt_bias": dt0 + jnp.log(-jnp.expm1(-dt0)),
        "a_log": jnp.log(jax.random.uniform(ks[6], (DEPTH, H_SSM), f32, 1.0, 16.0)),
        "d_skip": gain(ks[7], (DEPTH, H_SSM)),
        "ssm_norm_w": gain(ks[8], (DEPTH, D_SSM)),
        "q_norm_w": gain(ks[9], (DEPTH, HEAD_DIM)),
        "k_norm_w": gain(ks[10], (DEPTH, HEAD_DIM)),
        "fox_f_bias": 3.0 + 0.5 * jax.random.normal(ks[11], (DEPTH, H_ATT), f32),
        "att_out_norm_w": gain(ks[12], (DEPTH, D_ATT)),
        "w_out": nrm(ks[13], (DEPTH, D_MIX, D_MODEL), D_MIX ** -0.5),
        "norm_ffn_w": gain(ks[14], (DEPTH, D_MODEL)),
        "w_router_group": nrm(ks[15], (DEPTH, D_MODEL, N_EXPERT_GROUPS), D_MODEL ** -0.5),
        "b_router_group": nrm(ks[16], (DEPTH, N_EXPERT_GROUPS), 0.01),
        "w_router_expert": nrm(ks[17], (DEPTH, D_MODEL, N_EXPERTS), D_MODEL ** -0.5),
        "b_router_expert": nrm(ks[18], (DEPTH, N_EXPERTS), 0.01),
        "w_expert_gate": nrm(ks[19], (DEPTH, N_EXPERTS, D_MODEL, D_EXPERT), D_MODEL ** -0.5),
        "w_expert_up": nrm(ks[20], (DEPTH, N_EXPERTS, D_MODEL, D_EXPERT), D_MODEL ** -0.5),
        "w_expert_down": nrm(ks[21], (DEPTH, N_EXPERTS, D_EXPERT, D_MODEL), D_EXPERT ** -0.5),
    }


def reference(x, norm_mix_w, w_in, conv_w, conv_b, dt_bias, a_log, d_skip, ssm_norm_w,
              q_norm_w, k_norm_w, fox_f_bias, att_out_norm_w, w_out, norm_ffn_w,
              w_router_group, b_router_group, w_router_expert, b_router_expert,
              w_expert_gate, w_expert_up, w_expert_down):
    b, s = x.shape[0], x.shape[1]
    for l in range(DEPTH):
        h = rmsnorm(x, norm_mix_w[l])
        proj = h @ w_in[l]
        z = proj[..., OFF_Z:OFF_XBC]
        xbc = jax.nn.silu(causal_dwconv(proj[..., OFF_XBC:OFF_DT], conv_w[l], conv_b[l]))
        xs = xbc[..., :D_SSM]
        Bm = xbc[..., D_SSM:D_SSM + SSM_GROUPS * D_STATE]
        Cm = xbc[..., D_SSM + SSM_GROUPS * D_STATE:]
        dt = jax.nn.softplus(proj[..., OFF_DT:OFF_Q].astype(jnp.float32) + dt_bias[l].astype(jnp.float32))
        A = -jnp.exp(a_log[l].astype(jnp.float32))
        y_ssm = ssd_chunked(xs, dt, A, Bm, Cm, d_skip[l])
        y_ssm = rmsnorm(y_ssm * jax.nn.silu(z), ssm_norm_w[l])

        q = rmsnorm(proj[..., OFF_Q:OFF_K].reshape(b, s, H_ATT, HEAD_DIM), q_norm_w[l])
        k = rmsnorm(proj[..., OFF_K:OFF_V].reshape(b, s, H_ATT, HEAD_DIM), k_norm_w[l])
        v = proj[..., OFF_V:OFF_F].reshape(b, s, H_ATT, HEAD_DIM)
        log_f = jax.nn.log_sigmoid(proj[..., OFF_F:].astype(jnp.float32) + fox_f_bias[l].astype(jnp.float32))
        y_att = rmsnorm(forgetting_attention(q, k, v, log_f), att_out_norm_w[l])

        x = x + jnp.concatenate([y_ssm, y_att], axis=-1) @ w_out[l]
        x = x + hierarchical_moe(rmsnorm(x, norm_ffn_w[l]), w_router_group[l], b_router_group[l],
                                 w_router_expert[l], b_router_expert[l], w_expert_gate[l],
                                 w_expert_up[l], w_expert_down[l])
    return x
```

```python
import functools

import jax
import jax.numpy as jnp
import numpy as np
from jax import lax
from jax.experimental import pallas as pl
from jax.experimental.pallas import tpu as pltpu

F32 = jnp.float32
BF16 = jnp.bfloat16

HEAD_DIM = 64
D_STATE = 128
SSM_GROUPS = 2
CONV_K = 4
SSD_CHUNK = 128
N_EXPERT_GROUPS = 4
EXPERTS_PER_GROUP = 8
N_EXPERTS = N_EXPERT_GROUPS * EXPERTS_PER_GROUP
EPS = 1e-6
LANES = 128
NEG_BIG = -1e30
VMEM_LIMIT = 56 * 1024 * 1024


def _cparams(sem):
    return pltpu.CompilerParams(dimension_semantics=sem, vmem_limit_bytes=VMEM_LIMIT)


def _split3(a):
    a1 = a.astype(BF16)
    r1 = a - a1.astype(F32)
    a2 = r1.astype(BF16)
    a3 = (r1 - a2.astype(F32)).astype(BF16)
    return a1, a2, a3


def _dot(a, b):
    return jnp.dot(a, b, preferred_element_type=F32)


def _dot3(a_f32, b_bf16):
    a1, a2, a3 = _split3(a_f32)
    return _dot(a1, b_bf16) + _dot(a2, b_bf16) + _dot(a3, b_bf16)


def _dot3_tn(b_bf16_t, a_f32):
    a1, a2, a3 = _split3(a_f32)
    return _dot(b_bf16_t, a1) + _dot(b_bf16_t, a2) + _dot(b_bf16_t, a3)


def _silu(x):
    return x / (1.0 + jnp.exp(-x))


def _inproj_kernel(col_tiles, x_ref, nw_ref, w_ref, ws_ref, *rest):
    outs = rest[:len(col_tiles)]
    small_ref = rest[len(col_tiles)]
    h_ref = rest[len(col_tiles) + 1]
    j = pl.program_id(1)

    @pl.when(j == 0)
    def _():
        x = x_ref[...]
        ms = jnp.mean(x * x, axis=-1, keepdims=True)
        hb = (x * lax.rsqrt(ms + EPS) * nw_ref[...]).astype(BF16)
        h_ref[...] = hb
        small_ref[...] = _dot(hb, ws_ref[...])

    res = _dot(h_ref[...], w_ref[...]).astype(BF16)
    for (start, n), o_ref in zip(col_tiles, outs):
        @pl.when((j >= start) & (j < start + n))
        def _(o_ref=o_ref):
            o_ref[...] = res


def _inproj(x2, nw, w_main, w_small, widths, tm, tn):
    t, d = x2.shape
    col_tiles = []
    s = 0
    for w in widths:
        assert w % tn == 0
        col_tiles.append((s, w // tn))
        s += w // tn
    nj = s
    assert w_main.shape == (d, nj * tn) and t % tm == 0

    def omap(start, n):
        return lambda i, j: (i, jnp.clip(j - start, 0, n - 1))

    out_shape = [jax.ShapeDtypeStruct((t, w), BF16) for w in widths]
    out_shape.append(jax.ShapeDtypeStruct((t, LANES), F32))
    out_specs = [pl.BlockSpec((tm, tn), omap(st, n)) for st, n in col_tiles]
    out_specs.append(pl.BlockSpec((tm, LANES), lambda i, j: (i, 0)))
    return pl.pallas_call(
        functools.partial(_inproj_kernel, tuple(col_tiles)),
        grid=(t // tm, nj),
        in_specs=[
            pl.BlockSpec((tm, d), lambda i, j: (i, 0)),
            pl.BlockSpec((1, d), lambda i, j: (0, 0)),
            pl.BlockSpec((d, tn), lambda i, j: (0, j)),
            pl.BlockSpec((d, LANES), lambda i, j: (0, 0)),
        ],
        out_specs=out_specs,
        out_shape=out_shape,
        scratch_shapes=[pltpu.VMEM((tm, d), BF16)],
        compiler_params=_cparams(("parallel", "arbitrary")),
        name="inproj",
    )(x2, nw, w_main, w_small)


def _ssd_kernel(d_ssm, xbc_ref, z_ref, small_ref, convw_ref, convb_ref, bias_ref, arow_ref,
                dskip_ref, nw_ref, expand_ref, y_ref, c_ref, ct_ref,
                ext_ref, state_ref, carry_ref):
    L = SSD_CHUNK
    n_pairs = d_ssm // LANES
    gw = d_ssm // SSM_GROUPS
    c_idx = pl.program_id(1)

    @pl.when(c_idx == 0)
    def _():
        ext_ref[0:8, :] = jnp.zeros((8, ext_ref.shape[1]), F32)
        state_ref[...] = jnp.zeros(state_ref.shape, F32)
        carry_ref[...] = jnp.zeros(carry_ref.shape, F32)

    ext_ref[8:8 + L, :] = xbc_ref[...].astype(F32)
    acc = convb_ref[...] + convw_ref[0:1, :] * ext_ref[5:5 + L, :]
    for k in range(1, CONV_K):
        acc = acc + convw_ref[k:k + 1, :] * ext_ref[5 + k:5 + k + L, :]
    ext_ref[0:8, :] = ext_ref[L:L + 8, :]
    xbc = _silu(acc)
    xs = xbc[:, :d_ssm]
    b_all = xbc[:, d_ssm:d_ssm + SSM_GROUPS * D_STATE].astype(BF16)
    c_all = xbc[:, d_ssm + SSM_GROUPS * D_STATE:].astype(BF16)

    row = lax.broadcasted_iota(jnp.int32, (L, L), 0)
    col = lax.broadcasted_iota(jnp.int32, (L, L), 1)
    causal = row >= col
    tril = jnp.where(causal, 1.0, 0.0).astype(BF16)
    pre = small_ref[...] + bias_ref[...]
    sp = jnp.maximum(pre, 0.0) + jnp.log(1.0 + jnp.exp(-jnp.abs(pre)))
    lane = lax.broadcasted_iota(jnp.int32, (L, LANES), 1)
    is_dt = lane < 16
    dt = jnp.where(is_dt, sp, 0.0)
    logf = jnp.where((lane >= 16) & (lane < 32), pre - sp, 0.0)
    a = dt * arow_ref[...]
    a_and_f = a + logf
    cum_all = _dot3_tn(tril, a_and_f)
    cum = jnp.where(is_dt, cum_all, 0.0)
    cum_t = cum.T
    c_run = cum_all + carry_ref[0:1, :]
    c_ref[...] = c_run
    ct_ref[0] = c_run.T
    carry_ref[0:1, :] = c_run[L - 1:L, :]

    expand = expand_ref[...]
    dt_x = _dot3(dt, expand)
    ecum_x = _dot3(jnp.exp(cum), expand)
    dte_x = _dot3(jnp.exp(cum[L - 1:L, :] - cum), expand)
    xdt = xs * dt_x
    xdt_b = xdt.astype(BF16)
    w_b = (xdt * dte_x).astype(BF16)

    y_parts = []
    for g in range(SSM_GROUPS):
        bg = b_all[:, g * D_STATE:(g + 1) * D_STATE]
        cg = c_all[:, g * D_STATE:(g + 1) * D_STATE]
        cb = lax.dot_general(cg, bg, (((1,), (1,)), ((), ())), preferred_element_type=F32)
        s_old = state_ref[g]
        y_off = _dot(cg, s_old.astype(BF16)) * ecum_x[:, g * gw:(g + 1) * gw]
        upd = lax.dot_general(bg, w_b[:, g * gw:(g + 1) * gw], (((0,), (0,)), ((), ())),
                              preferred_element_type=F32)
        state_ref[g] = s_old * ecum_x[L - 1:L, g * gw:(g + 1) * gw] + upd
        pairs_per_group = n_pairs // SSM_GROUPS
        for pp in range(pairs_per_group):
            p = g * pairs_per_group + pp
            xp = xdt_b[:, p * LANES:(p + 1) * LANES]
            ys = []
            for hh in range(2):
                h = 2 * p + hh
                seg = cum[:, h:h + 1] - cum_t[h:h + 1, :]
                decay = jnp.exp(jnp.where(causal, seg, NEG_BIG))
                ys.append(_dot((cb * decay).astype(BF16), xp))
            first = lax.broadcasted_iota(jnp.int32, (L, LANES), 1) < HEAD_DIM
            y_parts.append(jnp.where(first, ys[0], ys[1])
                           + y_off[:, pp * LANES:(pp + 1) * LANES])
    y = jnp.concatenate(y_parts, axis=1) + xs * dskip_ref[...]
    gated = y * _silu(z_ref[...].astype(F32))
    ms = jnp.mean(gated * gated, axis=-1, keepdims=True)
    y_ref[...] = (gated * lax.rsqrt(ms + EPS) * nw_ref[...]).astype(BF16)


def _ssd(xbc, z, small, convw, convb, bias_row, a_row, dskip_row, nw, expand, batch, seq):
    t, cdim = xbc.shape
    d_ssm = z.shape[1]
    L = SSD_CHUNK
    nc = seq // L
    rowmap = lambda b, c: (b * nc + c, 0)
    const = lambda b, c: (0, 0)
    return pl.pallas_call(
        functools.partial(_ssd_kernel, d_ssm),
        grid=(batch, nc),
        in_specs=[
            pl.BlockSpec((L, cdim), rowmap),
            pl.BlockSpec((L, d_ssm), rowmap),
            pl.BlockSpec((L, LANES), rowmap),
            pl.BlockSpec((CONV_K, cdim), const),
            pl.BlockSpec((1, cdim), const),
            pl.BlockSpec((1, LANES), const),
            pl.BlockSpec((1, LANES), const),
            pl.BlockSpec((1, d_ssm), const),
            pl.BlockSpec((1, d_ssm), const),
            pl.BlockSpec((LANES, d_ssm), const),
        ],
        out_specs=[
            pl.BlockSpec((L, d_ssm), rowmap),
            pl.BlockSpec((L, LANES), rowmap),
            pl.BlockSpec((1, LANES, L), lambda b, c: (b, 0, c)),
        ],
        out_shape=[
            jax.ShapeDtypeStruct((t, d_ssm), BF16),
            jax.ShapeDtypeStruct((t, LANES), F32),
            jax.ShapeDtypeStruct((batch, LANES, seq), F32),
        ],
        scratch_shapes=[
            pltpu.VMEM((L + 8, cdim), F32),
            pltpu.VMEM((SSM_GROUPS, D_STATE, d_ssm // SSM_GROUPS), F32),
            pltpu.VMEM((8, LANES), F32),
        ],
        compiler_params=_cparams(("parallel", "arbitrary")),
        name="ssd",
    )(xbc, z, small, convw, convb, bias_row, a_row, dskip_row, nw, expand)


def _headnorm(x, w_row, first):
    x2 = x * x
    s0 = jnp.sum(jnp.where(first, x2, 0.0), axis=-1, keepdims=True)
    s1 = jnp.sum(jnp.where(first, 0.0, x2), axis=-1, keepdims=True)
    ms = jnp.where(first, s0, s1) * (1.0 / HEAD_DIM)
    return x * lax.rsqrt(ms + EPS) * w_row


def _attn_kernel(tq, q_ref, k_ref, v_ref, c_ref, ct_ref, qw_ref, kw_ref, o_ref, kn_ref):
    hp = pl.program_id(1)
    qi = pl.program_id(2)
    seq = k_ref.shape[1]
    first = lax.broadcasted_iota(jnp.int32, (1, LANES), 1) < HEAD_DIM

    @pl.when(qi == 0)
    def _():
        def body(r, carry):
            rows = pl.ds(pl.multiple_of(r * tq, tq), tq)
            kn_ref[rows, :] = _headnorm(k_ref[0, rows, :].astype(F32), kw_ref[...], first).astype(BF16)
            return carry
        lax.fori_loop(0, seq // tq, body, 0)

    scale = 1.0 / np.sqrt(HEAD_DIM).astype(np.float32)
    qn = _headnorm(q_ref[0].astype(F32), qw_ref[...], first) * scale
    q_heads = [jnp.where(first, qn, 0.0).astype(BF16), jnp.where(first, 0.0, qn).astype(BF16)]

    c_tile = c_ref[0]
    sel_r = lax.broadcasted_iota(jnp.int32, (LANES, LANES), 0)
    cq = [_dot3(c_tile, jnp.where(sel_r == 16 + 2 * hp + hh, 1.0, 0.0).astype(BF16))
          for hh in range(2)]

    row = lax.broadcasted_iota(jnp.int32, (tq, tq), 0)
    col = lax.broadcasted_iota(jnp.int32, (tq, tq), 1)
    keep = row >= col

    def step(j, carry, masked):
        ms, ls, acc = carry
        cols = pl.ds(pl.multiple_of(j * tq, tq), tq)
        kb = kn_ref[cols, :]
        vb = v_ref[0, cols, :]
        new_m, new_l, alphas, pvs = [], [], [], []
        for hh in range(2):
            s = lax.dot_general(q_heads[hh], kb, (((1,), (1,)), ((), ())),
                                preferred_element_type=F32)
            ck = ct_ref[0, pl.ds(16 + 2 * hp + hh, 1), cols]
            s = s + (cq[hh] if tq == LANES else jnp.tile(cq[hh], (1, tq // LANES))) - ck
            if masked:
                s = jnp.where(keep, s, NEG_BIG)
            m_new = jnp.maximum(ms[hh], jnp.max(s, axis=-1, keepdims=True))
            alpha = jnp.exp(ms[hh] - m_new)
            p = jnp.exp(s - m_new)
            new_l.append(alpha * ls[hh] + jnp.sum(p, axis=-1, keepdims=True))
            new_m.append(m_new)
            alphas.append(alpha)
            pvs.append(_dot(p.astype(BF16), vb))
        acc = jnp.where(first, alphas[0], alphas[1]) * acc + jnp.where(first, pvs[0], pvs[1])
        return tuple(new_m), tuple(new_l), acc

    init = ((jnp.full((tq, 1), NEG_BIG, F32),) * 2, (jnp.zeros((tq, 1), F32),) * 2,
            jnp.zeros((tq, LANES), F32))
    carry = lax.fori_loop(0, qi, lambda j, c: step(j, c, False), init)
    ms, ls, acc = step(qi, carry, True)
    o_ref[0] = (acc / jnp.where(first, ls[0], ls[1])).astype(BF16)


def _attn(q, k, v, c, ct, qw_row, kw_row, batch, seq, tq):
    d_att = q.shape[-1]
    n_hp = d_att // LANES
    q3, k3, v3 = (a.reshape(batch, seq, d_att) for a in (q, k, v))
    c3 = c.reshape(batch, seq, LANES)
    return pl.pallas_call(
        functools.partial(_attn_kernel, tq),
        grid=(batch, n_hp, seq // tq),
        in_specs=[
            pl.BlockSpec((1, tq, LANES), lambda b, h, i: (b, i, h)),
            pl.BlockSpec((1, seq, LANES), lambda b, h, i: (b, 0, h)),
            pl.BlockSpec((1, seq, LANES), lambda b, h, i: (b, 0, h)),
            pl.BlockSpec((1, tq, LANES), lambda b, h, i: (b, i, 0)),
            pl.BlockSpec((1, LANES, seq), lambda b, h, i: (b, 0, 0)),
            pl.BlockSpec((1, LANES), lambda b, h, i: (0, 0)),
            pl.BlockSpec((1, LANES), lambda b, h, i: (0, 0)),
        ],
        out_specs=pl.BlockSpec((1, tq, LANES), lambda b, h, i: (b, i, h)),
        out_shape=jax.ShapeDtypeStruct((batch, seq, d_att), BF16),
        scratch_shapes=[pltpu.VMEM((seq, LANES), BF16)],
        compiler_params=_cparams(("parallel", "parallel", "arbitrary")),
        name="attn",
    )(q3, k3, v3, c3, ct, qw_row, kw_row).reshape(batch * seq, d_att)


def _outproj_kernel(ys_ref, ya_ref, x_ref, wos_ref, woa_ref, aw_ref, fw_ref, wr_ref, br_ref,
                    x1_ref, xn_ref, gates_ref):
    ya = ya_ref[...].astype(F32)
    ya = ya * lax.rsqrt(jnp.mean(ya * ya, axis=-1, keepdims=True) + EPS) * aw_ref[...]
    x1 = x_ref[...] + _dot(ys_ref[...], wos_ref[...]) + _dot(ya.astype(BF16), woa_ref[...])
    x1_ref[...] = x1
    xn = x1 * lax.rsqrt(jnp.mean(x1 * x1, axis=-1, keepdims=True) + EPS) * fw_ref[...]
    xn_ref[...] = xn.astype(BF16)

    x_hi = xn.astype(BF16)
    x_lo = (xn - x_hi.astype(F32)).astype(BF16)
    w = wr_ref[...]
    w_hi = w.astype(BF16)
    w_lo = (w - w_hi.astype(F32)).astype(BF16)
    logits = _dot(x_hi, w_hi) + _dot(x_lo, w_hi) + _dot(x_hi, w_lo) + br_ref[...]
    lane = lax.broadcasted_iota(jnp.int32, logits.shape, 1).astype(F32)

    def rmax(mask):
        return jnp.max(jnp.where(mask, logits, NEG_BIG), axis=-1, keepdims=True)

    def first_at(mask, val):
        return jnp.min(jnp.where(mask & (logits == val), lane, 1e9), axis=-1, keepdims=True)

    gmask = (lane >= N_EXPERTS) & (lane < N_EXPERTS + N_EXPERT_GROUPS)
    gmax = rmax(gmask)
    p_g = 1.0 / jnp.sum(jnp.where(gmask, jnp.exp(logits - gmax), 0.0), axis=-1, keepdims=True)
    g_star = first_at(gmask, gmax) - N_EXPERTS
    lo = g_star * EXPERTS_PER_GROUP
    emask = (lane >= lo) & (lane < lo + EXPERTS_PER_GROUP)
    v1 = rmax(emask)
    i1 = first_at(emask, v1)
    mask2 = emask & (lane != i1)
    v2 = rmax(mask2)
    i2 = first_at(mask2, v2)
    e2 = jnp.exp(v2 - v1)
    w1 = p_g / (1.0 + e2)
    w2 = p_g * e2 / (1.0 + e2)
    gates_ref[...] = jnp.where(lane == i1, w1, 0.0) + jnp.where(lane == i2, w2, 0.0)


def _outproj(ys, ya, x, wos, woa, aw, fw, wr, br, tm):
    t, d = x.shape
    dm = ys.shape[1]
    rows = lambda i: (i, 0)
    const = lambda i: (0, 0)
    return pl.pallas_call(
        _outproj_kernel,
        grid=(t // tm,),
        in_specs=[
            pl.BlockSpec((tm, dm), rows),
            pl.BlockSpec((tm, dm), rows),
            pl.BlockSpec((tm, d), rows),
            pl.BlockSpec((dm, d), const),
            pl.BlockSpec((dm, d), const),
            pl.BlockSpec((1, dm), const),
            pl.BlockSpec((1, d), const),
            pl.BlockSpec((d, LANES), const),
            pl.BlockSpec((1, LANES), const),
        ],
        out_specs=[
            pl.BlockSpec((tm, d), rows),
            pl.BlockSpec((tm, d), rows),
            pl.BlockSpec((tm, LANES), rows),
        ],
        out_shape=[
            jax.ShapeDtypeStruct((t, d), F32),
            jax.ShapeDtypeStruct((t, d), BF16),
            jax.ShapeDtypeStruct((t, LANES), F32),
        ],
        compiler_params=_cparams(("parallel",)),
        name="outproj",
    )(ys, ya, x, wos, woa, aw, fw, wr, br)


def _moe_kernel(xn_ref, gates_ref, x1_ref, wg_ref, wu_ref, wd_ref, o_ref, acc_ref):
    e = pl.program_id(1)

    @pl.when(e == 0)
    def _():
        acc_ref[...] = x1_ref[...]

    xn = xn_ref[...]
    gates = gates_ref[...]
    lane = lax.broadcasted_iota(jnp.int32, gates.shape, 1)
    gate = jnp.sum(jnp.where(lane == e, gates, 0.0), axis=-1, keepdims=True)
    hid = _silu(_dot(xn, wg_ref[0])) * _dot(xn, wu_ref[0]) * gate
    acc_ref[...] += _dot(hid.astype(BF16), wd_ref[0])

    @pl.when(e == pl.num_programs(1) - 1)
    def _():
        o_ref[...] = acc_ref[...]


def _moe(xn, gates, x1, wg, wu, wd, tm):
    t, d = x1.shape
    n_e, _, f = wg.shape
    rows = lambda i, e: (i, 0)
    return pl.pallas_call(
        _moe_kernel,
        grid=(t // tm, n_e),
        in_specs=[
            pl.BlockSpec((tm, d), rows),
            pl.BlockSpec((tm, LANES), rows),
            pl.BlockSpec((tm, d), rows),
            pl.BlockSpec((1, d, f), lambda i, e: (e, 0, 0)),
            pl.BlockSpec((1, d, f), lambda i, e: (e, 0, 0)),
            pl.BlockSpec((1, f, d), lambda i, e: (e, 0, 0)),
        ],
        out_specs=pl.BlockSpec((tm, d), rows),
        out_shape=jax.ShapeDtypeStruct((t, d), F32),
        scratch_shapes=[pltpu.VMEM((tm, d), F32)],
        compiler_params=_cparams(("parallel", "arbitrary")),
        name="moe",
    )(xn, gates, x1, wg, wu, wd)


def _pad_lanes(v, offset=0):
    out = jnp.zeros((1, LANES), F32)
    return out.at[0, offset:offset + v.shape[0]].set(v.astype(F32))


def kernel(x, norm_mix_w, w_in, conv_w, conv_b, dt_bias, a_log, d_skip, ssm_norm_w, q_norm_w,
           k_norm_w, fox_f_bias, att_out_norm_w, w_out, norm_ffn_w, w_router_group,
           b_router_group, w_router_expert, b_router_expert, w_expert_gate, w_expert_up,
           w_expert_down):
    batch, seq, d = x.shape
    depth = w_in.shape[0]
    d_ssm = ssm_norm_w.shape[1]
    d_att = att_out_norm_w.shape[1]
    h_ssm = dt_bias.shape[1]
    h_att = fox_f_bias.shape[1]
    conv_dim = conv_w.shape[2]
    assert h_ssm == 16 and h_att == 16 and d_ssm // h_ssm == HEAD_DIM and d_att // h_att == HEAD_DIM
    assert conv_dim == d_ssm + 2 * SSM_GROUPS * D_STATE
    off_xbc = d_ssm
    off_dt = off_xbc + conv_dim
    off_q = off_dt + h_ssm
    off_k = off_q + d_att
    off_v = off_k + d_att
    off_f = off_v + d_att
    t = batch * seq
    tm_in = min(1024, t)
    tm_out = min(512, t)
    tm_moe = min(1024, t)
    tq = min(256, seq)

    head_of_chan = jnp.arange(d_ssm) // HEAD_DIM
    expand = (jnp.arange(LANES)[:, None] == head_of_chan[None, :]).astype(BF16)

    xf = x.reshape(t, d)
    for l in range(depth):
        wl = w_in[l]
        w_main = jnp.concatenate(
            [wl[:, :off_xbc], wl[:, off_xbc:off_dt], wl[:, off_q:off_k], wl[:, off_k:off_v],
             wl[:, off_v:off_f]], axis=1).astype(BF16)
        w_small = jnp.zeros((d, LANES), F32)
        w_small = w_small.at[:, 0:h_ssm].set(wl[:, off_dt:off_q])
        w_small = w_small.at[:, 16:16 + h_att].set(wl[:, off_f:off_f + h_att]).astype(BF16)
        z, xbc, q, k, v, small = _inproj(
            xf, norm_mix_w[l][None, :], w_main, w_small,
            (d_ssm, conv_dim, d_att, d_att, d_att), tm_in, 512)

        bias_row = _pad_lanes(dt_bias[l]) + _pad_lanes(fox_f_bias[l], 16)
        a_row = _pad_lanes(-jnp.exp(a_log[l].astype(F32)))
        dskip_row = jnp.repeat(d_skip[l].astype(F32), HEAD_DIM)[None, :]
        y_ssm, c, ct = _ssd(xbc, z, small, conv_w[l].astype(F32), conv_b[l].astype(F32)[None, :],
                            bias_row, a_row, dskip_row, ssm_norm_w[l][None, :], expand, batch, seq)

        qw_row = jnp.tile(q_norm_w[l].astype(F32), 2)[None, :]
        kw_row = jnp.tile(k_norm_w[l].astype(F32), 2)[None, :]
        y_att = _attn(q, k, v, c, ct, qw_row, kw_row, batch, seq, tq)

        wr = jnp.zeros((d, LANES), F32)
        wr = wr.at[:, :N_EXPERTS].set(w_router_expert[l])
        wr = wr.at[:, N_EXPERTS:N_EXPERTS + N_EXPERT_GROUPS].set(w_router_group[l])
        br = _pad_lanes(b_router_expert[l]) + _pad_lanes(b_router_group[l], N_EXPERTS)
        wo = w_out[l].astype(BF16)
        x1, xn, gates = _outproj(y_ssm, y_att, xf, wo[:d_ssm], wo[d_ssm:],
                                 att_out_norm_w[l][None, :], norm_ffn_w[l][None, :], wr, br, tm_out)

        xf = _moe(xn, gates, x1, w_expert_gate[l].astype(BF16), w_expert_up[l].astype(BF16),
                  w_expert_down[l].astype(BF16), tm_moe)
    return xf.reshape(batch, seq, d)
```

```python
import functools

import jax
import jax.numpy as jnp
import numpy as np
from jax import lax
from jax.experimental import pallas as pl
from jax.experimental.pallas import tpu as pltpu

F32 = jnp.float32
BF16 = jnp.bfloat16

HEAD_DIM = 64
D_STATE = 128
SSM_GROUPS = 2
CONV_K = 4
SSD_CHUNK = 128
N_EXPERT_GROUPS = 4
EXPERTS_PER_GROUP = 8
N_EXPERTS = N_EXPERT_GROUPS * EXPERTS_PER_GROUP
EPS = 1e-6
LANES = 128
NEG_BIG = -1e30
LOG2E = 1.4426950408889634
VMEM_LIMIT = 56 * 1024 * 1024


def _cparams(sem):
    return pltpu.CompilerParams(dimension_semantics=sem, vmem_limit_bytes=VMEM_LIMIT)


def _split3(a):
    a1 = a.astype(BF16)
    r1 = a - a1.astype(F32)
    a2 = r1.astype(BF16)
    a3 = (r1 - a2.astype(F32)).astype(BF16)
    return a1, a2, a3


def _dot(a, b):
    return jnp.dot(a, b, preferred_element_type=F32)


def _dot3(a_f32, b_bf16):
    a1, a2, a3 = _split3(a_f32)
    return _dot(a1, b_bf16) + _dot(a2, b_bf16) + _dot(a3, b_bf16)


def _dot3_tn(b_bf16_t, a_f32):
    a1, a2, a3 = _split3(a_f32)
    return _dot(b_bf16_t, a1) + _dot(b_bf16_t, a2) + _dot(b_bf16_t, a3)


def _silu(x):
    return x / (1.0 + jnp.exp(-x))


def _inproj_kernel(col_tiles, x_ref, nw_ref, w_ref, ws_ref, *rest):
    outs = rest[:len(col_tiles)]
    small_ref = rest[len(col_tiles)]
    h_ref = rest[len(col_tiles) + 1]
    j = pl.program_id(1)

    @pl.when(j == 0)
    def _():
        x = x_ref[...]
        ms = jnp.mean(x * x, axis=-1, keepdims=True)
        hb = (x * lax.rsqrt(ms + EPS) * nw_ref[...]).astype(BF16)
        h_ref[...] = hb
        small_ref[...] = _dot(hb, ws_ref[...])

    res = _dot(h_ref[...], w_ref[...]).astype(BF16)
    for (start, n), o_ref in zip(col_tiles, outs):
        @pl.when((j >= start) & (j < start + n))
        def _(o_ref=o_ref):
            o_ref[...] = res


def _inproj(x2, nw, w_main, w_small, widths, tm, tn):
    t, d = x2.shape
    col_tiles = []
    s = 0
    for w in widths:
        assert w % tn == 0
        col_tiles.append((s, w // tn))
        s += w // tn
    nj = s
    assert w_main.shape == (d, nj * tn) and t % tm == 0

    def omap(start, n):
        return lambda i, j: (i, jnp.clip(j - start, 0, n - 1))

    out_shape = [jax.ShapeDtypeStruct((t, w), BF16) for w in widths]
    out_shape.append(jax.ShapeDtypeStruct((t, LANES), F32))
    out_specs = [pl.BlockSpec((tm, tn), omap(st, n)) for st, n in col_tiles]
    out_specs.append(pl.BlockSpec((tm, LANES), lambda i, j: (i, 0)))
    return pl.pallas_call(
        functools.partial(_inproj_kernel, tuple(col_tiles)),
        grid=(t // tm, nj),
        in_specs=[
            pl.BlockSpec((tm, d), lambda i, j: (i, 0)),
            pl.BlockSpec((1, d), lambda i, j: (0, 0)),
            pl.BlockSpec((d, tn), lambda i, j: (0, j)),
            pl.BlockSpec((d, LANES), lambda i, j: (0, 0)),
        ],
        out_specs=out_specs,
        out_shape=out_shape,
        scratch_shapes=[pltpu.VMEM((tm, d), BF16)],
        compiler_params=_cparams(("parallel", "arbitrary")),
        name="inproj",
    )(x2, nw, w_main, w_small)


def _ssd_kernel(d_ssm, xbc_ref, z_ref, small_ref, convw_ref, convb_ref, bias_ref, arow_ref,
                dskip_ref, nw_ref, expand_ref, y_ref, c_ref,
                ext_ref, state_ref, carry_ref):
    L = SSD_CHUNK
    n_pairs = d_ssm // LANES
    gw = d_ssm // SSM_GROUPS
    c_idx = pl.program_id(1)

    @pl.when(c_idx == 0)
    def _():
        ext_ref[0:8, :] = jnp.zeros((8, ext_ref.shape[1]), F32)
        state_ref[...] = jnp.zeros(state_ref.shape, F32)
        carry_ref[...] = jnp.zeros(carry_ref.shape, F32)

    ext_ref[8:8 + L, :] = xbc_ref[...].astype(F32)
    acc = convb_ref[...] + convw_ref[0:1, :] * ext_ref[5:5 + L, :]
    for k in range(1, CONV_K):
        acc = acc + convw_ref[k:k + 1, :] * ext_ref[5 + k:5 + k + L, :]
    ext_ref[0:8, :] = ext_ref[L:L + 8, :]
    xbc = _silu(acc)
    xs = xbc[:, :d_ssm]
    b_all = xbc[:, d_ssm:d_ssm + SSM_GROUPS * D_STATE].astype(BF16)
    c_all = xbc[:, d_ssm + SSM_GROUPS * D_STATE:].astype(BF16)

    row = lax.broadcasted_iota(jnp.int32, (L, L), 0)
    col = lax.broadcasted_iota(jnp.int32, (L, L), 1)
    causal = row >= col
    tril = jnp.where(causal, 1.0, 0.0).astype(BF16)
    pre = small_ref[...] + bias_ref[...]
    sp = jnp.maximum(pre, 0.0) + jnp.log(1.0 + jnp.exp(-jnp.abs(pre)))
    lane = lax.broadcasted_iota(jnp.int32, (L, LANES), 1)
    is_dt = lane < 16
    dt = jnp.where(is_dt, sp, 0.0)
    logf = jnp.where((lane >= 16) & (lane < 32), pre - sp, 0.0)
    a = dt * arow_ref[...]
    a_and_f = a + logf
    cum_all = _dot3_tn(tril, a_and_f)
    cum = jnp.where(is_dt, cum_all, 0.0)
    cum_t = cum.T
    c_run = cum_all + carry_ref[0:1, :]
    c_ref[...] = c_run
    carry_ref[0:1, :] = c_run[L - 1:L, :]

    expand = expand_ref[...]
    dt_x = _dot3(dt, expand)
    ecum_x = _dot3(jnp.exp(cum), expand)
    dte_x = _dot3(jnp.exp(cum[L - 1:L, :] - cum), expand)
    xdt = xs * dt_x
    xdt_b = xdt.astype(BF16)
    w_b = (xdt * dte_x).astype(BF16)

    y_parts = []
    for g in range(SSM_GROUPS):
        bg = b_all[:, g * D_STATE:(g + 1) * D_STATE]
        cg = c_all[:, g * D_STATE:(g + 1) * D_STATE]
        cb = lax.dot_general(cg, bg, (((1,), (1,)), ((), ())), preferred_element_type=F32)
        s_old = state_ref[g]
        y_off = _dot(cg, s_old.astype(BF16)) * ecum_x[:, g * gw:(g + 1) * gw]
        upd = lax.dot_general(bg, w_b[:, g * gw:(g + 1) * gw], (((0,), (0,)), ((), ())),
                              preferred_element_type=F32)
        state_ref[g] = s_old * ecum_x[L - 1:L, g * gw:(g + 1) * gw] + upd
        pairs_per_group = n_pairs // SSM_GROUPS
        for pp in range(pairs_per_group):
            p = g * pairs_per_group + pp
            xp = xdt_b[:, p * LANES:(p + 1) * LANES]
            ys = []
            for hh in range(2):
                h = 2 * p + hh
                seg = cum[:, h:h + 1] - cum_t[h:h + 1, :]
                decay = jnp.exp(jnp.where(causal, seg, NEG_BIG))
                ys.append(_dot((cb * decay).astype(BF16), xp))
            first = lax.broadcasted_iota(jnp.int32, (L, LANES), 1) < HEAD_DIM
            y_parts.append(jnp.where(first, ys[0], ys[1])
                           + y_off[:, pp * LANES:(pp + 1) * LANES])
    y = jnp.concatenate(y_parts, axis=1) + xs * dskip_ref[...]
    gated = y * _silu(z_ref[...].astype(F32))
    ms = jnp.mean(gated * gated, axis=-1, keepdims=True)
    y_ref[...] = (gated * lax.rsqrt(ms + EPS) * nw_ref[...]).astype(BF16)


def _ssd(xbc, z, small, convw, convb, bias_row, a_row, dskip_row, nw, expand, batch, seq):
    t, cdim = xbc.shape
    d_ssm = z.shape[1]
    L = SSD_CHUNK
    nc = seq // L
    rowmap = lambda b, c: (b * nc + c, 0)
    const = lambda b, c: (0, 0)
    return pl.pallas_call(
        functools.partial(_ssd_kernel, d_ssm),
        grid=(batch, nc),
        in_specs=[
            pl.BlockSpec((L, cdim), rowmap),
            pl.BlockSpec((L, d_ssm), rowmap),
            pl.BlockSpec((L, LANES), rowmap),
            pl.BlockSpec((CONV_K, cdim), const),
            pl.BlockSpec((1, cdim), const),
            pl.BlockSpec((1, LANES), const),
            pl.BlockSpec((1, LANES), const),
            pl.BlockSpec((1, d_ssm), const),
            pl.BlockSpec((1, d_ssm), const),
            pl.BlockSpec((LANES, d_ssm), const),
        ],
        out_specs=[
            pl.BlockSpec((L, d_ssm), rowmap),
            pl.BlockSpec((L, LANES), rowmap),
        ],
        out_shape=[
            jax.ShapeDtypeStruct((t, d_ssm), BF16),
            jax.ShapeDtypeStruct((t, LANES), F32),
        ],
        scratch_shapes=[
            pltpu.VMEM((L + 8, cdim), F32),
            pltpu.VMEM((SSM_GROUPS, D_STATE, d_ssm // SSM_GROUPS), F32),
            pltpu.VMEM((8, LANES), F32),
        ],
        compiler_params=_cparams(("parallel", "arbitrary")),
        name="ssd",
    )(xbc, z, small, convw, convb, bias_row, a_row, dskip_row, nw, expand)


def _headnorm(x, w_row, first):
    x2 = x * x
    s0 = jnp.sum(jnp.where(first, x2, 0.0), axis=-1, keepdims=True)
    s1 = jnp.sum(jnp.where(first, 0.0, x2), axis=-1, keepdims=True)
    ms = jnp.where(first, s0, s1) * (1.0 / HEAD_DIM)
    return x * lax.rsqrt(ms + EPS) * w_row


def _place(parts, moves, rr, cc):
    out = None
    for i, part in enumerate(parts):
        sel = None
        for src, dst in moves:
            one = jnp.where((rr == src) & (cc == dst + i), 1.0, 0.0)
            sel = one if sel is None else sel + one
        term = _dot(part, sel.astype(BF16))
        out = term if out is None else out + term
    return out


def _attn_kernel(tq, npair, q_ref, k_ref, v_ref, c_ref, qw_ref, kw_ref, o_ref,
                 kx_ref, vx_ref, qx_ref, sa_ref, sb_ref, m_ref, acc_ref):
    grp = pl.program_id(1)
    qi = pl.program_id(2)
    seq = k_ref.shape[1]
    lane = lax.broadcasted_iota(jnp.int32, (1, LANES), 1)
    first = lane < HEAD_DIM
    rr = lax.broadcasted_iota(jnp.int32, (LANES, LANES), 0)
    cc = lax.broadcasted_iota(jnp.int32, (LANES, LANES), 1)
    pairs = range(npair)
    bases = [16 + 2 * (grp * npair + p) for p in pairs]

    @pl.when(qi == 0)
    def _():
        k_const = jnp.where((lane >= 6) & (lane < 9), 1.0, 0.0)

        def body(r, carry):
            rows = pl.ds(pl.multiple_of(r * tq, tq), tq)
            parts = _split3(c_ref[0, rows, :] * LOG2E)
            for p in pairs:
                lanes = slice(p * LANES, (p + 1) * LANES)
                kn = _headnorm(k_ref[0, rows, lanes].astype(F32), kw_ref[...], first)
                kext = _place(parts, [(bases[p], 0), (bases[p] + 1, 3)], rr, cc) + k_const
                kx_ref[p, rows, 0:LANES] = kn.astype(BF16)
                kx_ref[p, rows, LANES:2 * LANES] = kext.astype(BF16)
                v = v_ref[0, rows, lanes].astype(F32)
                vx_ref[p, 0, :, rows] = jnp.where(first, v, 1.0).T.astype(BF16)
                vx_ref[p, 1, :, rows] = jnp.where(first, 1.0, v).T.astype(BF16)
            return carry
        lax.fori_loop(0, seq // tq, body, 0)

    scale = (1.0 / np.sqrt(HEAD_DIM)) * LOG2E
    q_rows = pl.ds(pl.multiple_of(qi * tq, tq), tq)
    q_parts = _split3(c_ref[0, q_rows, :] * LOG2E)
    for p in pairs:
        qn = _headnorm(q_ref[0, :, p * LANES:(p + 1) * LANES].astype(F32), qw_ref[...], first) * scale
        for hh in range(2):
            q_const = jnp.where((lane >= 3 * hh) & (lane < 3 * hh + 3), -1.0, 0.0)
            qext = _place(q_parts, [(bases[p] + hh, 6)], rr, cc) + q_const
            qh = jnp.where(first, qn, 0.0) if hh == 0 else jnp.where(first, 0.0, qn)
            qx_ref[p, 0:LANES, hh * tq:(hh + 1) * tq] = qh.T.astype(BF16)
            qx_ref[p, LANES:2 * LANES, hh * tq:(hh + 1) * tq] = qext.T.astype(BF16)
        m_ref[p] = jnp.full((1, 2 * tq), NEG_BIG, F32)
        acc_ref[p] = jnp.zeros((2, LANES, tq), F32)

    key_i = lax.broadcasted_iota(jnp.int32, (tq, 2 * tq), 0)
    qry_i = lax.broadcasted_iota(jnp.int32, (tq, 2 * tq), 1)
    keep = jnp.where(qry_i >= tq, qry_i - tq, qry_i) >= key_i

    def scores(j, s_ref):
        rows = pl.ds(pl.multiple_of(j * tq, tq), tq)
        for p in pairs:
            s_ref[p] = _dot(kx_ref[p, rows, :], qx_ref[p])

    def softmax_pv(j, s_ref, masked):
        cols = pl.ds(pl.multiple_of(j * tq, tq), tq)
        for p in pairs:
            s = s_ref[p]
            if masked:
                s = jnp.where(keep, s, NEG_BIG)
            m = m_ref[p]
            m_new = jnp.maximum(m, jnp.max(s, axis=0, keepdims=True))
            alpha = jnp.exp2(m - m_new)
            pb = jnp.exp2(s - m_new).astype(BF16)
            acc_ref[p, 0] = alpha[:, :tq] * acc_ref[p, 0] + _dot(vx_ref[p, 0, :, cols], pb[:, :tq])
            acc_ref[p, 1] = alpha[:, tq:] * acc_ref[p, 1] + _dot(vx_ref[p, 1, :, cols], pb[:, tq:])
            m_ref[p] = m_new

    scores(0, sa_ref)

    def two_blocks(i, carry):
        j = 2 * i
        scores(j + 1, sb_ref)
        softmax_pv(j, sa_ref, False)
        scores(j + 2, sa_ref)
        softmax_pv(j + 1, sb_ref, False)
        return carry
    lax.fori_loop(0, qi // 2, two_blocks, 0)

    @pl.when(qi % 2 == 0)
    def _():
        softmax_pv(qi, sa_ref, True)

    @pl.when(qi % 2 == 1)
    def _():
        scores(qi, sb_ref)
        softmax_pv(qi - 1, sa_ref, False)
        softmax_pv(qi, sb_ref, True)

    for p in pairs:
        out_t = jnp.concatenate(
            [acc_ref[p, 0, 0:HEAD_DIM, :] / acc_ref[p, 0, HEAD_DIM:LANES, :],
             acc_ref[p, 1, HEAD_DIM:LANES, :] / acc_ref[p, 1, 0:HEAD_DIM, :]], axis=0)
        o_ref[0, :, p * LANES:(p + 1) * LANES] = out_t.T.astype(BF16)


def _attn(q, k, v, c, qw_row, kw_row, batch, seq, tq, npair):
    d_att = q.shape[-1]
    width = npair * LANES
    assert d_att % width == 0 and seq % tq == 0
    q3, k3, v3 = (a.reshape(batch, seq, d_att) for a in (q, k, v))
    c3 = c.reshape(batch, seq, LANES)
    return pl.pallas_call(
        functools.partial(_attn_kernel, tq, npair),
        grid=(batch, d_att // width, seq // tq),
        in_specs=[
            pl.BlockSpec((1, tq, width), lambda b, h, i: (b, i, h)),
            pl.BlockSpec((1, seq, width), lambda b, h, i: (b, 0, h)),
            pl.BlockSpec((1, seq, width), lambda b, h, i: (b, 0, h)),
            pl.BlockSpec((1, seq, LANES), lambda b, h, i: (b, 0, 0)),
            pl.BlockSpec((1, LANES), lambda b, h, i: (0, 0)),
            pl.BlockSpec((1, LANES), lambda b, h, i: (0, 0)),
        ],
        out_specs=pl.BlockSpec((1, tq, width), lambda b, h, i: (b, i, h)),
        out_shape=jax.ShapeDtypeStruct((batch, seq, d_att), BF16),
        scratch_shapes=[
            pltpu.VMEM((npair, seq, 2 * LANES), BF16),
            pltpu.VMEM((npair, 2, LANES, seq), BF16),
            pltpu.VMEM((npair, 2 * LANES, 2 * tq), BF16),
            pltpu.VMEM((npair, tq, 2 * tq), F32),
            pltpu.VMEM((npair, tq, 2 * tq), F32),
            pltpu.VMEM((npair, 1, 2 * tq), F32),
            pltpu.VMEM((npair, 2, LANES, tq), F32),
        ],
        compiler_params=_cparams(("parallel", "parallel", "arbitrary")),
        name="attn",
    )(q3, k3, v3, c3, qw_row, kw_row).reshape(batch * seq, d_att)


def _outproj_kernel(ys_ref, ya_ref, x_ref, wos_ref, woa_ref, aw_ref, fw_ref, wr_ref, br_ref,
                    x1_ref, xn_ref, gates_ref):
    ya = ya_ref[...].astype(F32)
    ya = ya * lax.rsqrt(jnp.mean(ya * ya, axis=-1, keepdims=True) + EPS) * aw_ref[...]
    x1 = x_ref[...] + _dot(ys_ref[...], wos_ref[...]) + _dot(ya.astype(BF16), woa_ref[...])
    x1_ref[...] = x1
    xn = x1 * lax.rsqrt(jnp.mean(x1 * x1, axis=-1, keepdims=True) + EPS) * fw_ref[...]
    xn_ref[...] = xn.astype(BF16)

    x_hi = xn.astype(BF16)
    x_lo = (xn - x_hi.astype(F32)).astype(BF16)
    w = wr_ref[...]
    w_hi = w.astype(BF16)
    w_lo = (w - w_hi.astype(F32)).astype(BF16)
    logits = _dot(x_hi, w_hi) + _dot(x_lo, w_hi) + _dot(x_hi, w_lo) + br_ref[...]
    lane = lax.broadcasted_iota(jnp.int32, logits.shape, 1).astype(F32)

    def rmax(mask):
        return jnp.max(jnp.where(mask, logits, NEG_BIG), axis=-1, keepdims=True)

    def first_at(mask, val):
        return jnp.min(jnp.where(mask & (logits == val), lane, 1e9), axis=-1, keepdims=True)

    gmask = (lane >= N_EXPERTS) & (lane < N_EXPERTS + N_EXPERT_GROUPS)
    gmax = rmax(gmask)
    p_g = 1.0 / jnp.sum(jnp.where(gmask, jnp.exp(logits - gmax), 0.0), axis=-1, keepdims=True)
    g_star = first_at(gmask, gmax) - N_EXPERTS
    lo = g_star * EXPERTS_PER_GROUP
    emask = (lane >= lo) & (lane < lo + EXPERTS_PER_GROUP)
    v1 = rmax(emask)
    i1 = first_at(emask, v1)
    mask2 = emask & (lane != i1)
    v2 = rmax(mask2)
    i2 = first_at(mask2, v2)
    e2 = jnp.exp(v2 - v1)
    w1 = p_g / (1.0 + e2)
    w2 = p_g * e2 / (1.0 + e2)
    gates_ref[...] = jnp.where(lane == i1, w1, 0.0) + jnp.where(lane == i2, w2, 0.0)


def _outproj(ys, ya, x, wos, woa, aw, fw, wr, br, tm):
    t, d = x.shape
    dm = ys.shape[1]
    rows = lambda i: (i, 0)
    const = lambda i: (0, 0)
    return pl.pallas_call(
        _outproj_kernel,
        grid=(t // tm,),
        in_specs=[
            pl.BlockSpec((tm, dm), rows),
            pl.BlockSpec((tm, dm), rows),
            pl.BlockSpec((tm, d), rows),
            pl.BlockSpec((dm, d), const),
            pl.BlockSpec((dm, d), const),
            pl.BlockSpec((1, dm), const),
            pl.BlockSpec((1, d), const),
            pl.BlockSpec((d, LANES), const),
            pl.BlockSpec((1, LANES), const),
        ],
        out_specs=[
            pl.BlockSpec((tm, d), rows),
            pl.BlockSpec((tm, d), rows),
            pl.BlockSpec((tm, LANES), rows),
        ],
        out_shape=[
            jax.ShapeDtypeStruct((t, d), F32),
            jax.ShapeDtypeStruct((t, d), BF16),
            jax.ShapeDtypeStruct((t, LANES), F32),
        ],
        compiler_params=_cparams(("parallel",)),
        name="outproj",
    )(ys, ya, x, wos, woa, aw, fw, wr, br)


def _moe_kernel(xn_ref, gates_ref, x1_ref, wg_ref, wu_ref, wd_ref, o_ref, acc_ref):
    e = pl.program_id(1)

    @pl.when(e == 0)
    def _():
        acc_ref[...] = x1_ref[...]

    xn = xn_ref[...]
    gates = gates_ref[...]
    lane = lax.broadcasted_iota(jnp.int32, gates.shape, 1)
    gate = jnp.sum(jnp.where(lane == e, gates, 0.0), axis=-1, keepdims=True)
    hid = _silu(_dot(xn, wg_ref[0])) * _dot(xn, wu_ref[0]) * gate
    acc_ref[...] += _dot(hid.astype(BF16), wd_ref[0])

    @pl.when(e == pl.num_programs(1) - 1)
    def _():
        o_ref[...] = acc_ref[...]


def _moe(xn, gates, x1, wg, wu, wd, tm):
    t, d = x1.shape
    n_e, _, f = wg.shape
    rows = lambda i, e: (i, 0)
    return pl.pallas_call(
        _moe_kernel,
        grid=(t // tm, n_e),
        in_specs=[
            pl.BlockSpec((tm, d), rows),
            pl.BlockSpec((tm, LANES), rows),
            pl.BlockSpec((tm, d), rows),
            pl.BlockSpec((1, d, f), lambda i, e: (e, 0, 0)),
            pl.BlockSpec((1, d, f), lambda i, e: (e, 0, 0)),
            pl.BlockSpec((1, f, d), lambda i, e: (e, 0, 0)),
        ],
        out_specs=pl.BlockSpec((tm, d), rows),
        out_shape=jax.ShapeDtypeStruct((t, d), F32),
        scratch_shapes=[pltpu.VMEM((tm, d), F32)],
        compiler_params=_cparams(("parallel", "arbitrary")),
        name="moe",
    )(xn, gates, x1, wg, wu, wd)


def _pad_lanes(v, offset=0):
    out = jnp.zeros((1, LANES), F32)
    return out.at[0, offset:offset + v.shape[0]].set(v.astype(F32))


def kernel(x, norm_mix_w, w_in, conv_w, conv_b, dt_bias, a_log, d_skip, ssm_norm_w, q_norm_w,
           k_norm_w, fox_f_bias, att_out_norm_w, w_out, norm_ffn_w, w_router_group,
           b_router_group, w_router_expert, b_router_expert, w_expert_gate, w_expert_up,
           w_expert_down):
    batch, seq, d = x.shape
    depth = w_in.shape[0]
    d_ssm = ssm_norm_w.shape[1]
    d_att = att_out_norm_w.shape[1]
    h_ssm = dt_bias.shape[1]
    h_att = fox_f_bias.shape[1]
    conv_dim = conv_w.shape[2]
    assert h_ssm == 16 and h_att == 16 and d_ssm // h_ssm == HEAD_DIM and d_att // h_att == HEAD_DIM
    assert conv_dim == d_ssm + 2 * SSM_GROUPS * D_STATE
    off_xbc = d_ssm
    off_dt = off_xbc + conv_dim
    off_q = off_dt + h_ssm
    off_k = off_q + d_att
    off_v = off_k + d_att
    off_f = off_v + d_att
    t = batch * seq
    tm_in = min(1024, t)
    tm_out = min(512, t)
    tm_moe = min(1024, t)
    tq = min(256, seq)

    head_of_chan = jnp.arange(d_ssm) // HEAD_DIM
    expand = (jnp.arange(LANES)[:, None] == head_of_chan[None, :]).astype(BF16)

    xf = x.reshape(t, d)
    for l in range(depth):
        wl = w_in[l]
        w_main = jnp.concatenate(
            [wl[:, :off_xbc], wl[:, off_xbc:off_dt], wl[:, off_q:off_k], wl[:, off_k:off_v],
             wl[:, off_v:off_f]], axis=1).astype(BF16)
        w_small = jnp.zeros((d, LANES), F32)
        w_small = w_small.at[:, 0:h_ssm].set(wl[:, off_dt:off_q])
        w_small = w_small.at[:, 16:16 + h_att].set(wl[:, off_f:off_f + h_att]).astype(BF16)
        z, xbc, q, k, v, small = _inproj(
            xf, norm_mix_w[l][None, :], w_main, w_small,
            (d_ssm, conv_dim, d_att, d_att, d_att), tm_in, 512)

        bias_row = _pad_lanes(dt_bias[l]) + _pad_lanes(fox_f_bias[l], 16)
        a_row = _pad_lanes(-jnp.exp(a_log[l].astype(F32)))
        dskip_row = jnp.repeat(d_skip[l].astype(F32), HEAD_DIM)[None, :]
        y_ssm, c = _ssd(xbc, z, small, conv_w[l].astype(F32), conv_b[l].astype(F32)[None, :],
                            bias_row, a_row, dskip_row, ssm_norm_w[l][None, :], expand, batch, seq)

        qw_row = jnp.tile(q_norm_w[l].astype(F32), 2)[None, :]
        kw_row = jnp.tile(k_norm_w[l].astype(F32), 2)[None, :]
        y_att = _attn(q, k, v, c, qw_row, kw_row, batch, seq, tq, 2)

        wr = jnp.zeros((d, LANES), F32)
        wr = wr.at[:, :N_EXPERTS].set(w_router_expert[l])
        wr = wr.at[:, N_EXPERTS:N_EXPERTS + N_EXPERT_GROUPS].set(w_router_group[l])
        br = _pad_lanes(b_router_expert[l]) + _pad_lanes(b_router_group[l], N_EXPERTS)
        wo = w_out[l].astype(BF16)
        x1, xn, gates = _outproj(y_ssm, y_att, xf, wo[:d_ssm], wo[d_ssm:],
                                 att_out_norm_w[l][None, :], norm_ffn_w[l][None, :], wr, br, tm_out)

        xf = _moe(xn, gates, x1, w_expert_gate[l].astype(BF16), w_expert_up[l].astype(BF16),
                  w_expert_down[l].astype(BF16), tm_moe)
    return xf.reshape(batch, seq, d)
```

```python
import functools

import jax
import jax.numpy as jnp
import numpy as np
from jax import lax
from jax.experimental import pallas as pl
from jax.experimental.pallas import tpu as pltpu

F32 = jnp.float32
BF16 = jnp.bfloat16

HEAD_DIM = 64
D_STATE = 128
SSM_GROUPS = 2
CONV_K = 4
SSD_CHUNK = 128
N_EXPERT_GROUPS = 4
EXPERTS_PER_GROUP = 8
N_EXPERTS = N_EXPERT_GROUPS * EXPERTS_PER_GROUP
EPS = 1e-6
LANES = 128
NEG_BIG = -1e30
LOG2E = 1.4426950408889634
VMEM_LIMIT = 56 * 1024 * 1024


def _cparams(sem):
    return pltpu.CompilerParams(dimension_semantics=sem, vmem_limit_bytes=VMEM_LIMIT)


def _split3(a):
    a1 = a.astype(BF16)
    r1 = a - a1.astype(F32)
    a2 = r1.astype(BF16)
    a3 = (r1 - a2.astype(F32)).astype(BF16)
    return a1, a2, a3


def _dot(a, b):
    return jnp.dot(a, b, preferred_element_type=F32)


def _dot3(a_f32, b_bf16):
    a1, a2, a3 = _split3(a_f32)
    return _dot(a1, b_bf16) + _dot(a2, b_bf16) + _dot(a3, b_bf16)


def _dot2(a_f32, b_bf16):
    a1 = a_f32.astype(BF16)
    a2 = (a_f32 - a1.astype(F32)).astype(BF16)
    return _dot(a1, b_bf16) + _dot(a2, b_bf16)


def _dot3_tn(b_bf16_t, a_f32):
    a1, a2, a3 = _split3(a_f32)
    return _dot(b_bf16_t, a1) + _dot(b_bf16_t, a2) + _dot(b_bf16_t, a3)


def _silu(x):
    return x / (1.0 + jnp.exp(-x))


def _inproj_kernel(x_ref, nw_ref, w_ref, ws_ref, proj_ref, small_ref, h_ref):
    @pl.when(pl.program_id(1) == 0)
    def _():
        x = x_ref[...]
        ms = jnp.mean(x * x, axis=-1, keepdims=True)
        hb = (x * lax.rsqrt(ms + EPS) * nw_ref[...]).astype(BF16)
        h_ref[...] = hb
        small_ref[...] = _dot(hb, ws_ref[...])

    proj_ref[...] = _dot(h_ref[...], w_ref[...]).astype(BF16)


def _inproj(x2, nw, w_main, w_small, tm, tn):
    t, d = x2.shape
    n = w_main.shape[1]
    assert n % tn == 0 and t % tm == 0
    return pl.pallas_call(
        _inproj_kernel,
        grid=(t // tm, n // tn),
        in_specs=[
            pl.BlockSpec((tm, d), lambda i, j: (i, 0)),
            pl.BlockSpec((1, d), lambda i, j: (0, 0)),
            pl.BlockSpec((d, tn), lambda i, j: (0, j)),
            pl.BlockSpec((d, LANES), lambda i, j: (0, 0)),
        ],
        out_specs=[
            pl.BlockSpec((tm, tn), lambda i, j: (i, j)),
            pl.BlockSpec((tm, LANES), lambda i, j: (i, 0)),
        ],
        out_shape=[
            jax.ShapeDtypeStruct((t, n), BF16),
            jax.ShapeDtypeStruct((t, LANES), F32),
        ],
        scratch_shapes=[pltpu.VMEM((tm, d), BF16)],
        compiler_params=_cparams(("parallel", "arbitrary")),
        name="inproj",
    )(x2, nw, w_main, w_small)


def _ssd_kernel(d_ssm, n_xbc, *refs):
    xbc_refs = refs[:n_xbc]
    (z_ref, small_ref, convw_ref, convb_ref, bias_ref, arow_ref, dskip_ref, nw_ref, expand_ref,
     shift_ref, y_ref, c_ref, ext_ref, state_ref, carry_ref) = refs[n_xbc:]
    L = SSD_CHUNK
    n_pairs = d_ssm // LANES
    gw = d_ssm // SSM_GROUPS
    c_idx = pl.program_id(1)

    @pl.when(c_idx == 0)
    def _():
        ext_ref[L:2 * L, :] = jnp.zeros((L, ext_ref.shape[1]), BF16)
        state_ref[...] = jnp.zeros(state_ref.shape, F32)
        carry_ref[...] = jnp.zeros(carry_ref.shape, F32)

    u = jnp.concatenate([r[...] for r in xbc_refs], axis=1)
    ext_ref[0:L, :] = ext_ref[L:2 * L, :]
    ext_ref[L:2 * L, :] = u
    n_delay = CONV_K - 1
    delayed = _dot(shift_ref[...], ext_ref[...])
    acc = convb_ref[...] + convw_ref[n_delay:CONV_K, :] * u.astype(F32)
    for k in range(n_delay):
        acc = acc + convw_ref[k:k + 1, :] * delayed[k * L:(k + 1) * L, :]
    xbc = _silu(acc)
    xs = xbc[:, :d_ssm]
    b_all = xbc[:, d_ssm:d_ssm + SSM_GROUPS * D_STATE].astype(BF16)
    c_all = xbc[:, d_ssm + SSM_GROUPS * D_STATE:].astype(BF16)

    row = lax.broadcasted_iota(jnp.int32, (L, L), 0)
    col = lax.broadcasted_iota(jnp.int32, (L, L), 1)
    causal = row >= col
    tril = jnp.where(causal, 1.0, 0.0).astype(BF16)
    pre = small_ref[...] + bias_ref[...]
    sp = jnp.maximum(pre, 0.0) + jnp.log(1.0 + jnp.exp(-jnp.abs(pre)))
    lane = lax.broadcasted_iota(jnp.int32, (L, LANES), 1)
    is_dt = lane < 16
    dt = jnp.where(is_dt, sp, 0.0)
    logf = jnp.where((lane >= 16) & (lane < 32), pre - sp, 0.0)
    a = dt * arow_ref[...]
    a_and_f = a + logf
    cum_all = _dot3_tn(tril, a_and_f)
    cum = jnp.where(is_dt, cum_all, 0.0)
    cum_t = cum.T
    c_run = cum_all + carry_ref[0:1, :]
    c_ref[...] = c_run
    carry_ref[0:1, :] = c_run[L - 1:L, :]

    expand = expand_ref[...]
    dt_x = _dot2(dt, expand)
    ecum_x = _dot2(jnp.exp(cum), expand)
    dte_x = _dot2(jnp.exp(cum[L - 1:L, :] - cum), expand)
    xdt = xs * dt_x
    xdt_b = xdt.astype(BF16)
    w_b = (xdt * dte_x).astype(BF16)

    y_parts = []
    for g in range(SSM_GROUPS):
        bg = b_all[:, g * D_STATE:(g + 1) * D_STATE]
        cg = c_all[:, g * D_STATE:(g + 1) * D_STATE]
        cb = lax.dot_general(cg, bg, (((1,), (1,)), ((), ())), preferred_element_type=F32)
        s_old = state_ref[g]
        y_off = _dot(cg, s_old.astype(BF16)) * ecum_x[:, g * gw:(g + 1) * gw]
        upd = lax.dot_general(bg, w_b[:, g * gw:(g + 1) * gw], (((0,), (0,)), ((), ())),
                              preferred_element_type=F32)
        state_ref[g] = s_old * ecum_x[L - 1:L, g * gw:(g + 1) * gw] + upd
        pairs_per_group = n_pairs // SSM_GROUPS
        for pp in range(pairs_per_group):
            p = g * pairs_per_group + pp
            xp = xdt_b[:, p * LANES:(p + 1) * LANES]
            ys = []
            for hh in range(2):
                h = 2 * p + hh
                seg = cum[:, h:h + 1] - cum_t[h:h + 1, :]
                decay = jnp.exp(jnp.where(causal, seg, NEG_BIG))
                ys.append(_dot((cb * decay).astype(BF16), xp))
            first = lax.broadcasted_iota(jnp.int32, (L, LANES), 1) < HEAD_DIM
            y_parts.append(jnp.where(first, ys[0], ys[1])
                           + y_off[:, pp * LANES:(pp + 1) * LANES])
    y = jnp.concatenate(y_parts, axis=1) + xs * dskip_ref[...]
    gated = y * _silu(z_ref[...].astype(F32))
    ms = jnp.mean(gated * gated, axis=-1, keepdims=True)
    y_ref[...] = (gated * lax.rsqrt(ms + EPS) * nw_ref[...]).astype(BF16)


def _ssd(proj, z_off, xbc_off, cdim, d_ssm, small, convw, convb, bias_row, a_row, dskip_row, nw,
         expand, batch, seq):
    t = proj.shape[0]
    L = SSD_CHUNK
    nc = seq // L
    wx = 512
    assert cdim % wx == 0 and xbc_off % wx == 0 and z_off % d_ssm == 0
    n_xbc = cdim // wx
    rowmap = lambda b, c: (b * nc + c, 0)
    const = lambda b, c: (0, 0)
    n_delay = CONV_K - 1
    r = jnp.arange(n_delay * L)
    target = L + r % L - (n_delay - r // L)
    shift = (jnp.arange(2 * L)[None, :] == target[:, None]).astype(BF16)
    xbc_specs = [pl.BlockSpec((L, wx), lambda b, c, o=xbc_off // wx + i: (b * nc + c, o))
                 for i in range(n_xbc)]
    return pl.pallas_call(
        functools.partial(_ssd_kernel, d_ssm, n_xbc),
        grid=(batch, nc),
        in_specs=xbc_specs + [
            pl.BlockSpec((L, d_ssm), lambda b, c: (b * nc + c, z_off // d_ssm)),
            pl.BlockSpec((L, LANES), rowmap),
            pl.BlockSpec((CONV_K, cdim), const),
            pl.BlockSpec((1, cdim), const),
            pl.BlockSpec((1, LANES), const),
            pl.BlockSpec((1, LANES), const),
            pl.BlockSpec((1, d_ssm), const),
            pl.BlockSpec((1, d_ssm), const),
            pl.BlockSpec((LANES, d_ssm), const),
            pl.BlockSpec((n_delay * L, 2 * L), const),
        ],
        out_specs=[
            pl.BlockSpec((L, d_ssm), rowmap),
            pl.BlockSpec((L, LANES), rowmap),
        ],
        out_shape=[
            jax.ShapeDtypeStruct((t, d_ssm), BF16),
            jax.ShapeDtypeStruct((t, LANES), F32),
        ],
        scratch_shapes=[
            pltpu.VMEM((2 * L, cdim), BF16),
            pltpu.VMEM((SSM_GROUPS, D_STATE, d_ssm // SSM_GROUPS), F32),
            pltpu.VMEM((8, LANES), F32),
        ],
        compiler_params=_cparams(("parallel", "arbitrary")),
        name="ssd",
    )(*([proj] * n_xbc), proj, small, convw, convb, bias_row, a_row, dskip_row, nw, expand, shift)


def _headnorm(x, w_row, same_head):
    x2 = x * x
    hi = x2.astype(BF16)
    lo = (x2 - hi.astype(F32)).astype(BF16)
    ms = (_dot(hi, same_head) + _dot(lo, same_head)) * (1.0 / HEAD_DIM)
    return x * lax.rsqrt(ms + EPS) * w_row


def _place(parts, moves, rr, cc):
    out = None
    for i, part in enumerate(parts):
        sel = None
        for src, dst in moves:
            one = jnp.where((rr == src) & (cc == dst + i), 1.0, 0.0)
            sel = one if sel is None else sel + one
        term = _dot(part, sel.astype(BF16))
        out = term if out is None else out + term
    return out


def _attn_kernel(tq, npair, q_ref, k_ref, v_ref, c_ref, qw_ref, kw_ref, o_ref,
                 kx_ref, vx_ref, qx_ref, sa_ref, sb_ref, m_ref, acc_ref):
    grp = pl.program_id(1)
    qi = pl.program_id(2)
    seq = k_ref.shape[1]
    lane = lax.broadcasted_iota(jnp.int32, (1, LANES), 1)
    first = lane < HEAD_DIM
    rr = lax.broadcasted_iota(jnp.int32, (LANES, LANES), 0)
    cc = lax.broadcasted_iota(jnp.int32, (LANES, LANES), 1)
    same_head = jnp.where((rr < HEAD_DIM) == (cc < HEAD_DIM), 1.0, 0.0).astype(BF16)
    pairs = range(npair)
    bases = [16 + 2 * (grp * npair + p) for p in pairs]

    @pl.when(qi == 0)
    def _():
        k_const = jnp.where((lane >= 6) & (lane < 9), 1.0, 0.0)

        def body(r, carry):
            rows = pl.ds(pl.multiple_of(r * tq, tq), tq)
            parts = _split3(c_ref[0, rows, :] * LOG2E)
            for p in pairs:
                lanes = slice(p * LANES, (p + 1) * LANES)
                kn = _headnorm(k_ref[0, rows, lanes].astype(F32), kw_ref[...], same_head)
                kext = _place(parts, [(bases[p], 0), (bases[p] + 1, 3)], rr, cc) + k_const
                kx_ref[p, rows, 0:LANES] = kn.astype(BF16)
                kx_ref[p, rows, LANES:2 * LANES] = kext.astype(BF16)
                v = v_ref[0, rows, lanes].astype(F32)
                vx_ref[p, 0, :, rows] = jnp.where(first, v, 1.0).T.astype(BF16)
                vx_ref[p, 1, :, rows] = jnp.where(first, 1.0, v).T.astype(BF16)
            return carry
        lax.fori_loop(0, seq // tq, body, 0)

    scale = (1.0 / np.sqrt(HEAD_DIM)) * LOG2E
    q_rows = pl.ds(pl.multiple_of(qi * tq, tq), tq)
    q_parts = _split3(c_ref[0, q_rows, :] * LOG2E)
    for p in pairs:
        qn = _headnorm(q_ref[0, :, p * LANES:(p + 1) * LANES].astype(F32), qw_ref[...],
                       same_head) * scale
        for hh in range(2):
            q_const = jnp.where((lane >= 3 * hh) & (lane < 3 * hh + 3), -1.0, 0.0)
            qext = _place(q_parts, [(bases[p] + hh, 6)], rr, cc) + q_const
            qh = jnp.where(first, qn, 0.0) if hh == 0 else jnp.where(first, 0.0, qn)
            qx_ref[p, 0:LANES, hh * tq:(hh + 1) * tq] = qh.T.astype(BF16)
            qx_ref[p, LANES:2 * LANES, hh * tq:(hh + 1) * tq] = qext.T.astype(BF16)
        m_ref[p] = jnp.full((1, 2 * tq), NEG_BIG, F32)
        acc_ref[p] = jnp.zeros((2, LANES, tq), F32)

    key_i = lax.broadcasted_iota(jnp.int32, (tq, 2 * tq), 0)
    qry_i = lax.broadcasted_iota(jnp.int32, (tq, 2 * tq), 1)
    keep = jnp.where(qry_i >= tq, qry_i - tq, qry_i) >= key_i

    def scores(j, s_ref):
        rows = pl.ds(pl.multiple_of(j * tq, tq), tq)
        for p in pairs:
            s_ref[p] = _dot(kx_ref[p, rows, :], qx_ref[p])

    def softmax_pv(j, s_ref, masked):
        cols = pl.ds(pl.multiple_of(j * tq, tq), tq)
        for p in pairs:
            s = s_ref[p]
            if masked:
                s = jnp.where(keep, s, NEG_BIG)
            m = m_ref[p]
            m_new = jnp.maximum(m, jnp.max(s, axis=0, keepdims=True))
            alpha = jnp.exp2(m - m_new)
            pb = jnp.exp2(s - m_new).astype(BF16)
            acc_ref[p, 0] = alpha[:, :tq] * acc_ref[p, 0] + _dot(vx_ref[p, 0, :, cols], pb[:, :tq])
            acc_ref[p, 1] = alpha[:, tq:] * acc_ref[p, 1] + _dot(vx_ref[p, 1, :, cols], pb[:, tq:])
            m_ref[p] = m_new

    scores(0, sa_ref)

    def two_blocks(i, carry):
        j = 2 * i
        scores(j + 1, sb_ref)
        softmax_pv(j, sa_ref, False)
        scores(j + 2, sa_ref)
        softmax_pv(j + 1, sb_ref, False)
        return carry
    lax.fori_loop(0, qi // 2, two_blocks, 0)

    @pl.when(qi % 2 == 0)
    def _():
        softmax_pv(qi, sa_ref, True)

    @pl.when(qi % 2 == 1)
    def _():
        scores(qi, sb_ref)
        softmax_pv(qi - 1, sa_ref, False)
        softmax_pv(qi, sb_ref, True)

    for p in pairs:
        out_t = jnp.concatenate(
            [acc_ref[p, 0, 0:HEAD_DIM, :] / acc_ref[p, 0, HEAD_DIM:LANES, :],
             acc_ref[p, 1, HEAD_DIM:LANES, :] / acc_ref[p, 1, 0:HEAD_DIM, :]], axis=0)
        o_ref[0, :, p * LANES:(p + 1) * LANES] = out_t.T.astype(BF16)


def _attn(proj, q_off, k_off, v_off, d_att, c, qw_row, kw_row, batch, seq, tq, npair):
    width = npair * LANES
    assert d_att % width == 0 and seq % tq == 0
    assert q_off % width == 0 and k_off % width == 0 and v_off % width == 0
    qo, ko, vo = q_off // width, k_off // width, v_off // width
    q3 = k3 = v3 = proj.reshape(batch, seq, proj.shape[-1])
    c3 = c.reshape(batch, seq, LANES)
    return pl.pallas_call(
        functools.partial(_attn_kernel, tq, npair),
        grid=(batch, d_att // width, seq // tq),
        in_specs=[
            pl.BlockSpec((1, tq, width), lambda b, h, i: (b, i, qo + h)),
            pl.BlockSpec((1, seq, width), lambda b, h, i: (b, 0, ko + h)),
            pl.BlockSpec((1, seq, width), lambda b, h, i: (b, 0, vo + h)),
            pl.BlockSpec((1, seq, LANES), lambda b, h, i: (b, 0, 0)),
            pl.BlockSpec((1, LANES), lambda b, h, i: (0, 0)),
            pl.BlockSpec((1, LANES), lambda b, h, i: (0, 0)),
        ],
        out_specs=pl.BlockSpec((1, tq, width), lambda b, h, i: (b, i, h)),
        out_shape=jax.ShapeDtypeStruct((batch, seq, d_att), BF16),
        scratch_shapes=[
            pltpu.VMEM((npair, seq, 2 * LANES), BF16),
            pltpu.VMEM((npair, 2, LANES, seq), BF16),
            pltpu.VMEM((npair, 2 * LANES, 2 * tq), BF16),
            pltpu.VMEM((npair, tq, 2 * tq), F32),
            pltpu.VMEM((npair, tq, 2 * tq), F32),
            pltpu.VMEM((npair, 1, 2 * tq), F32),
            pltpu.VMEM((npair, 2, LANES, tq), F32),
        ],
        compiler_params=_cparams(("parallel", "parallel", "arbitrary")),
        name="attn",
    )(q3, k3, v3, c3, qw_row, kw_row).reshape(batch * seq, d_att)


def _outproj_kernel(ys_ref, ya_ref, x_ref, wos_ref, woa_ref, aw_ref, fw_ref, wr_ref, br_ref,
                    x1_ref, xn_ref, gates_ref):
    ya = ya_ref[...].astype(F32)
    ya = ya * lax.rsqrt(jnp.mean(ya * ya, axis=-1, keepdims=True) + EPS) * aw_ref[...]
    x1 = x_ref[...] + _dot(ys_ref[...], wos_ref[...]) + _dot(ya.astype(BF16), woa_ref[...])
    x1_ref[...] = x1
    xn = x1 * lax.rsqrt(jnp.mean(x1 * x1, axis=-1, keepdims=True) + EPS) * fw_ref[...]
    xn_ref[...] = xn.astype(BF16)

    x_hi = xn.astype(BF16)
    x_lo = (xn - x_hi.astype(F32)).astype(BF16)
    w = wr_ref[...]
    w_hi = w.astype(BF16)
    w_lo = (w - w_hi.astype(F32)).astype(BF16)
    logits = _dot(x_hi, w_hi) + _dot(x_lo, w_hi) + _dot(x_hi, w_lo) + br_ref[...]
    lane = lax.broadcasted_iota(jnp.int32, logits.shape, 1).astype(F32)

    def rmax(mask):
        return jnp.max(jnp.where(mask, logits, NEG_BIG), axis=-1, keepdims=True)

    def first_at(mask, val):
        return jnp.min(jnp.where(mask & (logits == val), lane, 1e9), axis=-1, keepdims=True)

    gmask = (lane >= N_EXPERTS) & (lane < N_EXPERTS + N_EXPERT_GROUPS)
    gmax = rmax(gmask)
    p_g = 1.0 / jnp.sum(jnp.where(gmask, jnp.exp(logits - gmax), 0.0), axis=-1, keepdims=True)
    g_star = first_at(gmask, gmax) - N_EXPERTS
    lo = g_star * EXPERTS_PER_GROUP
    emask = (lane >= lo) & (lane < lo + EXPERTS_PER_GROUP)
    v1 = rmax(emask)
    i1 = first_at(emask, v1)
    mask2 = emask & (lane != i1)
    v2 = rmax(mask2)
    i2 = first_at(mask2, v2)
    e2 = jnp.exp(v2 - v1)
    w1 = p_g / (1.0 + e2)
    w2 = p_g * e2 / (1.0 + e2)
    gates_ref[...] = jnp.where(lane == i1, w1, 0.0) + jnp.where(lane == i2, w2, 0.0)


def _outproj(ys, ya, x, wos, woa, aw, fw, wr, br, tm):
    t, d = x.shape
    dm = ys.shape[1]
    rows = lambda i: (i, 0)
    const = lambda i: (0, 0)
    return pl.pallas_call(
        _outproj_kernel,
        grid=(t // tm,),
        in_specs=[
            pl.BlockSpec((tm, dm), rows),
            pl.BlockSpec((tm, dm), rows),
            pl.BlockSpec((tm, d), rows),
            pl.BlockSpec((dm, d), const),
            pl.BlockSpec((dm, d), const),
            pl.BlockSpec((1, dm), const),
            pl.BlockSpec((1, d), const),
            pl.BlockSpec((d, LANES), const),
            pl.BlockSpec((1, LANES), const),
        ],
        out_specs=[
            pl.BlockSpec((tm, d), rows),
            pl.BlockSpec((tm, d), rows),
            pl.BlockSpec((tm, LANES), rows),
        ],
        out_shape=[
            jax.ShapeDtypeStruct((t, d), F32),
            jax.ShapeDtypeStruct((t, d), BF16),
            jax.ShapeDtypeStruct((t, LANES), F32),
        ],
        compiler_params=_cparams(("parallel",)),
        name="outproj",
    )(ys, ya, x, wos, woa, aw, fw, wr, br)


def _moe_kernel(xn_ref, gates_ref, x1_ref, wg_ref, wu_ref, wd_ref, o_ref, acc_ref):
    e = pl.program_id(1)

    @pl.when(e == 0)
    def _():
        acc_ref[...] = x1_ref[...]

    xn = xn_ref[...]
    gates = gates_ref[...]
    lane = lax.broadcasted_iota(jnp.int32, gates.shape, 1)
    gate = jnp.sum(jnp.where(lane == e, gates, 0.0), axis=-1, keepdims=True)
    hid = _silu(_dot(xn, wg_ref[0])) * _dot(xn, wu_ref[0]) * gate
    acc_ref[...] += _dot(hid.astype(BF16), wd_ref[0])

    @pl.when(e == pl.num_programs(1) - 1)
    def _():
        o_ref[...] = acc_ref[...]


def _moe(xn, gates, x1, wg, wu, wd, tm):
    t, d = x1.shape
    n_e, _, f = wg.shape
    rows = lambda i, e: (i, 0)
    return pl.pallas_call(
        _moe_kernel,
        grid=(t // tm, n_e),
        in_specs=[
            pl.BlockSpec((tm, d), rows),
            pl.BlockSpec((tm, LANES), rows),
            pl.BlockSpec((tm, d), rows),
            pl.BlockSpec((1, d, f), lambda i, e: (e, 0, 0)),
            pl.BlockSpec((1, d, f), lambda i, e: (e, 0, 0)),
            pl.BlockSpec((1, f, d), lambda i, e: (e, 0, 0)),
        ],
        out_specs=pl.BlockSpec((tm, d), rows),
        out_shape=jax.ShapeDtypeStruct((t, d), F32),
        scratch_shapes=[pltpu.VMEM((tm, d), F32)],
        compiler_params=_cparams(("parallel", "arbitrary")),
        name="moe",
    )(xn, gates, x1, wg, wu, wd)


def _pad_lanes(v, offset=0):
    out = jnp.zeros((1, LANES), F32)
    return out.at[0, offset:offset + v.shape[0]].set(v.astype(F32))


def kernel(x, norm_mix_w, w_in, conv_w, conv_b, dt_bias, a_log, d_skip, ssm_norm_w, q_norm_w,
           k_norm_w, fox_f_bias, att_out_norm_w, w_out, norm_ffn_w, w_router_group,
           b_router_group, w_router_expert, b_router_expert, w_expert_gate, w_expert_up,
           w_expert_down):
    batch, seq, d = x.shape
    depth = w_in.shape[0]
    d_ssm = ssm_norm_w.shape[1]
    d_att = att_out_norm_w.shape[1]
    h_ssm = dt_bias.shape[1]
    h_att = fox_f_bias.shape[1]
    conv_dim = conv_w.shape[2]
    assert h_ssm == 16 and h_att == 16 and d_ssm // h_ssm == HEAD_DIM and d_att // h_att == HEAD_DIM
    assert conv_dim == d_ssm + 2 * SSM_GROUPS * D_STATE
    off_xbc = d_ssm
    off_dt = off_xbc + conv_dim
    off_q = off_dt + h_ssm
    off_k = off_q + d_att
    off_v = off_k + d_att
    off_f = off_v + d_att
    t = batch * seq
    tm_in = min(1024, t)
    tm_out = min(512, t)
    tm_moe = min(1024, t)
    tq = min(256, seq)

    head_of_chan = jnp.arange(d_ssm) // HEAD_DIM
    expand = (jnp.arange(LANES)[:, None] == head_of_chan[None, :]).astype(BF16)

    xf = x.reshape(t, d)
    for l in range(depth):
        wl = w_in[l]
        w_main = jnp.concatenate(
            [wl[:, :off_xbc], wl[:, off_q:off_k], wl[:, off_k:off_v], wl[:, off_v:off_f],
             wl[:, off_xbc:off_dt]], axis=1).astype(BF16)
        p_q, p_k, p_v, p_xbc = d_ssm, d_ssm + d_att, d_ssm + 2 * d_att, d_ssm + 3 * d_att
        w_small = jnp.zeros((d, LANES), F32)
        w_small = w_small.at[:, 0:h_ssm].set(wl[:, off_dt:off_q])
        w_small = w_small.at[:, 16:16 + h_att].set(wl[:, off_f:off_f + h_att]).astype(BF16)
        proj, small = _inproj(xf, norm_mix_w[l][None, :], w_main, w_small, tm_in, 512)

        bias_row = _pad_lanes(dt_bias[l]) + _pad_lanes(fox_f_bias[l], 16)
        a_row = _pad_lanes(-jnp.exp(a_log[l].astype(F32)))
        dskip_row = jnp.repeat(d_skip[l].astype(F32), HEAD_DIM)[None, :]
        y_ssm, c = _ssd(proj, 0, p_xbc, conv_dim, d_ssm, small, conv_w[l].astype(F32),
                        conv_b[l].astype(F32)[None, :], bias_row, a_row, dskip_row,
                        ssm_norm_w[l][None, :], expand, batch, seq)

        qw_row = jnp.tile(q_norm_w[l].astype(F32), 2)[None, :]
        kw_row = jnp.tile(k_norm_w[l].astype(F32), 2)[None, :]
        y_att = _attn(proj, p_q, p_k, p_v, d_att, c, qw_row, kw_row, batch, seq, tq, 2)

        wr = jnp.zeros((d, LANES), F32)
        wr = wr.at[:, :N_EXPERTS].set(w_router_expert[l])
        wr = wr.at[:, N_EXPERTS:N_EXPERTS + N_EXPERT_GROUPS].set(w_router_group[l])
        br = _pad_lanes(b_router_expert[l]) + _pad_lanes(b_router_group[l], N_EXPERTS)
        wo = w_out[l].astype(BF16)
        x1, xn, gates = _outproj(y_ssm, y_att, xf, wo[:d_ssm], wo[d_ssm:],
                                 att_out_norm_w[l][None, :], norm_ffn_w[l][None, :], wr, br, tm_out)

        xf = _moe(xn, gates, x1, w_expert_gate[l].astype(BF16), w_expert_up[l].astype(BF16),
                  w_expert_down[l].astype(BF16), tm_moe)
    return xf.reshape(batch, seq, d)
```

```python
import functools

import jax
import jax.numpy as jnp
import numpy as np
from jax import lax
from jax.experimental import pallas as pl
from jax.experimental.pallas import tpu as pltpu

F32 = jnp.float32
BF16 = jnp.bfloat16

HEAD_DIM = 64
D_STATE = 128
SSM_GROUPS = 2
CONV_K = 4
SSD_CHUNK = 128
N_EXPERT_GROUPS = 4
EXPERTS_PER_GROUP = 8
N_EXPERTS = N_EXPERT_GROUPS * EXPERTS_PER_GROUP
EPS = 1e-6
LANES = 128
NEG_BIG = -1e30
LOG2E = 1.4426950408889634
VMEM_LIMIT = 56 * 1024 * 1024


def _cparams(sem):
    return pltpu.CompilerParams(dimension_semantics=sem, vmem_limit_bytes=VMEM_LIMIT)


def _split3(a):
    a1 = a.astype(BF16)
    r1 = a - a1.astype(F32)
    a2 = r1.astype(BF16)
    a3 = (r1 - a2.astype(F32)).astype(BF16)
    return a1, a2, a3


def _dot(a, b):
    return jnp.dot(a, b, preferred_element_type=F32)


def _dot3(a_f32, b_bf16):
    a1, a2, a3 = _split3(a_f32)
    return _dot(a1, b_bf16) + _dot(a2, b_bf16) + _dot(a3, b_bf16)


def _dot2(a_f32, b_bf16):
    a1 = a_f32.astype(BF16)
    a2 = (a_f32 - a1.astype(F32)).astype(BF16)
    return _dot(a1, b_bf16) + _dot(a2, b_bf16)


def _dot3_tn(b_bf16_t, a_f32):
    a1, a2, a3 = _split3(a_f32)
    return _dot(b_bf16_t, a1) + _dot(b_bf16_t, a2) + _dot(b_bf16_t, a3)


def _silu(x):
    return x / (1.0 + jnp.exp(-x))


def _inproj_kernel(tn, x_ref, nw_ref, w_ref, ws_ref, proj_ref, small_ref):
    x = x_ref[...]
    ms = jnp.mean(x * x, axis=-1, keepdims=True)
    hb = (x * lax.rsqrt(ms + EPS) * nw_ref[...]).astype(BF16)
    small_ref[...] = _dot(hb, ws_ref[...])
    for j in range(w_ref.shape[1] // tn):
        cols = slice(j * tn, (j + 1) * tn)
        proj_ref[:, cols] = _dot(hb, w_ref[:, cols]).astype(BF16)


def _inproj(x2, nw, w_main, w_small, tm, tn):
    t, d = x2.shape
    n = w_main.shape[1]
    assert n % tn == 0 and t % tm == 0
    return pl.pallas_call(
        functools.partial(_inproj_kernel, tn),
        grid=(t // tm,),
        in_specs=[
            pl.BlockSpec((tm, d), lambda i: (i, 0)),
            pl.BlockSpec((1, d), lambda i: (0, 0)),
            pl.BlockSpec((d, n), lambda i: (0, 0)),
            pl.BlockSpec((d, LANES), lambda i: (0, 0)),
        ],
        out_specs=[
            pl.BlockSpec((tm, n), lambda i: (i, 0)),
            pl.BlockSpec((tm, LANES), lambda i: (i, 0)),
        ],
        out_shape=[
            jax.ShapeDtypeStruct((t, n), BF16),
            jax.ShapeDtypeStruct((t, LANES), F32),
        ],
        compiler_params=_cparams(("parallel",)),
        name="inproj",
    )(x2, nw, w_main, w_small)


def _ssd_kernel(d_ssm, n_xbc, *refs):
    xbc_refs = refs[:n_xbc]
    (z_ref, small_ref, convw_ref, convb_ref, bias_ref, arow_ref, dskip_ref, nw_ref, expand_ref,
     shift_ref, y_ref, c_ref, ext_ref, state_ref, carry_ref) = refs[n_xbc:]
    L = SSD_CHUNK
    n_pairs = d_ssm // LANES
    gw = d_ssm // SSM_GROUPS
    c_idx = pl.program_id(1)

    @pl.when(c_idx == 0)
    def _():
        ext_ref[L:2 * L, :] = jnp.zeros((L, ext_ref.shape[1]), BF16)
        state_ref[...] = jnp.zeros(state_ref.shape, F32)
        carry_ref[...] = jnp.zeros(carry_ref.shape, F32)

    u = jnp.concatenate([r[...] for r in xbc_refs], axis=1)
    ext_ref[0:L, :] = ext_ref[L:2 * L, :]
    ext_ref[L:2 * L, :] = u
    n_delay = CONV_K - 1
    delayed = _dot(shift_ref[...], ext_ref[...])
    acc = convb_ref[...] + convw_ref[n_delay:CONV_K, :] * u.astype(F32)
    for k in range(n_delay):
        acc = acc + convw_ref[k:k + 1, :] * delayed[k * L:(k + 1) * L, :]
    xbc = _silu(acc)
    xs = xbc[:, :d_ssm]
    b_all = xbc[:, d_ssm:d_ssm + SSM_GROUPS * D_STATE].astype(BF16)
    c_all = xbc[:, d_ssm + SSM_GROUPS * D_STATE:].astype(BF16)

    row = lax.broadcasted_iota(jnp.int32, (L, L), 0)
    col = lax.broadcasted_iota(jnp.int32, (L, L), 1)
    causal = row >= col
    tril = jnp.where(causal, 1.0, 0.0).astype(BF16)
    pre = small_ref[...] + bias_ref[...]
    sp = jnp.maximum(pre, 0.0) + jnp.log(1.0 + jnp.exp(-jnp.abs(pre)))
    lane = lax.broadcasted_iota(jnp.int32, (L, LANES), 1)
    is_dt = lane < 16
    dt = jnp.where(is_dt, sp, 0.0)
    logf = jnp.where((lane >= 16) & (lane < 32), pre - sp, 0.0)
    a = dt * arow_ref[...]
    a_and_f = a + logf
    cum_all = _dot3_tn(tril, a_and_f)
    cum = jnp.where(is_dt, cum_all, 0.0)
    cum_t = cum.T
    c_run = cum_all + carry_ref[0:1, :]
    c_ref[...] = c_run
    carry_ref[0:1, :] = c_run[L - 1:L, :]

    expand = expand_ref[...]
    dt_x = _dot2(dt, expand)
    ecum_x = _dot2(jnp.exp(cum), expand)
    dte_x = _dot2(jnp.exp(cum[L - 1:L, :] - cum), expand)
    xdt = xs * dt_x
    xdt_b = xdt.astype(BF16)
    w_b = (xdt * dte_x).astype(BF16)

    y_parts = []
    for g in range(SSM_GROUPS):
        bg = b_all[:, g * D_STATE:(g + 1) * D_STATE]
        cg = c_all[:, g * D_STATE:(g + 1) * D_STATE]
        cb = lax.dot_general(cg, bg, (((1,), (1,)), ((), ())), preferred_element_type=F32)
        s_old = state_ref[g]
        y_off = _dot(cg, s_old.astype(BF16)) * ecum_x[:, g * gw:(g + 1) * gw]
        upd = lax.dot_general(bg, w_b[:, g * gw:(g + 1) * gw], (((0,), (0,)), ((), ())),
                              preferred_element_type=F32)
        state_ref[g] = s_old * ecum_x[L - 1:L, g * gw:(g + 1) * gw] + upd
        pairs_per_group = n_pairs // SSM_GROUPS
        for pp in range(pairs_per_group):
            p = g * pairs_per_group + pp
            xp = xdt_b[:, p * LANES:(p + 1) * LANES]
            ys = []
            for hh in range(2):
                h = 2 * p + hh
                seg = cum[:, h:h + 1] - cum_t[h:h + 1, :]
                decay = jnp.exp(jnp.where(causal, seg, NEG_BIG))
                ys.append(_dot((cb * decay).astype(BF16), xp))
            first = lax.broadcasted_iota(jnp.int32, (L, LANES), 1) < HEAD_DIM
            y_parts.append(jnp.where(first, ys[0], ys[1])
                           + y_off[:, pp * LANES:(pp + 1) * LANES])
    y = jnp.concatenate(y_parts, axis=1) + xs * dskip_ref[...]
    gated = y * _silu(z_ref[...].astype(F32))
    ms = jnp.mean(gated * gated, axis=-1, keepdims=True)
    y_ref[...] = (gated * lax.rsqrt(ms + EPS) * nw_ref[...]).astype(BF16)


def _ssd(proj, z_off, xbc_off, cdim, d_ssm, small, convw, convb, bias_row, a_row, dskip_row, nw,
         expand, batch, seq):
    t = proj.shape[0]
    L = SSD_CHUNK
    nc = seq // L
    wx = 512
    assert cdim % wx == 0 and xbc_off % wx == 0 and z_off % d_ssm == 0
    n_xbc = cdim // wx
    rowmap = lambda b, c: (b * nc + c, 0)
    const = lambda b, c: (0, 0)
    n_delay = CONV_K - 1
    r = jnp.arange(n_delay * L)
    target = L + r % L - (n_delay - r // L)
    shift = (jnp.arange(2 * L)[None, :] == target[:, None]).astype(BF16)
    xbc_specs = [pl.BlockSpec((L, wx), lambda b, c, o=xbc_off // wx + i: (b * nc + c, o))
                 for i in range(n_xbc)]
    return pl.pallas_call(
        functools.partial(_ssd_kernel, d_ssm, n_xbc),
        grid=(batch, nc),
        in_specs=xbc_specs + [
            pl.BlockSpec((L, d_ssm), lambda b, c: (b * nc + c, z_off // d_ssm)),
            pl.BlockSpec((L, LANES), rowmap),
            pl.BlockSpec((CONV_K, cdim), const),
            pl.BlockSpec((1, cdim), const),
            pl.BlockSpec((1, LANES), const),
            pl.BlockSpec((1, LANES), const),
            pl.BlockSpec((1, d_ssm), const),
            pl.BlockSpec((1, d_ssm), const),
            pl.BlockSpec((LANES, d_ssm), const),
            pl.BlockSpec((n_delay * L, 2 * L), const),
        ],
        out_specs=[
            pl.BlockSpec((L, d_ssm), rowmap),
            pl.BlockSpec((L, LANES), rowmap),
        ],
        out_shape=[
            jax.ShapeDtypeStruct((t, d_ssm), BF16),
            jax.ShapeDtypeStruct((t, LANES), F32),
        ],
        scratch_shapes=[
            pltpu.VMEM((2 * L, cdim), BF16),
            pltpu.VMEM((SSM_GROUPS, D_STATE, d_ssm // SSM_GROUPS), F32),
            pltpu.VMEM((8, LANES), F32),
        ],
        compiler_params=_cparams(("parallel", "arbitrary")),
        name="ssd",
    )(*([proj] * n_xbc), proj, small, convw, convb, bias_row, a_row, dskip_row, nw, expand, shift)


def _headnorm(x, w_row, same_head):
    x2 = x * x
    hi = x2.astype(BF16)
    lo = (x2 - hi.astype(F32)).astype(BF16)
    ms = (_dot(hi, same_head) + _dot(lo, same_head)) * (1.0 / HEAD_DIM)
    return x * lax.rsqrt(ms + EPS) * w_row


def _place(parts, moves, rr, cc):
    out = None
    for i, part in enumerate(parts):
        sel = None
        for src, dst in moves:
            one = jnp.where((rr == src) & (cc == dst + i), 1.0, 0.0)
            sel = one if sel is None else sel + one
        term = _dot(part, sel.astype(BF16))
        out = term if out is None else out + term
    return out


def _attn_kernel(tq, npair, q_ref, k_ref, v_ref, c_ref, qw_ref, kw_ref, o_ref,
                 kx_ref, vx_ref, qx_ref, sa_ref, sb_ref, m_ref, acc_ref):
    grp = pl.program_id(1)
    qi = pl.program_id(2)
    seq = k_ref.shape[1]
    lane = lax.broadcasted_iota(jnp.int32, (1, LANES), 1)
    first = lane < HEAD_DIM
    rr = lax.broadcasted_iota(jnp.int32, (LANES, LANES), 0)
    cc = lax.broadcasted_iota(jnp.int32, (LANES, LANES), 1)
    same_head = jnp.where((rr < HEAD_DIM) == (cc < HEAD_DIM), 1.0, 0.0).astype(BF16)
    pairs = range(npair)
    bases = [16 + 2 * (grp * npair + p) for p in pairs]

    @pl.when(qi == 0)
    def _():
        k_const = jnp.where((lane >= 6) & (lane < 9), 1.0, 0.0)

        def body(r, carry):
            rows = pl.ds(pl.multiple_of(r * tq, tq), tq)
            parts = _split3(c_ref[0, rows, :] * LOG2E)
            for p in pairs:
                lanes = slice(p * LANES, (p + 1) * LANES)
                kn = _headnorm(k_ref[0, rows, lanes].astype(F32), kw_ref[...], same_head)
                kext = _place(parts, [(bases[p], 0), (bases[p] + 1, 3)], rr, cc) + k_const
                kx_ref[p, rows, 0:LANES] = kn.astype(BF16)
                kx_ref[p, rows, LANES:2 * LANES] = kext.astype(BF16)
                v = v_ref[0, rows, lanes].astype(F32)
                vx_ref[p, 0, :, rows] = jnp.where(first, v, 1.0).T.astype(BF16)
                vx_ref[p, 1, :, rows] = jnp.where(first, 1.0, v).T.astype(BF16)
            return carry
        lax.fori_loop(0, seq // tq, body, 0)

    scale = (1.0 / np.sqrt(HEAD_DIM)) * LOG2E
    q_rows = pl.ds(pl.multiple_of(qi * tq, tq), tq)
    q_parts = _split3(c_ref[0, q_rows, :] * LOG2E)
    for p in pairs:
        qn = _headnorm(q_ref[0, :, p * LANES:(p + 1) * LANES].astype(F32), qw_ref[...],
                       same_head) * scale
        for hh in range(2):
            q_const = jnp.where((lane >= 3 * hh) & (lane < 3 * hh + 3), -1.0, 0.0)
            qext = _place(q_parts, [(bases[p] + hh, 6)], rr, cc) + q_const
            qh = jnp.where(first, qn, 0.0) if hh == 0 else jnp.where(first, 0.0, qn)
            qx_ref[p, 0:LANES, hh * tq:(hh + 1) * tq] = qh.T.astype(BF16)
            qx_ref[p, LANES:2 * LANES, hh * tq:(hh + 1) * tq] = qext.T.astype(BF16)
        m_ref[p] = jnp.full((1, 2 * tq), NEG_BIG, F32)
        acc_ref[p] = jnp.zeros((2, LANES, tq), F32)

    key_i = lax.broadcasted_iota(jnp.int32, (tq, 2 * tq), 0)
    qry_i = lax.broadcasted_iota(jnp.int32, (tq, 2 * tq), 1)
    keep = jnp.where(qry_i >= tq, qry_i - tq, qry_i) >= key_i

    def scores(j, s_ref):
        rows = pl.ds(pl.multiple_of(j * tq, tq), tq)
        for p in pairs:
            s_ref[p] = _dot(kx_ref[p, rows, :], qx_ref[p])

    def softmax_pv(j, s_ref, masked):
        cols = pl.ds(pl.multiple_of(j * tq, tq), tq)
        for p in pairs:
            s = s_ref[p]
            if masked:
                s = jnp.where(keep, s, NEG_BIG)
            m = m_ref[p]
            m_new = jnp.maximum(m, jnp.max(s, axis=0, keepdims=True))
            alpha = jnp.exp2(m - m_new)
            pb = jnp.exp2(s - m_new).astype(BF16)
            acc_ref[p, 0] = alpha[:, :tq] * acc_ref[p, 0] + _dot(vx_ref[p, 0, :, cols], pb[:, :tq])
            acc_ref[p, 1] = alpha[:, tq:] * acc_ref[p, 1] + _dot(vx_ref[p, 1, :, cols], pb[:, tq:])
            m_ref[p] = m_new

    scores(0, sa_ref)

    def two_blocks(i, carry):
        j = 2 * i
        scores(j + 1, sb_ref)
        softmax_pv(j, sa_ref, False)
        scores(j + 2, sa_ref)
        softmax_pv(j + 1, sb_ref, False)
        return carry
    lax.fori_loop(0, qi // 2, two_blocks, 0)

    @pl.when(qi % 2 == 0)
    def _():
        softmax_pv(qi, sa_ref, True)

    @pl.when(qi % 2 == 1)
    def _():
        scores(qi, sb_ref)
        softmax_pv(qi - 1, sa_ref, False)
        softmax_pv(qi, sb_ref, True)

    for p in pairs:
        out_t = jnp.concatenate(
            [acc_ref[p, 0, 0:HEAD_DIM, :] / acc_ref[p, 0, HEAD_DIM:LANES, :],
             acc_ref[p, 1, HEAD_DIM:LANES, :] / acc_ref[p, 1, 0:HEAD_DIM, :]], axis=0)
        o_ref[0, :, p * LANES:(p + 1) * LANES] = out_t.T.astype(BF16)


def _attn(proj, q_off, k_off, v_off, d_att, c, qw_row, kw_row, batch, seq, tq, npair):
    width = npair * LANES
    assert d_att % width == 0 and seq % tq == 0
    assert q_off % width == 0 and k_off % width == 0 and v_off % width == 0
    qo, ko, vo = q_off // width, k_off // width, v_off // width
    q3 = k3 = v3 = proj.reshape(batch, seq, proj.shape[-1])
    c3 = c.reshape(batch, seq, LANES)
    return pl.pallas_call(
        functools.partial(_attn_kernel, tq, npair),
        grid=(batch, d_att // width, seq // tq),
        in_specs=[
            pl.BlockSpec((1, tq, width), lambda b, h, i: (b, i, qo + h)),
            pl.BlockSpec((1, seq, width), lambda b, h, i: (b, 0, ko + h)),
            pl.BlockSpec((1, seq, width), lambda b, h, i: (b, 0, vo + h)),
            pl.BlockSpec((1, seq, LANES), lambda b, h, i: (b, 0, 0)),
            pl.BlockSpec((1, LANES), lambda b, h, i: (0, 0)),
            pl.BlockSpec((1, LANES), lambda b, h, i: (0, 0)),
        ],
        out_specs=pl.BlockSpec((1, tq, width), lambda b, h, i: (b, i, h)),
        out_shape=jax.ShapeDtypeStruct((batch, seq, d_att), BF16),
        scratch_shapes=[
            pltpu.VMEM((npair, seq, 2 * LANES), BF16),
            pltpu.VMEM((npair, 2, LANES, seq), BF16),
            pltpu.VMEM((npair, 2 * LANES, 2 * tq), BF16),
            pltpu.VMEM((npair, tq, 2 * tq), F32),
            pltpu.VMEM((npair, tq, 2 * tq), F32),
            pltpu.VMEM((npair, 1, 2 * tq), F32),
            pltpu.VMEM((npair, 2, LANES, tq), F32),
        ],
        compiler_params=_cparams(("parallel", "parallel", "arbitrary")),
        name="attn",
    )(q3, k3, v3, c3, qw_row, kw_row).reshape(batch * seq, d_att)


def _outproj_kernel(ys_ref, ya_ref, x_ref, wos_ref, woa_ref, aw_ref, fw_ref, wr_ref, br_ref,
                    x1_ref, xn_ref, gates_ref):
    ya = ya_ref[...].astype(F32)
    ya = ya * lax.rsqrt(jnp.mean(ya * ya, axis=-1, keepdims=True) + EPS) * aw_ref[...]
    x1 = x_ref[...] + _dot(ys_ref[...], wos_ref[...]) + _dot(ya.astype(BF16), woa_ref[...])
    x1_ref[...] = x1
    xn = x1 * lax.rsqrt(jnp.mean(x1 * x1, axis=-1, keepdims=True) + EPS) * fw_ref[...]
    xn_ref[...] = xn.astype(BF16)

    x_hi = xn.astype(BF16)
    x_lo = (xn - x_hi.astype(F32)).astype(BF16)
    w = wr_ref[...]
    w_hi = w.astype(BF16)
    w_lo = (w - w_hi.astype(F32)).astype(BF16)
    logits = _dot(x_hi, w_hi) + _dot(x_lo, w_hi) + _dot(x_hi, w_lo) + br_ref[...]
    lane = lax.broadcasted_iota(jnp.int32, logits.shape, 1).astype(F32)

    def rmax(mask):
        return jnp.max(jnp.where(mask, logits, NEG_BIG), axis=-1, keepdims=True)

    def first_at(mask, val):
        return jnp.min(jnp.where(mask & (logits == val), lane, 1e9), axis=-1, keepdims=True)

    gmask = (lane >= N_EXPERTS) & (lane < N_EXPERTS + N_EXPERT_GROUPS)
    gmax = rmax(gmask)
    p_g = 1.0 / jnp.sum(jnp.where(gmask, jnp.exp(logits - gmax), 0.0), axis=-1, keepdims=True)
    g_star = first_at(gmask, gmax) - N_EXPERTS
    lo = g_star * EXPERTS_PER_GROUP
    emask = (lane >= lo) & (lane < lo + EXPERTS_PER_GROUP)
    v1 = rmax(emask)
    i1 = first_at(emask, v1)
    mask2 = emask & (lane != i1)
    v2 = rmax(mask2)
    i2 = first_at(mask2, v2)
    e2 = jnp.exp(v2 - v1)
    w1 = p_g / (1.0 + e2)
    w2 = p_g * e2 / (1.0 + e2)
    gates_ref[...] = (jnp.where(lane == i1, w1, 0.0) + jnp.where(lane == i2, w2, 0.0)
                      + jnp.where(lane == N_EXPERTS, g_star, 0.0))


def _outproj(ys, ya, x, wos, woa, aw, fw, wr, br, tm):
    t, d = x.shape
    dm = ys.shape[1]
    rows = lambda i: (i, 0)
    const = lambda i: (0, 0)
    return pl.pallas_call(
        _outproj_kernel,
        grid=(t // tm,),
        in_specs=[
            pl.BlockSpec((tm, dm), rows),
            pl.BlockSpec((tm, dm), rows),
            pl.BlockSpec((tm, d), rows),
            pl.BlockSpec((dm, d), const),
            pl.BlockSpec((dm, d), const),
            pl.BlockSpec((1, dm), const),
            pl.BlockSpec((1, d), const),
            pl.BlockSpec((d, LANES), const),
            pl.BlockSpec((1, LANES), const),
        ],
        out_specs=[
            pl.BlockSpec((tm, d), rows),
            pl.BlockSpec((tm, d), rows),
            pl.BlockSpec((tm, LANES), rows),
        ],
        out_shape=[
            jax.ShapeDtypeStruct((t, d), F32),
            jax.ShapeDtypeStruct((t, d), BF16),
            jax.ShapeDtypeStruct((t, LANES), F32),
        ],
        compiler_params=_cparams(("parallel",)),
        name="outproj",
    )(ys, ya, x, wos, woa, aw, fw, wr, br)


MOE_BLOCK = 1024
MOE_CHUNK = 128
MOE_ROWS = MOE_BLOCK + N_EXPERT_GROUPS * MOE_CHUNK
GROUP_LANE = N_EXPERTS


def _group_runs(count):
    nch = jnp.floor((count + (MOE_CHUNK - 1)) * (1.0 / MOE_CHUNK))
    return nch, nch * MOE_CHUNK


def _moe_dispatch_kernel(xn_ref, route_ref, before_ref, xs_ref, gs_ref, meta_ref):
    pt = xn_ref.shape[0]
    route = route_ref[...]
    rt = route.T
    g_row = rt[GROUP_LANE:GROUP_LANE + 1, :]
    grp = lax.broadcasted_iota(jnp.int32, (8, pt), 0).astype(F32)
    onehot = jnp.where(grp == g_row, 1.0, 0.0)
    rank = _dot(onehot.astype(BF16), before_ref[...])
    nch, padded = _group_runs(jnp.sum(onehot, axis=1, keepdims=True))
    gi = lax.broadcasted_iota(jnp.int32, (8, 1), 0)
    start = jnp.zeros((8, 1), F32)
    for g in range(N_EXPERT_GROUPS - 1):
        start = start + jnp.where(gi > g, padded[g:g + 1, :], 0.0)
    pos = jnp.sum(onehot * (start + rank), axis=0, keepdims=True)
    dest = lax.broadcasted_iota(jnp.int32, (MOE_ROWS, pt), 0).astype(F32)
    perm = jnp.where(dest == pos, 1.0, 0.0).astype(BF16)
    g_hi = route.astype(BF16)
    g_lo = (route - g_hi.astype(F32)).astype(BF16)
    moved = _dot(perm, jnp.concatenate([xn_ref[...], g_hi, g_lo], axis=1))
    d = xn_ref.shape[1]
    xs_ref[0] = moved[:, :d].astype(BF16)
    gs_ref[0] = moved[:, d:d + LANES] + moved[:, d + LANES:]
    lane = lax.broadcasted_iota(jnp.int32, (8, LANES), 1)
    meta_ref[0] = jnp.where(lane == 0, start, jnp.where(lane == 1, nch, 0.0))


def _moe_expert_kernel(n_sub, n_exp, meta_ref, xs_ref, gs_ref, wg_ref, wu_ref, wd_ref, ys_ref):
    st = pl.program_id(0)
    eb = pl.program_id(1)
    g = (eb * n_exp) // EXPERTS_PER_GROUP

    @pl.when(eb == 0)
    def _():
        ys_ref[...] = jnp.zeros(ys_ref.shape, F32)

    def run(s, first_row, n_rows):
        rows = pl.ds(pl.multiple_of(first_row, MOE_CHUNK), n_rows)
        lane = lax.broadcasted_iota(jnp.int32, (n_rows, LANES), 1)
        xc = xs_ref[s, rows, :]
        gates = gs_ref[s, rows, :]
        y = ys_ref[s, rows, :]
        for k in range(n_exp):
            gate = jnp.sum(jnp.where(lane == eb * n_exp + k, gates, 0.0), axis=-1, keepdims=True)
            hid = _silu(_dot(xc, wg_ref[k])) * _dot(xc, wu_ref[k]) * gate
            y = y + _dot(hid.astype(BF16), wd_ref[k])
        ys_ref[s, rows, :] = y

    for s in range(n_sub):
        base = ((st * n_sub + s) * N_EXPERT_GROUPS + g) * 2
        start = meta_ref[base]
        n_chunks = meta_ref[base + 1]

        def two_chunks(c, carry, s=s, start=start):
            run(s, start + c * (2 * MOE_CHUNK), 2 * MOE_CHUNK)
            return carry
        lax.fori_loop(0, n_chunks // 2, two_chunks, 0)

        @pl.when(n_chunks % 2 == 1)
        def _(s=s, start=start, n_chunks=n_chunks):
            run(s, start + (n_chunks - 1) * MOE_CHUNK, MOE_CHUNK)


def _moe_combine_kernel(ys_ref, route_ref, x1_ref, before_ref, o_ref):
    pt = x1_ref.shape[0]
    route = route_ref[...]
    lane = lax.broadcasted_iota(jnp.int32, (pt, LANES), 1).astype(F32)
    onehot = jnp.where(lane == route[:, GROUP_LANE:GROUP_LANE + 1], 1.0, 0.0)
    rank = _dot(before_ref[...], onehot.astype(BF16))
    _, padded = _group_runs(jnp.sum(onehot, axis=0, keepdims=True))
    lane1 = lax.broadcasted_iota(jnp.int32, (1, LANES), 1)
    start = jnp.zeros((1, LANES), F32)
    for g in range(N_EXPERT_GROUPS - 1):
        start = start + jnp.where(lane1 > g, padded[:, g:g + 1], 0.0)
    pos = jnp.sum(onehot * (start + rank), axis=-1, keepdims=True)
    src = lax.broadcasted_iota(jnp.int32, (pt, MOE_ROWS), 1).astype(F32)
    perm_t = jnp.where(src == pos, 1.0, 0.0).astype(BF16)
    o_ref[...] = x1_ref[...] + _dot(perm_t, ys_ref[0].astype(BF16))


def _moe(xn, route, x1, wg, wu, wd, n_sub):
    t, d = x1.shape
    n_e, _, f = wg.shape
    pt = MOE_BLOCK
    assert t % (pt * n_sub) == 0
    nb = t // pt
    idx = jnp.arange(pt)
    before_row = (idx[:, None] < idx[None, :]).astype(BF16)
    before_col = (idx[None, :] < idx[:, None]).astype(BF16)
    rows = lambda i: (i, 0)
    const = lambda i: (0, 0)
    xs, gs, meta = pl.pallas_call(
        _moe_dispatch_kernel,
        grid=(nb,),
        in_specs=[
            pl.BlockSpec((pt, d), rows),
            pl.BlockSpec((pt, LANES), rows),
            pl.BlockSpec((pt, pt), const),
        ],
        out_specs=[
            pl.BlockSpec((1, MOE_ROWS, d), lambda i: (i, 0, 0)),
            pl.BlockSpec((1, MOE_ROWS, LANES), lambda i: (i, 0, 0)),
            pl.BlockSpec((1, 8, LANES), lambda i: (i, 0, 0)),
        ],
        out_shape=[
            jax.ShapeDtypeStruct((nb, MOE_ROWS, d), BF16),
            jax.ShapeDtypeStruct((nb, MOE_ROWS, LANES), F32),
            jax.ShapeDtypeStruct((nb, 8, LANES), F32),
        ],
        compiler_params=_cparams(("parallel",)),
        name="moe_dispatch",
    )(xn, route, before_row)

    meta_i = meta[:, :N_EXPERT_GROUPS, :2].astype(jnp.int32).reshape(-1)
    n_exp = 2
    assert EXPERTS_PER_GROUP % n_exp == 0
    ys = pl.pallas_call(
        functools.partial(_moe_expert_kernel, n_sub, n_exp),
        grid_spec=pltpu.PrefetchScalarGridSpec(
            num_scalar_prefetch=1,
            grid=(nb // n_sub, n_e // n_exp),
            in_specs=[
                pl.BlockSpec((n_sub, MOE_ROWS, d), lambda i, e, m: (i, 0, 0)),
                pl.BlockSpec((n_sub, MOE_ROWS, LANES), lambda i, e, m: (i, 0, 0)),
                pl.BlockSpec((n_exp, d, f), lambda i, e, m: (e, 0, 0)),
                pl.BlockSpec((n_exp, d, f), lambda i, e, m: (e, 0, 0)),
                pl.BlockSpec((n_exp, f, d), lambda i, e, m: (e, 0, 0)),
            ],
            out_specs=pl.BlockSpec((n_sub, MOE_ROWS, d), lambda i, e, m: (i, 0, 0)),
        ),
        out_shape=jax.ShapeDtypeStruct((nb, MOE_ROWS, d), F32),
        compiler_params=_cparams(("parallel", "arbitrary")),
        name="moe_experts",
    )(meta_i, xs, gs, wg, wu, wd)

    return pl.pallas_call(
        _moe_combine_kernel,
        grid=(nb,),
        in_specs=[
            pl.BlockSpec((1, MOE_ROWS, d), lambda i: (i, 0, 0)),
            pl.BlockSpec((pt, LANES), rows),
            pl.BlockSpec((pt, d), rows),
            pl.BlockSpec((pt, pt), const),
        ],
        out_specs=pl.BlockSpec((pt, d), rows),
        out_shape=jax.ShapeDtypeStruct((t, d), F32),
        compiler_params=_cparams(("parallel",)),
        name="moe_combine",
    )(ys, route, x1, before_col)


def _pad_lanes(v, offset=0):
    out = jnp.zeros((1, LANES), F32)
    return out.at[0, offset:offset + v.shape[0]].set(v.astype(F32))


def kernel(x, norm_mix_w, w_in, conv_w, conv_b, dt_bias, a_log, d_skip, ssm_norm_w, q_norm_w,
           k_norm_w, fox_f_bias, att_out_norm_w, w_out, norm_ffn_w, w_router_group,
           b_router_group, w_router_expert, b_router_expert, w_expert_gate, w_expert_up,
           w_expert_down):
    batch, seq, d = x.shape
    depth = w_in.shape[0]
    d_ssm = ssm_norm_w.shape[1]
    d_att = att_out_norm_w.shape[1]
    h_ssm = dt_bias.shape[1]
    h_att = fox_f_bias.shape[1]
    conv_dim = conv_w.shape[2]
    assert h_ssm == 16 and h_att == 16 and d_ssm // h_ssm == HEAD_DIM and d_att // h_att == HEAD_DIM
    assert conv_dim == d_ssm + 2 * SSM_GROUPS * D_STATE
    off_xbc = d_ssm
    off_dt = off_xbc + conv_dim
    off_q = off_dt + h_ssm
    off_k = off_q + d_att
    off_v = off_k + d_att
    off_f = off_v + d_att
    t = batch * seq
    tm_in = min(512, t)
    tm_out = min(512, t)
    moe_sub = 2 if t % (2 * MOE_BLOCK) == 0 else 1
    tq = min(256, seq)

    head_of_chan = jnp.arange(d_ssm) // HEAD_DIM
    expand = (jnp.arange(LANES)[:, None] == head_of_chan[None, :]).astype(BF16)

    xf = x.reshape(t, d)
    for l in range(depth):
        wl = w_in[l]
        w_main = jnp.concatenate(
            [wl[:, :off_xbc], wl[:, off_q:off_k], wl[:, off_k:off_v], wl[:, off_v:off_f],
             wl[:, off_xbc:off_dt]], axis=1).astype(BF16)
        p_q, p_k, p_v, p_xbc = d_ssm, d_ssm + d_att, d_ssm + 2 * d_att, d_ssm + 3 * d_att
        w_small = jnp.zeros((d, LANES), F32)
        w_small = w_small.at[:, 0:h_ssm].set(wl[:, off_dt:off_q])
        w_small = w_small.at[:, 16:16 + h_att].set(wl[:, off_f:off_f + h_att]).astype(BF16)
        proj, small = _inproj(xf, norm_mix_w[l][None, :], w_main, w_small, tm_in, 512)

        bias_row = _pad_lanes(dt_bias[l]) + _pad_lanes(fox_f_bias[l], 16)
        a_row = _pad_lanes(-jnp.exp(a_log[l].astype(F32)))
        dskip_row = jnp.repeat(d_skip[l].astype(F32), HEAD_DIM)[None, :]
        y_ssm, c = _ssd(proj, 0, p_xbc, conv_dim, d_ssm, small, conv_w[l].astype(F32),
                        conv_b[l].astype(F32)[None, :], bias_row, a_row, dskip_row,
                        ssm_norm_w[l][None, :], expand, batch, seq)

        qw_row = jnp.tile(q_norm_w[l].astype(F32), 2)[None, :]
        kw_row = jnp.tile(k_norm_w[l].astype(F32), 2)[None, :]
        y_att = _attn(proj, p_q, p_k, p_v, d_att, c, qw_row, kw_row, batch, seq, tq, 2)

        wr = jnp.zeros((d, LANES), F32)
        wr = wr.at[:, :N_EXPERTS].set(w_router_expert[l])
        wr = wr.at[:, N_EXPERTS:N_EXPERTS + N_EXPERT_GROUPS].set(w_router_group[l])
        br = _pad_lanes(b_router_expert[l]) + _pad_lanes(b_router_group[l], N_EXPERTS)
        wo = w_out[l].astype(BF16)
        x1, xn, gates = _outproj(y_ssm, y_att, xf, wo[:d_ssm], wo[d_ssm:],
                                 att_out_norm_w[l][None, :], norm_ffn_w[l][None, :], wr, br, tm_out)

        xf = _moe(xn, gates, x1, w_expert_gate[l].astype(BF16), w_expert_up[l].astype(BF16),
                  w_expert_down[l].astype(BF16), moe_sub)
    return xf.reshape(batch, seq, d)
```

```python
import functools

import jax
import jax.numpy as jnp
import numpy as np
from jax import lax
from jax.experimental import pallas as pl
from jax.experimental.pallas import tpu as pltpu

F32 = jnp.float32
BF16 = jnp.bfloat16

HEAD_DIM = 64
D_STATE = 128
SSM_GROUPS = 2
CONV_K = 4
SSD_CHUNK = 128
N_EXPERT_GROUPS = 4
EXPERTS_PER_GROUP = 8
N_EXPERTS = N_EXPERT_GROUPS * EXPERTS_PER_GROUP
EPS = 1e-6
LANES = 128
NEG_BIG = -1e30
LOG2E = 1.4426950408889634
VMEM_LIMIT = 56 * 1024 * 1024


def _cparams(sem):
    return pltpu.CompilerParams(dimension_semantics=sem, vmem_limit_bytes=VMEM_LIMIT)


def _split3(a):
    a1 = a.astype(BF16)
    r1 = a - a1.astype(F32)
    a2 = r1.astype(BF16)
    a3 = (r1 - a2.astype(F32)).astype(BF16)
    return a1, a2, a3


def _dot(a, b):
    return jnp.dot(a, b, preferred_element_type=F32)


def _dot3(a_f32, b_bf16):
    a1, a2, a3 = _split3(a_f32)
    return _dot(a1, b_bf16) + _dot(a2, b_bf16) + _dot(a3, b_bf16)


def _dot2(a_f32, b_bf16):
    a1 = a_f32.astype(BF16)
    a2 = (a_f32 - a1.astype(F32)).astype(BF16)
    return _dot(a1, b_bf16) + _dot(a2, b_bf16)


def _dot3_tn(b_bf16_t, a_f32):
    a1, a2, a3 = _split3(a_f32)
    return _dot(b_bf16_t, a1) + _dot(b_bf16_t, a2) + _dot(b_bf16_t, a3)


def _silu(x):
    return x / (1.0 + jnp.exp(-x))


def _inproj_kernel(tn, x_ref, nw_ref, w_ref, ws_ref, proj_ref, small_ref):
    x = x_ref[...]
    ms = jnp.mean(x * x, axis=-1, keepdims=True)
    hb = (x * lax.rsqrt(ms + EPS) * nw_ref[...]).astype(BF16)
    small_ref[...] = _dot(hb, ws_ref[...])
    for j in range(w_ref.shape[1] // tn):
        cols = slice(j * tn, (j + 1) * tn)
        proj_ref[:, cols] = _dot(hb, w_ref[:, cols]).astype(BF16)


def _inproj(x2, nw, w_main, w_small, tm, tn):
    t, d = x2.shape
    n = w_main.shape[1]
    assert n % tn == 0 and t % tm == 0
    return pl.pallas_call(
        functools.partial(_inproj_kernel, tn),
        grid=(t // tm,),
        in_specs=[
            pl.BlockSpec((tm, d), lambda i: (i, 0)),
            pl.BlockSpec((1, d), lambda i: (0, 0)),
            pl.BlockSpec((d, n), lambda i: (0, 0)),
            pl.BlockSpec((d, LANES), lambda i: (0, 0)),
        ],
        out_specs=[
            pl.BlockSpec((tm, n), lambda i: (i, 0)),
            pl.BlockSpec((tm, LANES), lambda i: (i, 0)),
        ],
        out_shape=[
            jax.ShapeDtypeStruct((t, n), BF16),
            jax.ShapeDtypeStruct((t, LANES), F32),
        ],
        compiler_params=_cparams(("parallel",)),
        name="inproj",
    )(x2, nw, w_main, w_small)


def _ssd_kernel(d_ssm, n_xbc, *refs):
    xbc_refs = refs[:n_xbc]
    (z_ref, small_ref, convw_ref, convb_ref, bias_ref, arow_ref, dskip_ref, nw_ref, expand_ref,
     shift_ref, y_ref, c_ref, ct_ref, ext_ref, state_ref, carry_ref) = refs[n_xbc:]
    L = SSD_CHUNK
    n_pairs = d_ssm // LANES
    gw = d_ssm // SSM_GROUPS
    c_idx = pl.program_id(1)

    @pl.when(c_idx == 0)
    def _():
        ext_ref[L:2 * L, :] = jnp.zeros((L, ext_ref.shape[1]), BF16)
        state_ref[...] = jnp.zeros(state_ref.shape, F32)
        carry_ref[...] = jnp.zeros(carry_ref.shape, F32)

    u = jnp.concatenate([r[...] for r in xbc_refs], axis=1)
    ext_ref[0:L, :] = ext_ref[L:2 * L, :]
    ext_ref[L:2 * L, :] = u
    n_delay = CONV_K - 1
    delayed = _dot(shift_ref[...], ext_ref[...])
    acc = convb_ref[...] + convw_ref[n_delay:CONV_K, :] * u.astype(F32)
    for k in range(n_delay):
        acc = acc + convw_ref[k:k + 1, :] * delayed[k * L:(k + 1) * L, :]
    xbc = _silu(acc)
    xs = xbc[:, :d_ssm]
    b_all = xbc[:, d_ssm:d_ssm + SSM_GROUPS * D_STATE].astype(BF16)
    c_all = xbc[:, d_ssm + SSM_GROUPS * D_STATE:].astype(BF16)

    row = lax.broadcasted_iota(jnp.int32, (L, L), 0)
    col = lax.broadcasted_iota(jnp.int32, (L, L), 1)
    causal = row >= col
    tril = jnp.where(causal, 1.0, 0.0).astype(BF16)
    pre = small_ref[...] + bias_ref[...]
    sp = jnp.maximum(pre, 0.0) + jnp.log(1.0 + jnp.exp(-jnp.abs(pre)))
    lane = lax.broadcasted_iota(jnp.int32, (L, LANES), 1)
    is_dt = lane < 16
    dt = jnp.where(is_dt, sp, 0.0)
    logf = jnp.where((lane >= 16) & (lane < 32), pre - sp, 0.0)
    a = dt * arow_ref[...]
    a_and_f = a + logf
    cum_all = _dot3_tn(tril, a_and_f)
    cum = jnp.where(is_dt, cum_all, 0.0)
    cum_t = cum.T
    c_run = cum_all + carry_ref[0:1, :]
    c_ref[...] = c_run
    ct_ref[0] = c_run.T
    carry_ref[0:1, :] = c_run[L - 1:L, :]

    expand = expand_ref[...]
    dt_x = _dot2(dt, expand)
    ecum_x = _dot2(jnp.exp(cum), expand)
    dte_x = _dot2(jnp.exp(cum[L - 1:L, :] - cum), expand)
    xdt = xs * dt_x
    xdt_b = xdt.astype(BF16)
    w_b = (xdt * dte_x).astype(BF16)

    y_parts = []
    for g in range(SSM_GROUPS):
        bg = b_all[:, g * D_STATE:(g + 1) * D_STATE]
        cg = c_all[:, g * D_STATE:(g + 1) * D_STATE]
        cb = lax.dot_general(cg, bg, (((1,), (1,)), ((), ())), preferred_element_type=F32)
        s_old = state_ref[g]
        y_off = _dot(cg, s_old.astype(BF16)) * ecum_x[:, g * gw:(g + 1) * gw]
        upd = lax.dot_general(bg, w_b[:, g * gw:(g + 1) * gw], (((0,), (0,)), ((), ())),
                              preferred_element_type=F32)
        state_ref[g] = s_old * ecum_x[L - 1:L, g * gw:(g + 1) * gw] + upd
        pairs_per_group = n_pairs // SSM_GROUPS
        for pp in range(pairs_per_group):
            p = g * pairs_per_group + pp
            xp = xdt_b[:, p * LANES:(p + 1) * LANES]
            ys = []
            for hh in range(2):
                h = 2 * p + hh
                seg = cum[:, h:h + 1] - cum_t[h:h + 1, :]
                decay = jnp.exp(jnp.where(causal, seg, NEG_BIG))
                ys.append(_dot((cb * decay).astype(BF16), xp))
            first = lax.broadcasted_iota(jnp.int32, (L, LANES), 1) < HEAD_DIM
            y_parts.append(jnp.where(first, ys[0], ys[1])
                           + y_off[:, pp * LANES:(pp + 1) * LANES])
    y = jnp.concatenate(y_parts, axis=1) + xs * dskip_ref[...]
    gated = y * _silu(z_ref[...].astype(F32))
    ms = jnp.mean(gated * gated, axis=-1, keepdims=True)
    y_ref[...] = (gated * lax.rsqrt(ms + EPS) * nw_ref[...]).astype(BF16)


def _ssd(proj, z_off, xbc_off, cdim, d_ssm, small, convw, convb, bias_row, a_row, dskip_row, nw,
         expand, batch, seq):
    t = proj.shape[0]
    L = SSD_CHUNK
    nc = seq // L
    wx = 512
    assert cdim % wx == 0 and xbc_off % wx == 0 and z_off % d_ssm == 0
    n_xbc = cdim // wx
    rowmap = lambda b, c: (b * nc + c, 0)
    const = lambda b, c: (0, 0)
    n_delay = CONV_K - 1
    r = jnp.arange(n_delay * L)
    target = L + r % L - (n_delay - r // L)
    shift = (jnp.arange(2 * L)[None, :] == target[:, None]).astype(BF16)
    xbc_specs = [pl.BlockSpec((L, wx), lambda b, c, o=xbc_off // wx + i: (b * nc + c, o))
                 for i in range(n_xbc)]
    return pl.pallas_call(
        functools.partial(_ssd_kernel, d_ssm, n_xbc),
        grid=(batch, nc),
        in_specs=xbc_specs + [
            pl.BlockSpec((L, d_ssm), lambda b, c: (b * nc + c, z_off // d_ssm)),
            pl.BlockSpec((L, LANES), rowmap),
            pl.BlockSpec((CONV_K, cdim), const),
            pl.BlockSpec((1, cdim), const),
            pl.BlockSpec((1, LANES), const),
            pl.BlockSpec((1, LANES), const),
            pl.BlockSpec((1, d_ssm), const),
            pl.BlockSpec((1, d_ssm), const),
            pl.BlockSpec((LANES, d_ssm), const),
            pl.BlockSpec((n_delay * L, 2 * L), const),
        ],
        out_specs=[
            pl.BlockSpec((L, d_ssm), rowmap),
            pl.BlockSpec((L, LANES), rowmap),
            pl.BlockSpec((1, LANES, L), lambda b, c: (b, 0, c)),
        ],
        out_shape=[
            jax.ShapeDtypeStruct((t, d_ssm), BF16),
            jax.ShapeDtypeStruct((t, LANES), F32),
            jax.ShapeDtypeStruct((batch, LANES, seq), F32),
        ],
        scratch_shapes=[
            pltpu.VMEM((2 * L, cdim), BF16),
            pltpu.VMEM((SSM_GROUPS, D_STATE, d_ssm // SSM_GROUPS), F32),
            pltpu.VMEM((8, LANES), F32),
        ],
        compiler_params=_cparams(("parallel", "arbitrary")),
        name="ssd",
    )(*([proj] * n_xbc), proj, small, convw, convb, bias_row, a_row, dskip_row, nw, expand, shift)


def _headnorm(x, w_row, same_head):
    x2 = x * x
    hi = x2.astype(BF16)
    lo = (x2 - hi.astype(F32)).astype(BF16)
    ms = (_dot(hi, same_head) + _dot(lo, same_head)) * (1.0 / HEAD_DIM)
    return x * lax.rsqrt(ms + EPS) * w_row


def _place(parts, moves, rr, cc):
    out = None
    for i, part in enumerate(parts):
        sel = None
        for src, dst in moves:
            one = jnp.where((rr == src) & (cc == dst + i), 1.0, 0.0)
            sel = one if sel is None else sel + one
        term = _dot(part, sel.astype(BF16))
        out = term if out is None else out + term
    return out


def _attn_kernel(tq, tk, npair, q_ref, k_ref, v_ref, c_ref, ct_ref, qw_ref, kw_ref, o_ref,
                 kx_ref, vx_ref, qx_ref, sa_ref, sb_ref, m_ref, acc_ref):
    grp = pl.program_id(1)
    qi = pl.program_id(2)
    seq = k_ref.shape[1]
    lane = lax.broadcasted_iota(jnp.int32, (1, LANES), 1)
    first = lane < HEAD_DIM
    rr = lax.broadcasted_iota(jnp.int32, (LANES, LANES), 0)
    cc = lax.broadcasted_iota(jnp.int32, (LANES, LANES), 1)
    same_head = jnp.where((rr < HEAD_DIM) == (cc < HEAD_DIM), 1.0, 0.0).astype(BF16)
    pairs = range(npair)
    bases = [16 + 2 * (grp * npair + p) for p in pairs]

    @pl.when(qi == 0)
    def _():
        k_const = jnp.where((lane >= 6) & (lane < 9), 1.0, 0.0)

        def body(r, carry):
            rows = pl.ds(pl.multiple_of(r * tq, tq), tq)
            parts = _split3(c_ref[0, rows, :] * LOG2E)
            for p in pairs:
                lanes = slice(p * LANES, (p + 1) * LANES)
                kn = _headnorm(k_ref[0, rows, lanes].astype(F32), kw_ref[...], same_head)
                kext = _place(parts, [(bases[p], 0), (bases[p] + 1, 3)], rr, cc) + k_const
                kx_ref[p, rows, 0:LANES] = kn.astype(BF16)
                kx_ref[p, rows, LANES:2 * LANES] = kext.astype(BF16)
                v = v_ref[0, rows, lanes].astype(F32)
                vx_ref[p, 0, :, rows] = jnp.where(first, v, 1.0).T.astype(BF16)
                vx_ref[p, 1, :, rows] = jnp.where(first, 1.0, v).T.astype(BF16)
            return carry
        lax.fori_loop(0, seq // tq, body, 0)

    scale = (1.0 / np.sqrt(HEAD_DIM)) * LOG2E
    sub = lax.broadcasted_iota(jnp.int32, (LANES, tq), 0)
    for p in pairs:
        qn = _headnorm(q_ref[0, :, p * LANES:(p + 1) * LANES].astype(F32), qw_ref[...],
                       same_head) * scale
        qt = qn.T
        for hh in range(2):
            c_hi, c_mid, c_lo = _split3(ct_ref[0, pl.ds(bases[p] + hh, 1), :] * LOG2E)
            minus = (sub >= 3 * hh) & (sub < 3 * hh + 3)
            ext = jnp.where(sub == 6, c_hi.astype(F32), jnp.where(sub == 7, c_mid.astype(F32),
                  jnp.where(sub == 8, c_lo.astype(F32), jnp.where(minus, -1.0, 0.0))))
            own = (sub < HEAD_DIM) if hh == 0 else (sub >= HEAD_DIM)
            qx_ref[p, 0:LANES, hh * tq:(hh + 1) * tq] = jnp.where(own, qt, 0.0).astype(BF16)
            qx_ref[p, LANES:2 * LANES, hh * tq:(hh + 1) * tq] = ext.astype(BF16)
        m_ref[p] = jnp.full((1, 2 * tq), NEG_BIG, F32)
        acc_ref[p] = jnp.zeros((2, LANES, tq), F32)

    n_full = (qi * tq) // tk
    q_first = qi * tq - n_full * tk
    key_i = lax.broadcasted_iota(jnp.int32, (tk, 2 * tq), 0)
    qry_i = lax.broadcasted_iota(jnp.int32, (tk, 2 * tq), 1)
    keep = jnp.where(qry_i >= tq, qry_i - tq, qry_i) + q_first >= key_i

    def scores(j, s_ref):
        rows = pl.ds(pl.multiple_of(j * tk, tk), tk)
        for p in pairs:
            s_ref[p] = _dot(kx_ref[p, rows, :], qx_ref[p])

    def softmax_pv(j, s_ref, masked):
        cols = pl.ds(pl.multiple_of(j * tk, tk), tk)
        for p in pairs:
            s = s_ref[p]
            if masked:
                s = jnp.where(keep, s, NEG_BIG)
            m = m_ref[p]
            m_new = jnp.maximum(m, jnp.max(s, axis=0, keepdims=True))
            alpha = jnp.exp2(m - m_new)
            pb = jnp.exp2(s - m_new).astype(BF16)
            acc_ref[p, 0] = alpha[:, :tq] * acc_ref[p, 0] + _dot(vx_ref[p, 0, :, cols], pb[:, :tq])
            acc_ref[p, 1] = alpha[:, tq:] * acc_ref[p, 1] + _dot(vx_ref[p, 1, :, cols], pb[:, tq:])
            m_ref[p] = m_new

    scores(0, sa_ref)

    def two_blocks(i, carry):
        j = 2 * i
        scores(j + 1, sb_ref)
        softmax_pv(j, sa_ref, False)
        scores(j + 2, sa_ref)
        softmax_pv(j + 1, sb_ref, False)
        return carry
    lax.fori_loop(0, n_full // 2, two_blocks, 0)

    @pl.when(n_full % 2 == 0)
    def _():
        softmax_pv(n_full, sa_ref, True)

    @pl.when(n_full % 2 == 1)
    def _():
        scores(n_full, sb_ref)
        softmax_pv(n_full - 1, sa_ref, False)
        softmax_pv(n_full, sb_ref, True)

    for p in pairs:
        out_t = jnp.concatenate(
            [acc_ref[p, 0, 0:HEAD_DIM, :] / acc_ref[p, 0, HEAD_DIM:LANES, :],
             acc_ref[p, 1, HEAD_DIM:LANES, :] / acc_ref[p, 1, 0:HEAD_DIM, :]], axis=0)
        o_ref[0, :, p * LANES:(p + 1) * LANES] = out_t.T.astype(BF16)


def _attn(proj, q_off, k_off, v_off, d_att, c, ct, qw_row, kw_row, batch, seq, tq, tk, npair):
    width = npair * LANES
    assert d_att % width == 0 and seq % tq == 0 and seq % tk == 0 and tk % tq == 0
    assert q_off % width == 0 and k_off % width == 0 and v_off % width == 0
    qo, ko, vo = q_off // width, k_off // width, v_off // width
    q3 = k3 = v3 = proj.reshape(batch, seq, proj.shape[-1])
    c3 = c.reshape(batch, seq, LANES)
    return pl.pallas_call(
        functools.partial(_attn_kernel, tq, tk, npair),
        grid=(batch, d_att // width, seq // tq),
        in_specs=[
            pl.BlockSpec((1, tq, width), lambda b, h, i: (b, i, qo + h)),
            pl.BlockSpec((1, seq, width), lambda b, h, i: (b, 0, ko + h)),
            pl.BlockSpec((1, seq, width), lambda b, h, i: (b, 0, vo + h)),
            pl.BlockSpec((1, seq, LANES), lambda b, h, i: (b, 0, 0)),
            pl.BlockSpec((1, LANES, tq), lambda b, h, i: (b, 0, i)),
            pl.BlockSpec((1, LANES), lambda b, h, i: (0, 0)),
            pl.BlockSpec((1, LANES), lambda b, h, i: (0, 0)),
        ],
        out_specs=pl.BlockSpec((1, tq, width), lambda b, h, i: (b, i, h)),
        out_shape=jax.ShapeDtypeStruct((batch, seq, d_att), BF16),
        scratch_shapes=[
            pltpu.VMEM((npair, seq, 2 * LANES), BF16),
            pltpu.VMEM((npair, 2, LANES, seq), BF16),
            pltpu.VMEM((npair, 2 * LANES, 2 * tq), BF16),
            pltpu.VMEM((npair, tk, 2 * tq), F32),
            pltpu.VMEM((npair, tk, 2 * tq), F32),
            pltpu.VMEM((npair, 1, 2 * tq), F32),
            pltpu.VMEM((npair, 2, LANES, tq), F32),
        ],
        compiler_params=_cparams(("parallel", "parallel", "arbitrary")),
        name="attn",
    )(q3, k3, v3, c3, ct, qw_row, kw_row).reshape(batch * seq, d_att)


def _outproj_kernel(ys_ref, ya_ref, x_ref, wos_ref, woa_ref, aw_ref, fw_ref, wr_ref, br_ref,
                    x1_ref, xn_ref, gates_ref):
    ya = ya_ref[...].astype(F32)
    ya = ya * lax.rsqrt(jnp.mean(ya * ya, axis=-1, keepdims=True) + EPS) * aw_ref[...]
    x1 = x_ref[...] + _dot(ys_ref[...], wos_ref[...]) + _dot(ya.astype(BF16), woa_ref[...])
    x1_ref[...] = x1
    xn = x1 * lax.rsqrt(jnp.mean(x1 * x1, axis=-1, keepdims=True) + EPS) * fw_ref[...]
    xn_ref[...] = xn.astype(BF16)

    x_hi = xn.astype(BF16)
    x_lo = (xn - x_hi.astype(F32)).astype(BF16)
    w = wr_ref[...]
    w_hi = w.astype(BF16)
    w_lo = (w - w_hi.astype(F32)).astype(BF16)
    logits = _dot(x_hi, w_hi) + _dot(x_lo, w_hi) + _dot(x_hi, w_lo) + br_ref[...]
    lane = lax.broadcasted_iota(jnp.int32, logits.shape, 1).astype(F32)

    def rmax(mask):
        return jnp.max(jnp.where(mask, logits, NEG_BIG), axis=-1, keepdims=True)

    def first_at(mask, val):
        return jnp.min(jnp.where(mask & (logits == val), lane, 1e9), axis=-1, keepdims=True)

    gmask = (lane >= N_EXPERTS) & (lane < N_EXPERTS + N_EXPERT_GROUPS)
    gmax = rmax(gmask)
    p_g = 1.0 / jnp.sum(jnp.where(gmask, jnp.exp(logits - gmax), 0.0), axis=-1, keepdims=True)
    g_star = first_at(gmask, gmax) - N_EXPERTS
    lo = g_star * EXPERTS_PER_GROUP
    emask = (lane >= lo) & (lane < lo + EXPERTS_PER_GROUP)
    v1 = rmax(emask)
    i1 = first_at(emask, v1)
    mask2 = emask & (lane != i1)
    v2 = rmax(mask2)
    i2 = first_at(mask2, v2)
    e2 = jnp.exp(v2 - v1)
    w1 = p_g / (1.0 + e2)
    w2 = p_g * e2 / (1.0 + e2)
    gates_ref[...] = (jnp.where(lane == i1, w1, 0.0) + jnp.where(lane == i2, w2, 0.0)
                      + jnp.where(lane == N_EXPERTS, g_star, 0.0))


def _outproj(ys, ya, x, wos, woa, aw, fw, wr, br, tm):
    t, d = x.shape
    dm = ys.shape[1]
    rows = lambda i: (i, 0)
    const = lambda i: (0, 0)
    return pl.pallas_call(
        _outproj_kernel,
        grid=(t // tm,),
        in_specs=[
            pl.BlockSpec((tm, dm), rows),
            pl.BlockSpec((tm, dm), rows),
            pl.BlockSpec((tm, d), rows),
            pl.BlockSpec((dm, d), const),
            pl.BlockSpec((dm, d), const),
            pl.BlockSpec((1, dm), const),
            pl.BlockSpec((1, d), const),
            pl.BlockSpec((d, LANES), const),
            pl.BlockSpec((1, LANES), const),
        ],
        out_specs=[
            pl.BlockSpec((tm, d), rows),
            pl.BlockSpec((tm, d), rows),
            pl.BlockSpec((tm, LANES), rows),
        ],
        out_shape=[
            jax.ShapeDtypeStruct((t, d), F32),
            jax.ShapeDtypeStruct((t, d), BF16),
            jax.ShapeDtypeStruct((t, LANES), F32),
        ],
        compiler_params=_cparams(("parallel",)),
        name="outproj",
    )(ys, ya, x, wos, woa, aw, fw, wr, br)


MOE_BLOCK = 1024
MOE_CHUNK = 128
MOE_ROWS = MOE_BLOCK + N_EXPERT_GROUPS * MOE_CHUNK
GROUP_LANE = N_EXPERTS


def _group_runs(count):
    nch = jnp.floor((count + (MOE_CHUNK - 1)) * (1.0 / MOE_CHUNK))
    return nch, nch * MOE_CHUNK


def _moe_dispatch_kernel(xn_ref, route_ref, before_ref, xs_ref, gs_ref, meta_ref):
    pt = xn_ref.shape[0]
    route = route_ref[...]
    rt = route.T
    g_row = rt[GROUP_LANE:GROUP_LANE + 1, :]
    grp = lax.broadcasted_iota(jnp.int32, (8, pt), 0).astype(F32)
    onehot = jnp.where(grp == g_row, 1.0, 0.0)
    rank = _dot(onehot.astype(BF16), before_ref[...])
    nch, padded = _group_runs(jnp.sum(onehot, axis=1, keepdims=True))
    gi = lax.broadcasted_iota(jnp.int32, (8, 1), 0)
    start = jnp.zeros((8, 1), F32)
    for g in range(N_EXPERT_GROUPS - 1):
        start = start + jnp.where(gi > g, padded[g:g + 1, :], 0.0)
    pos = jnp.sum(onehot * (start + rank), axis=0, keepdims=True)
    dest = lax.broadcasted_iota(jnp.int32, (MOE_ROWS, pt), 0).astype(F32)
    perm = jnp.where(dest == pos, 1.0, 0.0).astype(BF16)
    g_hi = route.astype(BF16)
    g_lo = (route - g_hi.astype(F32)).astype(BF16)
    moved = _dot(perm, jnp.concatenate([xn_ref[...], g_hi, g_lo], axis=1))
    d = xn_ref.shape[1]
    xs_ref[0] = moved[:, :d].astype(BF16)
    gs_ref[0] = moved[:, d:d + LANES] + moved[:, d + LANES:]
    lane = lax.broadcasted_iota(jnp.int32, (8, LANES), 1)
    meta_ref[0] = jnp.where(lane == 0, start, jnp.where(lane == 1, nch, 0.0))


def _moe_expert_kernel(n_sub, n_exp, meta_ref, xs_ref, gs_ref, wg_ref, wu_ref, wd_ref, ys_ref):
    st = pl.program_id(0)
    eb = pl.program_id(1)
    g = (eb * n_exp) // EXPERTS_PER_GROUP

    @pl.when(eb == 0)
    def _():
        ys_ref[...] = jnp.zeros(ys_ref.shape, F32)

    def run(s, first_row, n_rows):
        rows = pl.ds(pl.multiple_of(first_row, MOE_CHUNK), n_rows)
        lane = lax.broadcasted_iota(jnp.int32, (n_rows, LANES), 1)
        xc = xs_ref[s, rows, :]
        gates = gs_ref[s, rows, :]
        y = ys_ref[s, rows, :]
        for k in range(n_exp):
            gate = jnp.sum(jnp.where(lane == eb * n_exp + k, gates, 0.0), axis=-1, keepdims=True)
            hid = _silu(_dot(xc, wg_ref[k])) * _dot(xc, wu_ref[k]) * gate
            y = y + _dot(hid.astype(BF16), wd_ref[k])
        ys_ref[s, rows, :] = y

    for s in range(n_sub):
        base = ((st * n_sub + s) * N_EXPERT_GROUPS + g) * 2
        start = meta_ref[base]
        n_chunks = meta_ref[base + 1]

        def two_chunks(c, carry, s=s, start=start):
            run(s, start + c * (2 * MOE_CHUNK), 2 * MOE_CHUNK)
            return carry
        lax.fori_loop(0, n_chunks // 2, two_chunks, 0)

        @pl.when(n_chunks % 2 == 1)
        def _(s=s, start=start, n_chunks=n_chunks):
            run(s, start + (n_chunks - 1) * MOE_CHUNK, MOE_CHUNK)


def _moe_combine_kernel(ys_ref, route_ref, x1_ref, before_ref, o_ref):
    pt = x1_ref.shape[0]
    route = route_ref[...]
    lane = lax.broadcasted_iota(jnp.int32, (pt, LANES), 1).astype(F32)
    onehot = jnp.where(lane == route[:, GROUP_LANE:GROUP_LANE + 1], 1.0, 0.0)
    rank = _dot(before_ref[...], onehot.astype(BF16))
    _, padded = _group_runs(jnp.sum(onehot, axis=0, keepdims=True))
    lane1 = lax.broadcasted_iota(jnp.int32, (1, LANES), 1)
    start = jnp.zeros((1, LANES), F32)
    for g in range(N_EXPERT_GROUPS - 1):
        start = start + jnp.where(lane1 > g, padded[:, g:g + 1], 0.0)
    pos = jnp.sum(onehot * (start + rank), axis=-1, keepdims=True)
    src = lax.broadcasted_iota(jnp.int32, (pt, MOE_ROWS), 1).astype(F32)
    perm_t = jnp.where(src == pos, 1.0, 0.0).astype(BF16)
    o_ref[...] = x1_ref[...] + _dot(perm_t, ys_ref[0].astype(BF16))


def _moe(xn, route, x1, wg, wu, wd, n_sub):
    t, d = x1.shape
    n_e, _, f = wg.shape
    pt = MOE_BLOCK
    assert t % (pt * n_sub) == 0
    nb = t // pt
    idx = jnp.arange(pt)
    before_row = (idx[:, None] < idx[None, :]).astype(BF16)
    before_col = (idx[None, :] < idx[:, None]).astype(BF16)
    rows = lambda i: (i, 0)
    const = lambda i: (0, 0)
    xs, gs, meta = pl.pallas_call(
        _moe_dispatch_kernel,
        grid=(nb,),
        in_specs=[
            pl.BlockSpec((pt, d), rows),
            pl.BlockSpec((pt, LANES), rows),
            pl.BlockSpec((pt, pt), const),
        ],
        out_specs=[
            pl.BlockSpec((1, MOE_ROWS, d), lambda i: (i, 0, 0)),
            pl.BlockSpec((1, MOE_ROWS, LANES), lambda i: (i, 0, 0)),
            pl.BlockSpec((1, 8, LANES), lambda i: (i, 0, 0)),
        ],
        out_shape=[
            jax.ShapeDtypeStruct((nb, MOE_ROWS, d), BF16),
            jax.ShapeDtypeStruct((nb, MOE_ROWS, LANES), F32),
            jax.ShapeDtypeStruct((nb, 8, LANES), F32),
        ],
        compiler_params=_cparams(("parallel",)),
        name="moe_dispatch",
    )(xn, route, before_row)

    meta_i = meta[:, :N_EXPERT_GROUPS, :2].astype(jnp.int32).reshape(-1)
    n_exp = 2
    assert EXPERTS_PER_GROUP % n_exp == 0
    ys = pl.pallas_call(
        functools.partial(_moe_expert_kernel, n_sub, n_exp),
        grid_spec=pltpu.PrefetchScalarGridSpec(
            num_scalar_prefetch=1,
            grid=(nb // n_sub, n_e // n_exp),
            in_specs=[
                pl.BlockSpec((n_sub, MOE_ROWS, d), lambda i, e, m: (i, 0, 0)),
                pl.BlockSpec((n_sub, MOE_ROWS, LANES), lambda i, e, m: (i, 0, 0)),
                pl.BlockSpec((n_exp, d, f), lambda i, e, m: (e, 0, 0)),
                pl.BlockSpec((n_exp, d, f), lambda i, e, m: (e, 0, 0)),
                pl.BlockSpec((n_exp, f, d), lambda i, e, m: (e, 0, 0)),
            ],
            out_specs=pl.BlockSpec((n_sub, MOE_ROWS, d), lambda i, e, m: (i, 0, 0)),
        ),
        out_shape=jax.ShapeDtypeStruct((nb, MOE_ROWS, d), F32),
        compiler_params=_cparams(("parallel", "arbitrary")),
        name="moe_experts",
    )(meta_i, xs, gs, wg, wu, wd)

    return pl.pallas_call(
        _moe_combine_kernel,
        grid=(nb,),
        in_specs=[
            pl.BlockSpec((1, MOE_ROWS, d), lambda i: (i, 0, 0)),
            pl.BlockSpec((pt, LANES), rows),
            pl.BlockSpec((pt, d), rows),
            pl.BlockSpec((pt, pt), const),
        ],
        out_specs=pl.BlockSpec((pt, d), rows),
        out_shape=jax.ShapeDtypeStruct((t, d), F32),
        compiler_params=_cparams(("parallel",)),
        name="moe_combine",
    )(ys, route, x1, before_col)


def _pad_lanes(v, offset=0):
    out = jnp.zeros((1, LANES), F32)
    return out.at[0, offset:offset + v.shape[0]].set(v.astype(F32))


def kernel(x, norm_mix_w, w_in, conv_w, conv_b, dt_bias, a_log, d_skip, ssm_norm_w, q_norm_w,
           k_norm_w, fox_f_bias, att_out_norm_w, w_out, norm_ffn_w, w_router_group,
           b_router_group, w_router_expert, b_router_expert, w_expert_gate, w_expert_up,
           w_expert_down):
    batch, seq, d = x.shape
    depth = w_in.shape[0]
    d_ssm = ssm_norm_w.shape[1]
    d_att = att_out_norm_w.shape[1]
    h_ssm = dt_bias.shape[1]
    h_att = fox_f_bias.shape[1]
    conv_dim = conv_w.shape[2]
    assert h_ssm == 16 and h_att == 16 and d_ssm // h_ssm == HEAD_DIM and d_att // h_att == HEAD_DIM
    assert conv_dim == d_ssm + 2 * SSM_GROUPS * D_STATE
    off_xbc = d_ssm
    off_dt = off_xbc + conv_dim
    off_q = off_dt + h_ssm
    off_k = off_q + d_att
    off_v = off_k + d_att
    off_f = off_v + d_att
    t = batch * seq
    tm_in = min(512, t)
    tm_out = min(512, t)
    moe_sub = 2 if t % (2 * MOE_BLOCK) == 0 else 1
    tq = min(256, seq)
    tk = 2 * tq if seq % (2 * tq) == 0 else tq

    head_of_chan = jnp.arange(d_ssm) // HEAD_DIM
    expand = (jnp.arange(LANES)[:, None] == head_of_chan[None, :]).astype(BF16)

    xf = x.reshape(t, d)
    for l in range(depth):
        wl = w_in[l]
        w_main = jnp.concatenate(
            [wl[:, :off_xbc], wl[:, off_q:off_k], wl[:, off_k:off_v], wl[:, off_v:off_f],
             wl[:, off_xbc:off_dt]], axis=1).astype(BF16)
        p_q, p_k, p_v, p_xbc = d_ssm, d_ssm + d_att, d_ssm + 2 * d_att, d_ssm + 3 * d_att
        w_small = jnp.zeros((d, LANES), F32)
        w_small = w_small.at[:, 0:h_ssm].set(wl[:, off_dt:off_q])
        w_small = w_small.at[:, 16:16 + h_att].set(wl[:, off_f:off_f + h_att]).astype(BF16)
        proj, small = _inproj(xf, norm_mix_w[l][None, :], w_main, w_small, tm_in, 512)

        bias_row = _pad_lanes(dt_bias[l]) + _pad_lanes(fox_f_bias[l], 16)
        a_row = _pad_lanes(-jnp.exp(a_log[l].astype(F32)))
        dskip_row = jnp.repeat(d_skip[l].astype(F32), HEAD_DIM)[None, :]
        y_ssm, c, ct = _ssd(proj, 0, p_xbc, conv_dim, d_ssm, small, conv_w[l].astype(F32),
                        conv_b[l].astype(F32)[None, :], bias_row, a_row, dskip_row,
                        ssm_norm_w[l][None, :], expand, batch, seq)

        qw_row = jnp.tile(q_norm_w[l].astype(F32), 2)[None, :]
        kw_row = jnp.tile(k_norm_w[l].astype(F32), 2)[None, :]
        y_att = _attn(proj, p_q, p_k, p_v, d_att, c, ct, qw_row, kw_row, batch, seq, tq, tk, 2)

        wr = jnp.zeros((d, LANES), F32)
        wr = wr.at[:, :N_EXPERTS].set(w_router_expert[l])
        wr = wr.at[:, N_EXPERTS:N_EXPERTS + N_EXPERT_GROUPS].set(w_router_group[l])
        br = _pad_lanes(b_router_expert[l]) + _pad_lanes(b_router_group[l], N_EXPERTS)
        wo = w_out[l].astype(BF16)
        x1, xn, gates = _outproj(y_ssm, y_att, xf, wo[:d_ssm], wo[d_ssm:],
                                 att_out_norm_w[l][None, :], norm_ffn_w[l][None, :], wr, br, tm_out)

        xf = _moe(xn, gates, x1, w_expert_gate[l].astype(BF16), w_expert_up[l].astype(BF16),
                  w_expert_down[l].astype(BF16), moe_sub)
    return xf.reshape(batch, seq, d)
```

```python
import functools

import jax
import jax.numpy as jnp
import numpy as np
from jax import lax
from jax.experimental import pallas as pl
from jax.experimental.pallas import tpu as pltpu

F32 = jnp.float32
BF16 = jnp.bfloat16

HEAD_DIM = 64
D_STATE = 128
SSM_GROUPS = 2
CONV_K = 4
SSD_CHUNK = 128
N_EXPERT_GROUPS = 4
EXPERTS_PER_GROUP = 8
N_EXPERTS = N_EXPERT_GROUPS * EXPERTS_PER_GROUP
EPS = 1e-6
LANES = 128
NEG_BIG = -1e30
LOG2E = 1.4426950408889634
VMEM_LIMIT = 56 * 1024 * 1024


def _cparams(sem):
    return pltpu.CompilerParams(dimension_semantics=sem, vmem_limit_bytes=VMEM_LIMIT)


def _split3(a):
    a1 = a.astype(BF16)
    r1 = a - a1.astype(F32)
    a2 = r1.astype(BF16)
    a3 = (r1 - a2.astype(F32)).astype(BF16)
    return a1, a2, a3


def _dot(a, b):
    return jnp.dot(a, b, preferred_element_type=F32)


def _dot3(a_f32, b_bf16):
    a1, a2, a3 = _split3(a_f32)
    return _dot(a1, b_bf16) + _dot(a2, b_bf16) + _dot(a3, b_bf16)


def _dot2(a_f32, b_bf16):
    a1 = a_f32.astype(BF16)
    a2 = (a_f32 - a1.astype(F32)).astype(BF16)
    return _dot(a1, b_bf16) + _dot(a2, b_bf16)


def _dot3_tn(b_bf16_t, a_f32):
    a1, a2, a3 = _split3(a_f32)
    return _dot(b_bf16_t, a1) + _dot(b_bf16_t, a2) + _dot(b_bf16_t, a3)


def _silu(x):
    return x / (1.0 + jnp.exp(-x))


def _inproj_kernel(tn, x_ref, nw_ref, w_ref, ws_ref, proj_ref, small_ref):
    x = x_ref[...]
    ms = jnp.mean(x * x, axis=-1, keepdims=True)
    hb = (x * lax.rsqrt(ms + EPS) * nw_ref[...]).astype(BF16)
    small_ref[...] = _dot(hb, ws_ref[...])
    for j in range(w_ref.shape[1] // tn):
        cols = slice(j * tn, (j + 1) * tn)
        proj_ref[:, cols] = _dot(hb, w_ref[:, cols]).astype(BF16)


def _inproj(x2, nw, w_main, w_small, tm, tn):
    t, d = x2.shape
    n = w_main.shape[1]
    assert n % tn == 0 and t % tm == 0
    return pl.pallas_call(
        functools.partial(_inproj_kernel, tn),
        grid=(t // tm,),
        in_specs=[
            pl.BlockSpec((tm, d), lambda i: (i, 0)),
            pl.BlockSpec((1, d), lambda i: (0, 0)),
            pl.BlockSpec((d, n), lambda i: (0, 0)),
            pl.BlockSpec((d, LANES), lambda i: (0, 0)),
        ],
        out_specs=[
            pl.BlockSpec((tm, n), lambda i: (i, 0)),
            pl.BlockSpec((tm, LANES), lambda i: (i, 0)),
        ],
        out_shape=[
            jax.ShapeDtypeStruct((t, n), BF16),
            jax.ShapeDtypeStruct((t, LANES), F32),
        ],
        compiler_params=_cparams(("parallel",)),
        name="inproj",
    )(x2, nw, w_main, w_small)


def _ssd_kernel(d_ssm, n_xbc, n_seq, *refs):
    ext_ref, state_ref, carry_ref = refs[-3:]
    L = SSD_CHUNK

    @pl.when(pl.program_id(1) == 0)
    def _():
        ext_ref[:, L:2 * L, :] = jnp.zeros((n_seq, L, ext_ref.shape[2]), BF16)
        state_ref[...] = jnp.zeros(state_ref.shape, F32)
        carry_ref[...] = jnp.zeros(carry_ref.shape, F32)

    for bb in range(n_seq):
        _ssd_chunk(bb, d_ssm, n_xbc, refs)


def _ssd_chunk(bb, d_ssm, n_xbc, refs):
    xbc_refs = refs[:n_xbc]
    (z_ref, small_ref, convw_ref, convb_ref, bias_ref, arow_ref, dskip_ref, nw_ref, expand_ref,
     shift_ref, y_ref, c_ref, ct_ref, ext_ref, state_ref, carry_ref) = refs[n_xbc:]
    L = SSD_CHUNK
    n_pairs = d_ssm // LANES
    gw = d_ssm // SSM_GROUPS

    u = jnp.concatenate([r[bb] for r in xbc_refs], axis=1)
    ext_ref[bb, 0:L, :] = ext_ref[bb, L:2 * L, :]
    ext_ref[bb, L:2 * L, :] = u
    n_delay = CONV_K - 1
    delayed = _dot(shift_ref[...], ext_ref[bb])
    acc = convb_ref[...] + convw_ref[n_delay:CONV_K, :] * u.astype(F32)
    for k in range(n_delay):
        acc = acc + convw_ref[k:k + 1, :] * delayed[k * L:(k + 1) * L, :]
    xbc = _silu(acc)
    xs = xbc[:, :d_ssm]
    b_all = xbc[:, d_ssm:d_ssm + SSM_GROUPS * D_STATE].astype(BF16)
    c_all = xbc[:, d_ssm + SSM_GROUPS * D_STATE:].astype(BF16)

    row = lax.broadcasted_iota(jnp.int32, (L, L), 0)
    col = lax.broadcasted_iota(jnp.int32, (L, L), 1)
    causal = row >= col
    tril = jnp.where(causal, 1.0, 0.0).astype(BF16)
    pre = small_ref[bb] + bias_ref[...]
    sp = jnp.maximum(pre, 0.0) + jnp.log(1.0 + jnp.exp(-jnp.abs(pre)))
    lane = lax.broadcasted_iota(jnp.int32, (L, LANES), 1)
    is_dt = lane < 16
    dt = jnp.where(is_dt, sp, 0.0)
    logf = jnp.where((lane >= 16) & (lane < 32), pre - sp, 0.0)
    a = dt * arow_ref[...]
    a_and_f = a + logf
    cum_all = _dot3_tn(tril, a_and_f)
    cum = jnp.where(is_dt, cum_all, 0.0)
    cum_t = cum.T
    c_run = cum_all + carry_ref[bb, 0:1, :]
    c_ref[bb] = c_run
    ct_ref[bb] = c_run.T
    carry_ref[bb, 0:1, :] = c_run[L - 1:L, :]

    expand = expand_ref[...]
    dt_x = _dot2(dt, expand)
    ecum_x = _dot2(jnp.exp(cum), expand)
    dte_x = _dot2(jnp.exp(cum[L - 1:L, :] - cum), expand)
    xdt = xs * dt_x
    xdt_b = xdt.astype(BF16)
    w_b = (xdt * dte_x).astype(BF16)

    y_parts = []
    for g in range(SSM_GROUPS):
        bg = b_all[:, g * D_STATE:(g + 1) * D_STATE]
        cg = c_all[:, g * D_STATE:(g + 1) * D_STATE]
        cb = lax.dot_general(cg, bg, (((1,), (1,)), ((), ())), preferred_element_type=F32)
        s_old = state_ref[bb, g]
        y_off = _dot(cg, s_old.astype(BF16)) * ecum_x[:, g * gw:(g + 1) * gw]
        upd = lax.dot_general(bg, w_b[:, g * gw:(g + 1) * gw], (((0,), (0,)), ((), ())),
                              preferred_element_type=F32)
        state_ref[bb, g] = s_old * ecum_x[L - 1:L, g * gw:(g + 1) * gw] + upd
        pairs_per_group = n_pairs // SSM_GROUPS
        for pp in range(pairs_per_group):
            p = g * pairs_per_group + pp
            xp = xdt_b[:, p * LANES:(p + 1) * LANES]
            ys = []
            for hh in range(2):
                h = 2 * p + hh
                seg = cum[:, h:h + 1] - cum_t[h:h + 1, :]
                decay = jnp.exp(jnp.where(causal, seg, NEG_BIG))
                ys.append(_dot((cb * decay).astype(BF16), xp))
            first = lax.broadcasted_iota(jnp.int32, (L, LANES), 1) < HEAD_DIM
            y_parts.append(jnp.where(first, ys[0], ys[1])
                           + y_off[:, pp * LANES:(pp + 1) * LANES])
    y = jnp.concatenate(y_parts, axis=1) + xs * dskip_ref[...]
    gated = y * _silu(z_ref[bb].astype(F32))
    ms = jnp.mean(gated * gated, axis=-1, keepdims=True)
    y_ref[bb] = (gated * lax.rsqrt(ms + EPS) * nw_ref[...]).astype(BF16)


def _ssd(proj, z_off, xbc_off, cdim, d_ssm, small, convw, convb, bias_row, a_row, dskip_row, nw,
         expand, batch, seq):
    t = proj.shape[0]
    L = SSD_CHUNK
    nc = seq // L
    wx = 512
    assert cdim % wx == 0 and xbc_off % wx == 0 and z_off % d_ssm == 0
    n_xbc = cdim // wx
    n_seq = 2 if batch % 2 == 0 else 1
    proj3 = proj.reshape(batch, seq, proj.shape[1])
    small3 = small.reshape(batch, seq, LANES)
    rowmap = lambda b, c: (b, c, 0)
    const = lambda b, c: (0, 0)
    n_delay = CONV_K - 1
    r = jnp.arange(n_delay * L)
    target = L + r % L - (n_delay - r // L)
    shift = (jnp.arange(2 * L)[None, :] == target[:, None]).astype(BF16)
    xbc_specs = [pl.BlockSpec((n_seq, L, wx), lambda b, c, o=xbc_off // wx + i: (b, c, o))
                 for i in range(n_xbc)]
    y, c_out, ct = pl.pallas_call(
        functools.partial(_ssd_kernel, d_ssm, n_xbc, n_seq),
        grid=(batch // n_seq, nc),
        in_specs=xbc_specs + [
            pl.BlockSpec((n_seq, L, d_ssm), lambda b, c: (b, c, z_off // d_ssm)),
            pl.BlockSpec((n_seq, L, LANES), rowmap),
            pl.BlockSpec((CONV_K, cdim), const),
            pl.BlockSpec((1, cdim), const),
            pl.BlockSpec((1, LANES), const),
            pl.BlockSpec((1, LANES), const),
            pl.BlockSpec((1, d_ssm), const),
            pl.BlockSpec((1, d_ssm), const),
            pl.BlockSpec((LANES, d_ssm), const),
            pl.BlockSpec((n_delay * L, 2 * L), const),
        ],
        out_specs=[
            pl.BlockSpec((n_seq, L, d_ssm), rowmap),
            pl.BlockSpec((n_seq, L, LANES), rowmap),
            pl.BlockSpec((n_seq, LANES, L), lambda b, c: (b, 0, c)),
        ],
        out_shape=[
            jax.ShapeDtypeStruct((batch, seq, d_ssm), BF16),
            jax.ShapeDtypeStruct((batch, seq, LANES), F32),
            jax.ShapeDtypeStruct((batch, LANES, seq), F32),
        ],
        scratch_shapes=[
            pltpu.VMEM((n_seq, 2 * L, cdim), BF16),
            pltpu.VMEM((n_seq, SSM_GROUPS, D_STATE, d_ssm // SSM_GROUPS), F32),
            pltpu.VMEM((n_seq, 8, LANES), F32),
        ],
        compiler_params=_cparams(("parallel", "arbitrary")),
        name="ssd",
    )(*([proj3] * n_xbc), proj3, small3, convw, convb, bias_row, a_row, dskip_row, nw, expand, shift)
    return y.reshape(t, d_ssm), c_out.reshape(t, LANES), ct


def _headnorm(x, w_row, same_head):
    x2 = x * x
    hi = x2.astype(BF16)
    lo = (x2 - hi.astype(F32)).astype(BF16)
    ms = (_dot(hi, same_head) + _dot(lo, same_head)) * (1.0 / HEAD_DIM)
    return x * lax.rsqrt(ms + EPS) * w_row


def _place(parts, moves, rr, cc):
    out = None
    for i, part in enumerate(parts):
        sel = None
        for src, dst in moves:
            one = jnp.where((rr == src) & (cc == dst + i), 1.0, 0.0)
            sel = one if sel is None else sel + one
        term = _dot(part, sel.astype(BF16))
        out = term if out is None else out + term
    return out


def _attn_kernel(tq, tk, npair, q_ref, k_ref, v_ref, c_ref, ct_ref, qw_ref, kw_ref, o_ref,
                 kx_ref, vx_ref, qx_ref, sa_ref, sb_ref, m_ref, acc_ref):
    grp = pl.program_id(1)
    qi = pl.program_id(2)
    seq = k_ref.shape[1]
    lane = lax.broadcasted_iota(jnp.int32, (1, LANES), 1)
    first = lane < HEAD_DIM
    rr = lax.broadcasted_iota(jnp.int32, (LANES, LANES), 0)
    cc = lax.broadcasted_iota(jnp.int32, (LANES, LANES), 1)
    same_head = jnp.where((rr < HEAD_DIM) == (cc < HEAD_DIM), 1.0, 0.0).astype(BF16)
    pairs = range(npair)
    bases = [16 + 2 * (grp * npair + p) for p in pairs]

    @pl.when(qi == 0)
    def _():
        k_const = jnp.where((lane >= 6) & (lane < 9), 1.0, 0.0)

        def body(r, carry):
            rows = pl.ds(pl.multiple_of(r * tq, tq), tq)
            parts = _split3(c_ref[0, rows, :] * LOG2E)
            for p in pairs:
                lanes = slice(p * LANES, (p + 1) * LANES)
                kn = _headnorm(k_ref[0, rows, lanes].astype(F32), kw_ref[...], same_head)
                kext = _place(parts, [(bases[p], 0), (bases[p] + 1, 3)], rr, cc) + k_const
                kx_ref[p, rows, 0:LANES] = kn.astype(BF16)
                kx_ref[p, rows, LANES:2 * LANES] = kext.astype(BF16)
                v = v_ref[0, rows, lanes].astype(F32)
                vx_ref[p, 0, :, rows] = jnp.where(first, v, 1.0).T.astype(BF16)
                vx_ref[p, 1, :, rows] = jnp.where(first, 1.0, v).T.astype(BF16)
            return carry
        lax.fori_loop(0, seq // tq, body, 0)

    scale = (1.0 / np.sqrt(HEAD_DIM)) * LOG2E
    sub = lax.broadcasted_iota(jnp.int32, (LANES, tq), 0)
    for p in pairs:
        qn = _headnorm(q_ref[0, :, p * LANES:(p + 1) * LANES].astype(F32), qw_ref[...],
                       same_head) * scale
        qt = qn.T
        for hh in range(2):
            c_hi, c_mid, c_lo = _split3(ct_ref[0, pl.ds(bases[p] + hh, 1), :] * LOG2E)
            minus = (sub >= 3 * hh) & (sub < 3 * hh + 3)
            ext = jnp.where(sub == 6, c_hi.astype(F32), jnp.where(sub == 7, c_mid.astype(F32),
                  jnp.where(sub == 8, c_lo.astype(F32), jnp.where(minus, -1.0, 0.0))))
            own = (sub < HEAD_DIM) if hh == 0 else (sub >= HEAD_DIM)
            qx_ref[p, 0:LANES, hh * tq:(hh + 1) * tq] = jnp.where(own, qt, 0.0).astype(BF16)
            qx_ref[p, LANES:2 * LANES, hh * tq:(hh + 1) * tq] = ext.astype(BF16)
        m_ref[p] = jnp.full((1, 2 * tq), NEG_BIG, F32)
        acc_ref[p] = jnp.zeros((2, LANES, tq), F32)

    n_full = (qi * tq) // tk
    q_first = qi * tq - n_full * tk
    key_i = lax.broadcasted_iota(jnp.int32, (tk, 2 * tq), 0)
    qry_i = lax.broadcasted_iota(jnp.int32, (tk, 2 * tq), 1)
    keep = jnp.where(qry_i >= tq, qry_i - tq, qry_i) + q_first >= key_i

    def scores(j, s_ref):
        rows = pl.ds(pl.multiple_of(j * tk, tk), tk)
        for p in pairs:
            s_ref[p] = _dot(kx_ref[p, rows, :], qx_ref[p])

    def softmax_pv(j, s_ref, masked):
        cols = pl.ds(pl.multiple_of(j * tk, tk), tk)
        for p in pairs:
            s = s_ref[p]
            if masked:
                s = jnp.where(keep, s, NEG_BIG)
            m = m_ref[p]
            m_new = jnp.maximum(m, jnp.max(s, axis=0, keepdims=True))
            alpha = jnp.exp2(m - m_new)
            pb = jnp.exp2(s - m_new).astype(BF16)
            acc_ref[p, 0] = alpha[:, :tq] * acc_ref[p, 0] + _dot(vx_ref[p, 0, :, cols], pb[:, :tq])
            acc_ref[p, 1] = alpha[:, tq:] * acc_ref[p, 1] + _dot(vx_ref[p, 1, :, cols], pb[:, tq:])
            m_ref[p] = m_new

    scores(0, sa_ref)

    def two_blocks(i, carry):
        j = 2 * i
        scores(j + 1, sb_ref)
        softmax_pv(j, sa_ref, False)
        scores(j + 2, sa_ref)
        softmax_pv(j + 1, sb_ref, False)
        return carry
    lax.fori_loop(0, n_full // 2, two_blocks, 0)

    @pl.when(n_full % 2 == 0)
    def _():
        softmax_pv(n_full, sa_ref, True)

    @pl.when(n_full % 2 == 1)
    def _():
        scores(n_full, sb_ref)
        softmax_pv(n_full - 1, sa_ref, False)
        softmax_pv(n_full, sb_ref, True)

    for p in pairs:
        out_t = jnp.concatenate(
            [acc_ref[p, 0, 0:HEAD_DIM, :] / acc_ref[p, 0, HEAD_DIM:LANES, :],
             acc_ref[p, 1, HEAD_DIM:LANES, :] / acc_ref[p, 1, 0:HEAD_DIM, :]], axis=0)
        o_ref[0, :, p * LANES:(p + 1) * LANES] = out_t.T.astype(BF16)


def _attn(proj, q_off, k_off, v_off, d_att, c, ct, qw_row, kw_row, batch, seq, tq, tk, npair):
    width = npair * LANES
    assert d_att % width == 0 and seq % tq == 0 and seq % tk == 0 and tk % tq == 0
    assert q_off % width == 0 and k_off % width == 0 and v_off % width == 0
    qo, ko, vo = q_off // width, k_off // width, v_off // width
    q3 = k3 = v3 = proj.reshape(batch, seq, proj.shape[-1])
    c3 = c.reshape(batch, seq, LANES)
    return pl.pallas_call(
        functools.partial(_attn_kernel, tq, tk, npair),
        grid=(batch, d_att // width, seq // tq),
        in_specs=[
            pl.BlockSpec((1, tq, width), lambda b, h, i: (b, i, qo + h)),
            pl.BlockSpec((1, seq, width), lambda b, h, i: (b, 0, ko + h)),
            pl.BlockSpec((1, seq, width), lambda b, h, i: (b, 0, vo + h)),
            pl.BlockSpec((1, seq, LANES), lambda b, h, i: (b, 0, 0)),
            pl.BlockSpec((1, LANES, tq), lambda b, h, i: (b, 0, i)),
            pl.BlockSpec((1, LANES), lambda b, h, i: (0, 0)),
            pl.BlockSpec((1, LANES), lambda b, h, i: (0, 0)),
        ],
        out_specs=pl.BlockSpec((1, tq, width), lambda b, h, i: (b, i, h)),
        out_shape=jax.ShapeDtypeStruct((batch, seq, d_att), BF16),
        scratch_shapes=[
            pltpu.VMEM((npair, seq, 2 * LANES), BF16),
            pltpu.VMEM((npair, 2, LANES, seq), BF16),
            pltpu.VMEM((npair, 2 * LANES, 2 * tq), BF16),
            pltpu.VMEM((npair, tk, 2 * tq), F32),
            pltpu.VMEM((npair, tk, 2 * tq), F32),
            pltpu.VMEM((npair, 1, 2 * tq), F32),
            pltpu.VMEM((npair, 2, LANES, tq), F32),
        ],
        compiler_params=_cparams(("parallel", "parallel", "arbitrary")),
        name="attn",
    )(q3, k3, v3, c3, ct, qw_row, kw_row).reshape(batch * seq, d_att)


def _outproj_kernel(ys_ref, ya_ref, x_ref, wos_ref, woa_ref, aw_ref, fw_ref, wr_ref, br_ref,
                    x1_ref, xn_ref, gates_ref):
    ya = ya_ref[...].astype(F32)
    ya = ya * lax.rsqrt(jnp.mean(ya * ya, axis=-1, keepdims=True) + EPS) * aw_ref[...]
    x1 = x_ref[...] + _dot(ys_ref[...], wos_ref[...]) + _dot(ya.astype(BF16), woa_ref[...])
    x1_ref[...] = x1
    xn = x1 * lax.rsqrt(jnp.mean(x1 * x1, axis=-1, keepdims=True) + EPS) * fw_ref[...]
    xn_ref[...] = xn.astype(BF16)

    x_hi = xn.astype(BF16)
    x_lo = (xn - x_hi.astype(F32)).astype(BF16)
    w = wr_ref[...]
    w_hi = w.astype(BF16)
    w_lo = (w - w_hi.astype(F32)).astype(BF16)
    logits = _dot(x_hi, w_hi) + _dot(x_lo, w_hi) + _dot(x_hi, w_lo) + br_ref[...]
    lane = lax.broadcasted_iota(jnp.int32, logits.shape, 1).astype(F32)

    def rmax(mask):
        return jnp.max(jnp.where(mask, logits, NEG_BIG), axis=-1, keepdims=True)

    def first_at(mask, val):
        return jnp.min(jnp.where(mask & (logits == val), lane, 1e9), axis=-1, keepdims=True)

    gmask = (lane >= N_EXPERTS) & (lane < N_EXPERTS + N_EXPERT_GROUPS)
    gmax = rmax(gmask)
    p_g = 1.0 / jnp.sum(jnp.where(gmask, jnp.exp(logits - gmax), 0.0), axis=-1, keepdims=True)
    g_star = first_at(gmask, gmax) - N_EXPERTS
    lo = g_star * EXPERTS_PER_GROUP
    emask = (lane >= lo) & (lane < lo + EXPERTS_PER_GROUP)
    v1 = rmax(emask)
    i1 = first_at(emask, v1)
    mask2 = emask & (lane != i1)
    v2 = rmax(mask2)
    i2 = first_at(mask2, v2)
    e2 = jnp.exp(v2 - v1)
    w1 = p_g / (1.0 + e2)
    w2 = p_g * e2 / (1.0 + e2)
    gates_ref[...] = (jnp.where(lane == i1, w1, 0.0) + jnp.where(lane == i2, w2, 0.0)
                      + jnp.where(lane == N_EXPERTS, g_star, 0.0))


def _outproj(ys, ya, x, wos, woa, aw, fw, wr, br, tm):
    t, d = x.shape
    dm = ys.shape[1]
    rows = lambda i: (i, 0)
    const = lambda i: (0, 0)
    return pl.pallas_call(
        _outproj_kernel,
        grid=(t // tm,),
        in_specs=[
            pl.BlockSpec((tm, dm), rows),
            pl.BlockSpec((tm, dm), rows),
            pl.BlockSpec((tm, d), rows),
            pl.BlockSpec((dm, d), const),
            pl.BlockSpec((dm, d), const),
            pl.BlockSpec((1, dm), const),
            pl.BlockSpec((1, d), const),
            pl.BlockSpec((d, LANES), const),
            pl.BlockSpec((1, LANES), const),
        ],
        out_specs=[
            pl.BlockSpec((tm, d), rows),
            pl.BlockSpec((tm, d), rows),
            pl.BlockSpec((tm, LANES), rows),
        ],
        out_shape=[
            jax.ShapeDtypeStruct((t, d), F32),
            jax.ShapeDtypeStruct((t, d), BF16),
            jax.ShapeDtypeStruct((t, LANES), F32),
        ],
        compiler_params=_cparams(("parallel",)),
        name="outproj",
    )(ys, ya, x, wos, woa, aw, fw, wr, br)


MOE_BLOCK = 1024
MOE_CHUNK = 128
MOE_ROWS = MOE_BLOCK + N_EXPERT_GROUPS * MOE_CHUNK
GROUP_LANE = N_EXPERTS


def _group_runs(count):
    nch = jnp.floor((count + (MOE_CHUNK - 1)) * (1.0 / MOE_CHUNK))
    return nch, nch * MOE_CHUNK


def _moe_dispatch_kernel(xn_ref, route_ref, before_ref, xs_ref, gs_ref, meta_ref):
    pt = xn_ref.shape[0]
    route = route_ref[...]
    rt = route.T
    g_row = rt[GROUP_LANE:GROUP_LANE + 1, :]
    grp = lax.broadcasted_iota(jnp.int32, (8, pt), 0).astype(F32)
    onehot = jnp.where(grp == g_row, 1.0, 0.0)
    rank = _dot(onehot.astype(BF16), before_ref[...])
    nch, padded = _group_runs(jnp.sum(onehot, axis=1, keepdims=True))
    gi = lax.broadcasted_iota(jnp.int32, (8, 1), 0)
    start = jnp.zeros((8, 1), F32)
    for g in range(N_EXPERT_GROUPS - 1):
        start = start + jnp.where(gi > g, padded[g:g + 1, :], 0.0)
    pos = jnp.sum(onehot * (start + rank), axis=0, keepdims=True)
    dest = lax.broadcasted_iota(jnp.int32, (MOE_ROWS, pt), 0).astype(F32)
    perm = jnp.where(dest == pos, 1.0, 0.0).astype(BF16)
    g_hi = route.astype(BF16)
    g_lo = (route - g_hi.astype(F32)).astype(BF16)
    moved = _dot(perm, jnp.concatenate([xn_ref[...], g_hi, g_lo], axis=1))
    d = xn_ref.shape[1]
    xs_ref[0] = moved[:, :d].astype(BF16)
    gs_ref[0] = moved[:, d:d + LANES] + moved[:, d + LANES:]
    lane = lax.broadcasted_iota(jnp.int32, (8, LANES), 1)
    meta_ref[0] = jnp.where(lane == 0, start, jnp.where(lane == 1, nch, 0.0))


def _moe_expert_kernel(n_sub, n_exp, meta_ref, xs_ref, gs_ref, wg_ref, wu_ref, wd_ref, ys_ref):
    st = pl.program_id(0)
    eb = pl.program_id(1)
    g = (eb * n_exp) // EXPERTS_PER_GROUP

    @pl.when(eb == 0)
    def _():
        ys_ref[...] = jnp.zeros(ys_ref.shape, F32)

    def run(s, first_row, n_rows):
        rows = pl.ds(pl.multiple_of(first_row, MOE_CHUNK), n_rows)
        lane = lax.broadcasted_iota(jnp.int32, (n_rows, LANES), 1)
        xc = xs_ref[s, rows, :]
        gates = gs_ref[s, rows, :]
        y = ys_ref[s, rows, :]
        for k in range(n_exp):
            gate = jnp.sum(jnp.where(lane == eb * n_exp + k, gates, 0.0), axis=-1, keepdims=True)
            hid = _silu(_dot(xc, wg_ref[k])) * _dot(xc, wu_ref[k]) * gate
            y = y + _dot(hid.astype(BF16), wd_ref[k])
        ys_ref[s, rows, :] = y

    for s in range(n_sub):
        base = ((st * n_sub + s) * N_EXPERT_GROUPS + g) * 2
        start = meta_ref[base]
        n_chunks = meta_ref[base + 1]

        def two_chunks(c, carry, s=s, start=start):
            run(s, start + c * (2 * MOE_CHUNK), 2 * MOE_CHUNK)
            return carry
        lax.fori_loop(0, n_chunks // 2, two_chunks, 0)

        @pl.when(n_chunks % 2 == 1)
        def _(s=s, start=start, n_chunks=n_chunks):
            run(s, start + (n_chunks - 1) * MOE_CHUNK, MOE_CHUNK)


def _moe_combine_kernel(ys_ref, route_ref, x1_ref, before_ref, o_ref):
    pt = x1_ref.shape[0]
    route = route_ref[...]
    lane = lax.broadcasted_iota(jnp.int32, (pt, LANES), 1).astype(F32)
    onehot = jnp.where(lane == route[:, GROUP_LANE:GROUP_LANE + 1], 1.0, 0.0)
    rank = _dot(before_ref[...], onehot.astype(BF16))
    _, padded = _group_runs(jnp.sum(onehot, axis=0, keepdims=True))
    lane1 = lax.broadcasted_iota(jnp.int32, (1, LANES), 1)
    start = jnp.zeros((1, LANES), F32)
    for g in range(N_EXPERT_GROUPS - 1):
        start = start + jnp.where(lane1 > g, padded[:, g:g + 1], 0.0)
    pos = jnp.sum(onehot * (start + rank), axis=-1, keepdims=True)
    src = lax.broadcasted_iota(jnp.int32, (pt, MOE_ROWS), 1).astype(F32)
    perm_t = jnp.where(src == pos, 1.0, 0.0).astype(BF16)
    o_ref[...] = x1_ref[...] + _dot(perm_t, ys_ref[0].astype(BF16))


def _moe(xn, route, x1, wg, wu, wd, n_sub):
    t, d = x1.shape
    n_e, _, f = wg.shape
    pt = MOE_BLOCK
    assert t % (pt * n_sub) == 0
    nb = t // pt
    idx = jnp.arange(pt)
    before_row = (idx[:, None] < idx[None, :]).astype(BF16)
    before_col = (idx[None, :] < idx[:, None]).astype(BF16)
    rows = lambda i: (i, 0)
    const = lambda i: (0, 0)
    xs, gs, meta = pl.pallas_call(
        _moe_dispatch_kernel,
        grid=(nb,),
        in_specs=[
            pl.BlockSpec((pt, d), rows),
            pl.BlockSpec((pt, LANES), rows),
            pl.BlockSpec((pt, pt), const),
        ],
        out_specs=[
            pl.BlockSpec((1, MOE_ROWS, d), lambda i: (i, 0, 0)),
            pl.BlockSpec((1, MOE_ROWS, LANES), lambda i: (i, 0, 0)),
            pl.BlockSpec((1, 8, LANES), lambda i: (i, 0, 0)),
        ],
        out_shape=[
            jax.ShapeDtypeStruct((nb, MOE_ROWS, d), BF16),
            jax.ShapeDtypeStruct((nb, MOE_ROWS, LANES), F32),
            jax.ShapeDtypeStruct((nb, 8, LANES), F32),
        ],
        compiler_params=_cparams(("parallel",)),
        name="moe_dispatch",
    )(xn, route, before_row)

    meta_i = meta[:, :N_EXPERT_GROUPS, :2].astype(jnp.int32).reshape(-1)
    n_exp = 4
    assert EXPERTS_PER_GROUP % n_exp == 0
    ys = pl.pallas_call(
        functools.partial(_moe_expert_kernel, n_sub, n_exp),
        grid_spec=pltpu.PrefetchScalarGridSpec(
            num_scalar_prefetch=1,
            grid=(nb // n_sub, n_e // n_exp),
            in_specs=[
                pl.BlockSpec((n_sub, MOE_ROWS, d), lambda i, e, m: (i, 0, 0)),
                pl.BlockSpec((n_sub, MOE_ROWS, LANES), lambda i, e, m: (i, 0, 0)),
                pl.BlockSpec((n_exp, d, f), lambda i, e, m: (e, 0, 0)),
                pl.BlockSpec((n_exp, d, f), lambda i, e, m: (e, 0, 0)),
                pl.BlockSpec((n_exp, f, d), lambda i, e, m: (e, 0, 0)),
            ],
            out_specs=pl.BlockSpec((n_sub, MOE_ROWS, d), lambda i, e, m: (i, 0, 0)),
        ),
        out_shape=jax.ShapeDtypeStruct((nb, MOE_ROWS, d), F32),
        compiler_params=_cparams(("parallel", "arbitrary")),
        name="moe_experts",
    )(meta_i, xs, gs, wg, wu, wd)

    return pl.pallas_call(
        _moe_combine_kernel,
        grid=(nb,),
        in_specs=[
            pl.BlockSpec((1, MOE_ROWS, d), lambda i: (i, 0, 0)),
            pl.BlockSpec((pt, LANES), rows),
            pl.BlockSpec((pt, d), rows),
            pl.BlockSpec((pt, pt), const),
        ],
        out_specs=pl.BlockSpec((pt, d), rows),
        out_shape=jax.ShapeDtypeStruct((t, d), F32),
        compiler_params=_cparams(("parallel",)),
        name="moe_combine",
    )(ys, route, x1, before_col)


def _pad_lanes(v, offset=0):
    out = jnp.zeros((1, LANES), F32)
    return out.at[0, offset:offset + v.shape[0]].set(v.astype(F32))


def kernel(x, norm_mix_w, w_in, conv_w, conv_b, dt_bias, a_log, d_skip, ssm_norm_w, q_norm_w,
           k_norm_w, fox_f_bias, att_out_norm_w, w_out, norm_ffn_w, w_router_group,
           b_router_group, w_router_expert, b_router_expert, w_expert_gate, w_expert_up,
           w_expert_down):
    batch, seq, d = x.shape
    depth = w_in.shape[0]
    d_ssm = ssm_norm_w.shape[1]
    d_att = att_out_norm_w.shape[1]
    h_ssm = dt_bias.shape[1]
    h_att = fox_f_bias.shape[1]
    conv_dim = conv_w.shape[2]
    assert h_ssm == 16 and h_att == 16 and d_ssm // h_ssm == HEAD_DIM and d_att // h_att == HEAD_DIM
    assert conv_dim == d_ssm + 2 * SSM_GROUPS * D_STATE
    off_xbc = d_ssm
    off_dt = off_xbc + conv_dim
    off_q = off_dt + h_ssm
    off_k = off_q + d_att
    off_v = off_k + d_att
    off_f = off_v + d_att
    t = batch * seq
    tm_in = min(512, t)
    tm_out = min(512, t)
    moe_sub = 2 if t % (2 * MOE_BLOCK) == 0 else 1
    tq = min(512, seq)
    tk = tq

    head_of_chan = jnp.arange(d_ssm) // HEAD_DIM
    expand = (jnp.arange(LANES)[:, None] == head_of_chan[None, :]).astype(BF16)

    xf = x.reshape(t, d)
    for l in range(depth):
        wl = w_in[l]
        w_main = jnp.concatenate(
            [wl[:, :off_xbc], wl[:, off_q:off_k], wl[:, off_k:off_v], wl[:, off_v:off_f],
             wl[:, off_xbc:off_dt]], axis=1).astype(BF16)
        p_q, p_k, p_v, p_xbc = d_ssm, d_ssm + d_att, d_ssm + 2 * d_att, d_ssm + 3 * d_att
        w_small = jnp.zeros((d, LANES), F32)
        w_small = w_small.at[:, 0:h_ssm].set(wl[:, off_dt:off_q])
        w_small = w_small.at[:, 16:16 + h_att].set(wl[:, off_f:off_f + h_att]).astype(BF16)
        proj, small = _inproj(xf, norm_mix_w[l][None, :], w_main, w_small, tm_in, 512)

        bias_row = _pad_lanes(dt_bias[l]) + _pad_lanes(fox_f_bias[l], 16)
        a_row = _pad_lanes(-jnp.exp(a_log[l].astype(F32)))
        dskip_row = jnp.repeat(d_skip[l].astype(F32), HEAD_DIM)[None, :]
        y_ssm, c, ct = _ssd(proj, 0, p_xbc, conv_dim, d_ssm, small, conv_w[l].astype(F32),
                        conv_b[l].astype(F32)[None, :], bias_row, a_row, dskip_row,
                        ssm_norm_w[l][None, :], expand, batch, seq)

        qw_row = jnp.tile(q_norm_w[l].astype(F32), 2)[None, :]
        kw_row = jnp.tile(k_norm_w[l].astype(F32), 2)[None, :]
        y_att = _attn(proj, p_q, p_k, p_v, d_att, c, ct, qw_row, kw_row, batch, seq, tq, tk, 2)

        wr = jnp.zeros((d, LANES), F32)
        wr = wr.at[:, :N_EXPERTS].set(w_router_expert[l])
        wr = wr.at[:, N_EXPERTS:N_EXPERTS + N_EXPERT_GROUPS].set(w_router_group[l])
        br = _pad_lanes(b_router_expert[l]) + _pad_lanes(b_router_group[l], N_EXPERTS)
        wo = w_out[l].astype(BF16)
        x1, xn, gates = _outproj(y_ssm, y_att, xf, wo[:d_ssm], wo[d_ssm:],
                                 att_out_norm_w[l][None, :], norm_ffn_w[l][None, :], wr, br, tm_out)

        xf = _moe(xn, gates, x1, w_expert_gate[l].astype(BF16), w_expert_up[l].astype(BF16),
                  w_expert_down[l].astype(BF16), moe_sub)
    return xf.reshape(batch, seq, d)
```

```python
import functools

import jax
import jax.numpy as jnp
import numpy as np
from jax import lax
from jax.experimental import pallas as pl
from jax.experimental.pallas import tpu as pltpu

F32 = jnp.float32
BF16 = jnp.bfloat16

HEAD_DIM = 64
D_STATE = 128
SSM_GROUPS = 2
CONV_K = 4
SSD_CHUNK = 128
N_EXPERT_GROUPS = 4
EXPERTS_PER_GROUP = 8
N_EXPERTS = N_EXPERT_GROUPS * EXPERTS_PER_GROUP
EPS = 1e-6
LANES = 128
NEG_BIG = -1e30
LOG2E = 1.4426950408889634
VMEM_LIMIT = 56 * 1024 * 1024


def _cparams(sem):
    return pltpu.CompilerParams(dimension_semantics=sem, vmem_limit_bytes=VMEM_LIMIT)


def _split3(a):
    a1 = a.astype(BF16)
    r1 = a - a1.astype(F32)
    a2 = r1.astype(BF16)
    a3 = (r1 - a2.astype(F32)).astype(BF16)
    return a1, a2, a3


def _dot(a, b):
    return jnp.dot(a, b, preferred_element_type=F32)


def _dot3(a_f32, b_bf16):
    a1, a2, a3 = _split3(a_f32)
    return _dot(a1, b_bf16) + _dot(a2, b_bf16) + _dot(a3, b_bf16)


def _dot2(a_f32, b_bf16):
    a1 = a_f32.astype(BF16)
    a2 = (a_f32 - a1.astype(F32)).astype(BF16)
    return _dot(a1, b_bf16) + _dot(a2, b_bf16)


def _dot3_tn(b_bf16_t, a_f32):
    a1, a2, a3 = _split3(a_f32)
    return _dot(b_bf16_t, a1) + _dot(b_bf16_t, a2) + _dot(b_bf16_t, a3)


def _silu(x):
    return x / (1.0 + jnp.exp(-x))


def _inproj_kernel(tn, x_ref, nw_ref, w_ref, ws_ref, proj_ref, small_ref):
    x = x_ref[...]
    ms = jnp.mean(x * x, axis=-1, keepdims=True)
    hb = (x * lax.rsqrt(ms + EPS) * nw_ref[...]).astype(BF16)
    small_ref[...] = _dot(hb, ws_ref[...])
    for j in range(w_ref.shape[1] // tn):
        cols = slice(j * tn, (j + 1) * tn)
        proj_ref[:, cols] = _dot(hb, w_ref[:, cols]).astype(BF16)


def _inproj(x2, nw, w_main, w_small, tm, tn):
    t, d = x2.shape
    n = w_main.shape[1]
    assert n % tn == 0 and t % tm == 0
    return pl.pallas_call(
        functools.partial(_inproj_kernel, tn),
        grid=(t // tm,),
        in_specs=[
            pl.BlockSpec((tm, d), lambda i: (i, 0)),
            pl.BlockSpec((1, d), lambda i: (0, 0)),
            pl.BlockSpec((d, n), lambda i: (0, 0)),
            pl.BlockSpec((d, LANES), lambda i: (0, 0)),
        ],
        out_specs=[
            pl.BlockSpec((tm, n), lambda i: (i, 0)),
            pl.BlockSpec((tm, LANES), lambda i: (i, 0)),
        ],
        out_shape=[
            jax.ShapeDtypeStruct((t, n), BF16),
            jax.ShapeDtypeStruct((t, LANES), F32),
        ],
        compiler_params=_cparams(("parallel",)),
        name="inproj",
    )(x2, nw, w_main, w_small)


def _ssd_kernel(d_ssm, n_xbc, n_seq, *refs):
    ext_ref, state_ref, carry_ref = refs[-3:]
    L = SSD_CHUNK

    @pl.when(pl.program_id(1) == 0)
    def _():
        ext_ref[:, L:2 * L, :] = jnp.zeros((n_seq, L, ext_ref.shape[2]), BF16)
        state_ref[...] = jnp.zeros(state_ref.shape, F32)
        carry_ref[...] = jnp.zeros(carry_ref.shape, F32)

    for bb in range(n_seq):
        _ssd_chunk(bb, d_ssm, n_xbc, refs)


def _ssd_chunk(bb, d_ssm, n_xbc, refs):
    xbc_refs = refs[:n_xbc]
    (z_ref, small_ref, convw_ref, convb_ref, bias_ref, arow_ref, dskip_ref, nw_ref, expand_ref,
     shift_ref, y_ref, c_ref, ct_ref, ext_ref, state_ref, carry_ref) = refs[n_xbc:]
    L = SSD_CHUNK
    n_pairs = d_ssm // LANES
    gw = d_ssm // SSM_GROUPS

    u = jnp.concatenate([r[bb] for r in xbc_refs], axis=1)
    ext_ref[bb, 0:L, :] = ext_ref[bb, L:2 * L, :]
    ext_ref[bb, L:2 * L, :] = u
    n_delay = CONV_K - 1
    delayed = _dot(shift_ref[...], ext_ref[bb])
    acc = convb_ref[...] + convw_ref[n_delay:CONV_K, :] * u.astype(F32)
    for k in range(n_delay):
        acc = acc + convw_ref[k:k + 1, :] * delayed[k * L:(k + 1) * L, :]
    xbc = _silu(acc)
    xs = xbc[:, :d_ssm]
    b_all = xbc[:, d_ssm:d_ssm + SSM_GROUPS * D_STATE].astype(BF16)
    c_all = xbc[:, d_ssm + SSM_GROUPS * D_STATE:].astype(BF16)

    row = lax.broadcasted_iota(jnp.int32, (L, L), 0)
    col = lax.broadcasted_iota(jnp.int32, (L, L), 1)
    causal = row >= col
    tril = jnp.where(causal, 1.0, 0.0).astype(BF16)
    pre = small_ref[bb] + bias_ref[...]
    sp = jnp.maximum(pre, 0.0) + jnp.log(1.0 + jnp.exp(-jnp.abs(pre)))
    lane = lax.broadcasted_iota(jnp.int32, (L, LANES), 1)
    is_dt = lane < 16
    dt = jnp.where(is_dt, sp, 0.0)
    logf = jnp.where((lane >= 16) & (lane < 32), pre - sp, 0.0)
    a = dt * arow_ref[...]
    a_and_f = a + logf
    cum_all = _dot3_tn(tril, a_and_f)
    cum = jnp.where(is_dt, cum_all, 0.0)
    cum_t = cum.T
    c_run = cum_all + carry_ref[bb, 0:1, :]
    c_ref[bb] = c_run
    ct_ref[bb] = c_run.T
    carry_ref[bb, 0:1, :] = c_run[L - 1:L, :]

    expand = expand_ref[...]
    dt_x = _dot2(dt, expand)
    ecum_x = _dot2(jnp.exp(cum), expand)
    dte_x = _dot2(jnp.exp(cum[L - 1:L, :] - cum), expand)
    xdt = xs * dt_x
    xdt_b = xdt.astype(BF16)
    w_b = (xdt * dte_x).astype(BF16)

    y_parts = []
    for g in range(SSM_GROUPS):
        bg = b_all[:, g * D_STATE:(g + 1) * D_STATE]
        cg = c_all[:, g * D_STATE:(g + 1) * D_STATE]
        cb = lax.dot_general(cg, bg, (((1,), (1,)), ((), ())), preferred_element_type=F32)
        s_old = state_ref[bb, g]
        y_off = _dot(cg, s_old.astype(BF16)) * ecum_x[:, g * gw:(g + 1) * gw]
        upd = lax.dot_general(bg, w_b[:, g * gw:(g + 1) * gw], (((0,), (0,)), ((), ())),
                              preferred_element_type=F32)
        state_ref[bb, g] = s_old * ecum_x[L - 1:L, g * gw:(g + 1) * gw] + upd
        pairs_per_group = n_pairs // SSM_GROUPS
        for pp in range(pairs_per_group):
            p = g * pairs_per_group + pp
            xp = xdt_b[:, p * LANES:(p + 1) * LANES]
            ys = []
            for hh in range(2):
                h = 2 * p + hh
                seg = cum[:, h:h + 1] - cum_t[h:h + 1, :]
                decay = jnp.exp(jnp.where(causal, seg, NEG_BIG))
                ys.append(_dot((cb * decay).astype(BF16), xp))
            first = lax.broadcasted_iota(jnp.int32, (L, LANES), 1) < HEAD_DIM
            y_parts.append(jnp.where(first, ys[0], ys[1])
                           + y_off[:, pp * LANES:(pp + 1) * LANES])
    y = jnp.concatenate(y_parts, axis=1) + xs * dskip_ref[...]
    gated = y * _silu(z_ref[bb].astype(F32))
    ms = jnp.mean(gated * gated, axis=-1, keepdims=True)
    y_ref[bb] = (gated * lax.rsqrt(ms + EPS) * nw_ref[...]).astype(BF16)


def _ssd(proj, z_off, xbc_off, cdim, d_ssm, small, convw, convb, bias_row, a_row, dskip_row, nw,
         expand, batch, seq):
    t = proj.shape[0]
    L = SSD_CHUNK
    nc = seq // L
    wx = 512
    assert cdim % wx == 0 and xbc_off % wx == 0 and z_off % d_ssm == 0
    n_xbc = cdim // wx
    n_seq = 2 if batch % 2 == 0 else 1
    proj3 = proj.reshape(batch, seq, proj.shape[1])
    small3 = small.reshape(batch, seq, LANES)
    rowmap = lambda b, c: (b, c, 0)
    const = lambda b, c: (0, 0)
    n_delay = CONV_K - 1
    r = jnp.arange(n_delay * L)
    target = L + r % L - (n_delay - r // L)
    shift = (jnp.arange(2 * L)[None, :] == target[:, None]).astype(BF16)
    xbc_specs = [pl.BlockSpec((n_seq, L, wx), lambda b, c, o=xbc_off // wx + i: (b, c, o))
                 for i in range(n_xbc)]
    y, c_out, ct = pl.pallas_call(
        functools.partial(_ssd_kernel, d_ssm, n_xbc, n_seq),
        grid=(batch // n_seq, nc),
        in_specs=xbc_specs + [
            pl.BlockSpec((n_seq, L, d_ssm), lambda b, c: (b, c, z_off // d_ssm)),
            pl.BlockSpec((n_seq, L, LANES), rowmap),
            pl.BlockSpec((CONV_K, cdim), const),
            pl.BlockSpec((1, cdim), const),
            pl.BlockSpec((1, LANES), const),
            pl.BlockSpec((1, LANES), const),
            pl.BlockSpec((1, d_ssm), const),
            pl.BlockSpec((1, d_ssm), const),
            pl.BlockSpec((LANES, d_ssm), const),
            pl.BlockSpec((n_delay * L, 2 * L), const),
        ],
        out_specs=[
            pl.BlockSpec((n_seq, L, d_ssm), rowmap),
            pl.BlockSpec((n_seq, L, LANES), rowmap),
            pl.BlockSpec((n_seq, LANES, L), lambda b, c: (b, 0, c)),
        ],
        out_shape=[
            jax.ShapeDtypeStruct((batch, seq, d_ssm), BF16),
            jax.ShapeDtypeStruct((batch, seq, LANES), F32),
            jax.ShapeDtypeStruct((batch, LANES, seq), F32),
        ],
        scratch_shapes=[
            pltpu.VMEM((n_seq, 2 * L, cdim), BF16),
            pltpu.VMEM((n_seq, SSM_GROUPS, D_STATE, d_ssm // SSM_GROUPS), F32),
            pltpu.VMEM((n_seq, 8, LANES), F32),
        ],
        compiler_params=_cparams(("parallel", "arbitrary")),
        name="ssd",
    )(*([proj3] * n_xbc), proj3, small3, convw, convb, bias_row, a_row, dskip_row, nw, expand, shift)
    return y.reshape(t, d_ssm), c_out.reshape(t, LANES), ct


def _headnorm(x, w_row, same_head):
    ms = _dot((x * x).astype(BF16), same_head) * (1.0 / HEAD_DIM)
    return x * lax.rsqrt(ms + EPS) * w_row


def _place(parts, moves, rr, cc):
    out = None
    for i, part in enumerate(parts):
        sel = None
        for src, dst in moves:
            one = jnp.where((rr == src) & (cc == dst + i), 1.0, 0.0)
            sel = one if sel is None else sel + one
        term = _dot(part, sel.astype(BF16))
        out = term if out is None else out + term
    return out


def _attn_kernel(tq, tk, npair, q_ref, k_ref, v_ref, c_ref, ct_ref, qw_ref, kw_ref, o_ref,
                 kx_ref, vx_ref, qx_ref, sa_ref, sb_ref, m_ref, l_ref, acc_ref):
    grp = pl.program_id(1)
    qi = pl.program_id(2)
    seq = k_ref.shape[1]
    lane = lax.broadcasted_iota(jnp.int32, (1, LANES), 1)
    first = lane < HEAD_DIM
    rr = lax.broadcasted_iota(jnp.int32, (LANES, LANES), 0)
    cc = lax.broadcasted_iota(jnp.int32, (LANES, LANES), 1)
    same_head = jnp.where((rr < HEAD_DIM) == (cc < HEAD_DIM), 1.0, 0.0).astype(BF16)
    pairs = range(npair)
    bases = [16 + 2 * (grp * npair + p) for p in pairs]

    @pl.when(qi == 0)
    def _():
        k_const = jnp.where((lane >= 6) & (lane < 9), 1.0, 0.0)

        def body(r, carry):
            rows = pl.ds(pl.multiple_of(r * tq, tq), tq)
            parts = _split3(c_ref[0, rows, :] * LOG2E)
            for p in pairs:
                lanes = slice(p * LANES, (p + 1) * LANES)
                kn = _headnorm(k_ref[0, rows, lanes].astype(F32), kw_ref[...], same_head)
                kext = _place(parts, [(bases[p], 0), (bases[p] + 1, 3)], rr, cc) + k_const
                kx_ref[p, rows, 0:LANES] = kn.astype(BF16)
                kx_ref[p, rows, LANES:2 * LANES] = kext.astype(BF16)
                v = v_ref[0, rows, lanes].astype(F32)
                vt = v.T.astype(BF16)
                vx_ref[p, 0, :, rows] = vt[0:HEAD_DIM]
                vx_ref[p, 1, :, rows] = vt[HEAD_DIM:LANES]
            return carry
        lax.fori_loop(0, seq // tq, body, 0)

    scale = (1.0 / np.sqrt(HEAD_DIM)) * LOG2E
    sub = lax.broadcasted_iota(jnp.int32, (LANES, tq), 0)
    for p in pairs:
        qn = _headnorm(q_ref[0, :, p * LANES:(p + 1) * LANES].astype(F32), qw_ref[...],
                       same_head) * scale
        qt = qn.T
        for hh in range(2):
            c_hi, c_mid, c_lo = _split3(ct_ref[0, pl.ds(bases[p] + hh, 1), :] * LOG2E)
            minus = (sub >= 3 * hh) & (sub < 3 * hh + 3)
            ext = jnp.where(sub == 6, c_hi.astype(F32), jnp.where(sub == 7, c_mid.astype(F32),
                  jnp.where(sub == 8, c_lo.astype(F32), jnp.where(minus, -1.0, 0.0))))
            own = (sub < HEAD_DIM) if hh == 0 else (sub >= HEAD_DIM)
            qx_ref[p, 0:LANES, hh * tq:(hh + 1) * tq] = jnp.where(own, qt, 0.0).astype(BF16)
            qx_ref[p, LANES:2 * LANES, hh * tq:(hh + 1) * tq] = ext.astype(BF16)
        m_ref[p] = jnp.full((1, 2 * tq), NEG_BIG, F32)
        l_ref[p] = jnp.zeros((1, 2 * tq), F32)
        acc_ref[p] = jnp.zeros((2, HEAD_DIM, tq), F32)

    n_full = (qi * tq) // tk
    q_first = qi * tq - n_full * tk
    key_i = lax.broadcasted_iota(jnp.int32, (tk, 2 * tq), 0)
    qry_i = lax.broadcasted_iota(jnp.int32, (tk, 2 * tq), 1)
    keep = jnp.where(qry_i >= tq, qry_i - tq, qry_i) + q_first >= key_i

    def scores(j, s_ref):
        rows = pl.ds(pl.multiple_of(j * tk, tk), tk)
        for p in pairs:
            s_ref[p] = _dot(kx_ref[p, rows, :], qx_ref[p])

    def softmax_pv(j, s_ref, masked):
        cols = pl.ds(pl.multiple_of(j * tk, tk), tk)
        for p in pairs:
            s = s_ref[p]
            if masked:
                s = jnp.where(keep, s, NEG_BIG)
            m = m_ref[p]
            m_new = jnp.maximum(m, jnp.max(s, axis=0, keepdims=True))
            alpha = jnp.exp2(m - m_new)
            pw = jnp.exp2(s - m_new)
            l_ref[p] = alpha * l_ref[p] + jnp.sum(pw, axis=0, keepdims=True)
            pb = pw.astype(BF16)
            acc_ref[p, 0] = alpha[:, :tq] * acc_ref[p, 0] + _dot(vx_ref[p, 0, :, cols], pb[:, :tq])
            acc_ref[p, 1] = alpha[:, tq:] * acc_ref[p, 1] + _dot(vx_ref[p, 1, :, cols], pb[:, tq:])
            m_ref[p] = m_new

    scores(0, sa_ref)

    def two_blocks(i, carry):
        j = 2 * i
        scores(j + 1, sb_ref)
        softmax_pv(j, sa_ref, False)
        scores(j + 2, sa_ref)
        softmax_pv(j + 1, sb_ref, False)
        return carry
    lax.fori_loop(0, n_full // 2, two_blocks, 0)

    @pl.when(n_full % 2 == 0)
    def _():
        softmax_pv(n_full, sa_ref, True)

    @pl.when(n_full % 2 == 1)
    def _():
        scores(n_full, sb_ref)
        softmax_pv(n_full - 1, sa_ref, False)
        softmax_pv(n_full, sb_ref, True)

    for p in pairs:
        out_t = jnp.concatenate(
            [acc_ref[p, 0] / l_ref[p, :, 0:tq], acc_ref[p, 1] / l_ref[p, :, tq:2 * tq]], axis=0)
        o_ref[0, :, p * LANES:(p + 1) * LANES] = out_t.T.astype(BF16)


def _attn(proj, q_off, k_off, v_off, d_att, c, ct, qw_row, kw_row, batch, seq, tq, tk, npair):
    width = npair * LANES
    assert d_att % width == 0 and seq % tq == 0 and seq % tk == 0 and tk % tq == 0
    assert q_off % width == 0 and k_off % width == 0 and v_off % width == 0
    qo, ko, vo = q_off // width, k_off // width, v_off // width
    q3 = k3 = v3 = proj.reshape(batch, seq, proj.shape[-1])
    c3 = c.reshape(batch, seq, LANES)
    return pl.pallas_call(
        functools.partial(_attn_kernel, tq, tk, npair),
        grid=(batch, d_att // width, seq // tq),
        in_specs=[
            pl.BlockSpec((1, tq, width), lambda b, h, i: (b, i, qo + h)),
            pl.BlockSpec((1, seq, width), lambda b, h, i: (b, 0, ko + h)),
            pl.BlockSpec((1, seq, width), lambda b, h, i: (b, 0, vo + h)),
            pl.BlockSpec((1, seq, LANES), lambda b, h, i: (b, 0, 0)),
            pl.BlockSpec((1, LANES, tq), lambda b, h, i: (b, 0, i)),
            pl.BlockSpec((1, LANES), lambda b, h, i: (0, 0)),
            pl.BlockSpec((1, LANES), lambda b, h, i: (0, 0)),
        ],
        out_specs=pl.BlockSpec((1, tq, width), lambda b, h, i: (b, i, h)),
        out_shape=jax.ShapeDtypeStruct((batch, seq, d_att), BF16),
        scratch_shapes=[
            pltpu.VMEM((npair, seq, 2 * LANES), BF16),
            pltpu.VMEM((npair, 2, HEAD_DIM, seq), BF16),
            pltpu.VMEM((npair, 2 * LANES, 2 * tq), BF16),
            pltpu.VMEM((npair, tk, 2 * tq), F32),
            pltpu.VMEM((npair, tk, 2 * tq), F32),
            pltpu.VMEM((npair, 1, 2 * tq), F32),
            pltpu.VMEM((npair, 1, 2 * tq), F32),
            pltpu.VMEM((npair, 2, HEAD_DIM, tq), F32),
        ],
        compiler_params=_cparams(("parallel", "parallel", "arbitrary")),
        name="attn",
    )(q3, k3, v3, c3, ct, qw_row, kw_row).reshape(batch * seq, d_att)


def _outproj_kernel(ys_ref, ya_ref, x_ref, wos_ref, woa_ref, aw_ref, fw_ref, wr_ref, br_ref,
                    x1_ref, xn_ref, gates_ref):
    ya = ya_ref[...].astype(F32)
    ya = ya * lax.rsqrt(jnp.mean(ya * ya, axis=-1, keepdims=True) + EPS) * aw_ref[...]
    x1 = x_ref[...] + _dot(ys_ref[...], wos_ref[...]) + _dot(ya.astype(BF16), woa_ref[...])
    x1_ref[...] = x1
    xn = x1 * lax.rsqrt(jnp.mean(x1 * x1, axis=-1, keepdims=True) + EPS) * fw_ref[...]
    xn_ref[...] = xn.astype(BF16)

    x_hi = xn.astype(BF16)
    x_lo = (xn - x_hi.astype(F32)).astype(BF16)
    w = wr_ref[...]
    w_hi = w.astype(BF16)
    w_lo = (w - w_hi.astype(F32)).astype(BF16)
    hi_terms = _dot(x_hi, jnp.concatenate([w_hi, w_lo], axis=1))
    logits = hi_terms[:, :LANES] + _dot(x_lo, w_hi) + hi_terms[:, LANES:] + br_ref[...]
    lane = lax.broadcasted_iota(jnp.int32, logits.shape, 1).astype(F32)

    def rmax(mask):
        return jnp.max(jnp.where(mask, logits, NEG_BIG), axis=-1, keepdims=True)

    def first_at(mask, val):
        return jnp.min(jnp.where(mask & (logits == val), lane, 1e9), axis=-1, keepdims=True)

    gmask = (lane >= N_EXPERTS) & (lane < N_EXPERTS + N_EXPERT_GROUPS)
    gmax = rmax(gmask)
    p_g = 1.0 / jnp.sum(jnp.where(gmask, jnp.exp(logits - gmax), 0.0), axis=-1, keepdims=True)
    g_star = first_at(gmask, gmax) - N_EXPERTS
    lo = g_star * EXPERTS_PER_GROUP
    emask = (lane >= lo) & (lane < lo + EXPERTS_PER_GROUP)
    v1 = rmax(emask)
    i1 = first_at(emask, v1)
    mask2 = emask & (lane != i1)
    v2 = rmax(mask2)
    i2 = first_at(mask2, v2)
    e2 = jnp.exp(v2 - v1)
    w1 = p_g / (1.0 + e2)
    w2 = p_g * e2 / (1.0 + e2)
    gates_ref[...] = (jnp.where(lane == i1, w1, 0.0) + jnp.where(lane == i2, w2, 0.0)
                      + jnp.where(lane == N_EXPERTS, g_star, 0.0))


def _outproj(ys, ya, x, wos, woa, aw, fw, wr, br, tm):
    t, d = x.shape
    dm = ys.shape[1]
    rows = lambda i: (i, 0)
    const = lambda i: (0, 0)
    return pl.pallas_call(
        _outproj_kernel,
        grid=(t // tm,),
        in_specs=[
            pl.BlockSpec((tm, dm), rows),
            pl.BlockSpec((tm, dm), rows),
            pl.BlockSpec((tm, d), rows),
            pl.BlockSpec((dm, d), const),
            pl.BlockSpec((dm, d), const),
            pl.BlockSpec((1, dm), const),
            pl.BlockSpec((1, d), const),
            pl.BlockSpec((d, LANES), const),
            pl.BlockSpec((1, LANES), const),
        ],
        out_specs=[
            pl.BlockSpec((tm, d), rows),
            pl.BlockSpec((tm, d), rows),
            pl.BlockSpec((tm, LANES), rows),
        ],
        out_shape=[
            jax.ShapeDtypeStruct((t, d), F32),
            jax.ShapeDtypeStruct((t, d), BF16),
            jax.ShapeDtypeStruct((t, LANES), F32),
        ],
        compiler_params=_cparams(("parallel",)),
        name="outproj",
    )(ys, ya, x, wos, woa, aw, fw, wr, br)


MOE_BLOCK = 1024
MOE_CHUNK = 128
MOE_ROWS = MOE_BLOCK + N_EXPERT_GROUPS * MOE_CHUNK
GROUP_LANE = N_EXPERTS


def _group_runs(count):
    nch = jnp.floor((count + (MOE_CHUNK - 1)) * (1.0 / MOE_CHUNK))
    return nch, nch * MOE_CHUNK


def _moe_dispatch_kernel(xn_ref, route_ref, before_ref, xs_ref, gs_ref, meta_ref):
    pt = xn_ref.shape[0]
    route = route_ref[...]
    rt = route.T
    g_row = rt[GROUP_LANE:GROUP_LANE + 1, :]
    grp = lax.broadcasted_iota(jnp.int32, (8, pt), 0).astype(F32)
    onehot = jnp.where(grp == g_row, 1.0, 0.0)
    rank = _dot(onehot.astype(BF16), before_ref[...])
    nch, padded = _group_runs(jnp.sum(onehot, axis=1, keepdims=True))
    gi = lax.broadcasted_iota(jnp.int32, (8, 1), 0)
    start = jnp.zeros((8, 1), F32)
    for g in range(N_EXPERT_GROUPS - 1):
        start = start + jnp.where(gi > g, padded[g:g + 1, :], 0.0)
    pos = jnp.sum(onehot * (start + rank), axis=0, keepdims=True)
    dest = lax.broadcasted_iota(jnp.int32, (MOE_ROWS, pt), 0).astype(F32)
    perm = jnp.where(dest == pos, 1.0, 0.0).astype(BF16)
    g_hi = route.astype(BF16)
    g_lo = (route - g_hi.astype(F32)).astype(BF16)
    moved = _dot(perm, jnp.concatenate([xn_ref[...], g_hi, g_lo], axis=1))
    d = xn_ref.shape[1]
    xs_ref[0] = moved[:, :d].astype(BF16)
    gs_ref[0] = moved[:, d:d + LANES] + moved[:, d + LANES:]
    lane = lax.broadcasted_iota(jnp.int32, (8, LANES), 1)
    meta_ref[0] = jnp.where(lane == 0, start, jnp.where(lane == 1, nch, 0.0))


def _moe_expert_kernel(n_sub, n_exp, meta_ref, xs_ref, gs_ref, wg_ref, wu_ref, wd_ref, ys_ref):
    st = pl.program_id(0)
    eb = pl.program_id(1)
    g = (eb * n_exp) // EXPERTS_PER_GROUP

    @pl.when(eb == 0)
    def _():
        ys_ref[...] = jnp.zeros(ys_ref.shape, F32)

    def run(s, first_row, n_rows):
        rows = pl.ds(pl.multiple_of(first_row, MOE_CHUNK), n_rows)
        lane = lax.broadcasted_iota(jnp.int32, (n_rows, LANES), 1)
        xc = xs_ref[s, rows, :]
        gates = gs_ref[s, rows, :]
        y = ys_ref[s, rows, :]
        for k in range(n_exp):
            gate = jnp.sum(jnp.where(lane == eb * n_exp + k, gates, 0.0), axis=-1, keepdims=True)
            hid = _silu(_dot(xc, wg_ref[k])) * _dot(xc, wu_ref[k]) * gate
            y = y + _dot(hid.astype(BF16), wd_ref[k])
        ys_ref[s, rows, :] = y

    for s in range(n_sub):
        base = ((st * n_sub + s) * N_EXPERT_GROUPS + g) * 2
        start = meta_ref[base]
        n_chunks = meta_ref[base + 1]

        def two_chunks(c, carry, s=s, start=start):
            run(s, start + c * (2 * MOE_CHUNK), 2 * MOE_CHUNK)
            return carry
        lax.fori_loop(0, n_chunks // 2, two_chunks, 0)

        @pl.when(n_chunks % 2 == 1)
        def _(s=s, start=start, n_chunks=n_chunks):
            run(s, start + (n_chunks - 1) * MOE_CHUNK, MOE_CHUNK)


def _moe_combine_kernel(ys_ref, route_ref, x1_ref, before_ref, o_ref):
    pt = x1_ref.shape[0]
    route = route_ref[...]
    lane = lax.broadcasted_iota(jnp.int32, (pt, LANES), 1).astype(F32)
    onehot = jnp.where(lane == route[:, GROUP_LANE:GROUP_LANE + 1], 1.0, 0.0)
    rank = _dot(before_ref[...], onehot.astype(BF16))
    _, padded = _group_runs(jnp.sum(onehot, axis=0, keepdims=True))
    lane1 = lax.broadcasted_iota(jnp.int32, (1, LANES), 1)
    start = jnp.zeros((1, LANES), F32)
    for g in range(N_EXPERT_GROUPS - 1):
        start = start + jnp.where(lane1 > g, padded[:, g:g + 1], 0.0)
    pos = jnp.sum(onehot * (start + rank), axis=-1, keepdims=True)
    src = lax.broadcasted_iota(jnp.int32, (pt, MOE_ROWS), 1).astype(F32)
    perm_t = jnp.where(src == pos, 1.0, 0.0).astype(BF16)
    o_ref[...] = x1_ref[...] + _dot(perm_t, ys_ref[0].astype(BF16))


def _moe(xn, route, x1, wg, wu, wd, n_sub):
    t, d = x1.shape
    n_e, _, f = wg.shape
    pt = MOE_BLOCK
    assert t % (pt * n_sub) == 0
    nb = t // pt
    idx = jnp.arange(pt)
    before_row = (idx[:, None] < idx[None, :]).astype(BF16)
    before_col = (idx[None, :] < idx[:, None]).astype(BF16)
    rows = lambda i: (i, 0)
    const = lambda i: (0, 0)
    xs, gs, meta = pl.pallas_call(
        _moe_dispatch_kernel,
        grid=(nb,),
        in_specs=[
            pl.BlockSpec((pt, d), rows),
            pl.BlockSpec((pt, LANES), rows),
            pl.BlockSpec((pt, pt), const),
        ],
        out_specs=[
            pl.BlockSpec((1, MOE_ROWS, d), lambda i: (i, 0, 0)),
            pl.BlockSpec((1, MOE_ROWS, LANES), lambda i: (i, 0, 0)),
            pl.BlockSpec((1, 8, LANES), lambda i: (i, 0, 0)),
        ],
        out_shape=[
            jax.ShapeDtypeStruct((nb, MOE_ROWS, d), BF16),
            jax.ShapeDtypeStruct((nb, MOE_ROWS, LANES), F32),
            jax.ShapeDtypeStruct((nb, 8, LANES), F32),
        ],
        compiler_params=_cparams(("parallel",)),
        name="moe_dispatch",
    )(xn, route, before_row)

    meta_i = meta[:, :N_EXPERT_GROUPS, :2].astype(jnp.int32).reshape(-1)
    n_exp = 4
    assert EXPERTS_PER_GROUP % n_exp == 0
    ys = pl.pallas_call(
        functools.partial(_moe_expert_kernel, n_sub, n_exp),
        grid_spec=pltpu.PrefetchScalarGridSpec(
            num_scalar_prefetch=1,
            grid=(nb // n_sub, n_e // n_exp),
            in_specs=[
                pl.BlockSpec((n_sub, MOE_ROWS, d), lambda i, e, m: (i, 0, 0)),
                pl.BlockSpec((n_sub, MOE_ROWS, LANES), lambda i, e, m: (i, 0, 0)),
                pl.BlockSpec((n_exp, d, f), lambda i, e, m: (e, 0, 0)),
                pl.BlockSpec((n_exp, d, f), lambda i, e, m: (e, 0, 0)),
                pl.BlockSpec((n_exp, f, d), lambda i, e, m: (e, 0, 0)),
            ],
            out_specs=pl.BlockSpec((n_sub, MOE_ROWS, d), lambda i, e, m: (i, 0, 0)),
        ),
        out_shape=jax.ShapeDtypeStruct((nb, MOE_ROWS, d), F32),
        compiler_params=_cparams(("parallel", "arbitrary")),
        name="moe_experts",
    )(meta_i, xs, gs, wg, wu, wd)

    return pl.pallas_call(
        _moe_combine_kernel,
        grid=(nb,),
        in_specs=[
            pl.BlockSpec((1, MOE_ROWS, d), lambda i: (i, 0, 0)),
            pl.BlockSpec((pt, LANES), rows),
            pl.BlockSpec((pt, d), rows),
            pl.BlockSpec((pt, pt), const),
        ],
        out_specs=pl.BlockSpec((pt, d), rows),
        out_shape=jax.ShapeDtypeStruct((t, d), F32),
        compiler_params=_cparams(("parallel",)),
        name="moe_combine",
    )(ys, route, x1, before_col)


def _pad_lanes(v, offset=0):
    out = jnp.zeros((1, LANES), F32)
    return out.at[0, offset:offset + v.shape[0]].set(v.astype(F32))


def kernel(x, norm_mix_w, w_in, conv_w, conv_b, dt_bias, a_log, d_skip, ssm_norm_w, q_norm_w,
           k_norm_w, fox_f_bias, att_out_norm_w, w_out, norm_ffn_w, w_router_group,
           b_router_group, w_router_expert, b_router_expert, w_expert_gate, w_expert_up,
           w_expert_down):
    batch, seq, d = x.shape
    depth = w_in.shape[0]
    d_ssm = ssm_norm_w.shape[1]
    d_att = att_out_norm_w.shape[1]
    h_ssm = dt_bias.shape[1]
    h_att = fox_f_bias.shape[1]
    conv_dim = conv_w.shape[2]
    assert h_ssm == 16 and h_att == 16 and d_ssm // h_ssm == HEAD_DIM and d_att // h_att == HEAD_DIM
    assert conv_dim == d_ssm + 2 * SSM_GROUPS * D_STATE
    off_xbc = d_ssm
    off_dt = off_xbc + conv_dim
    off_q = off_dt + h_ssm
    off_k = off_q + d_att
    off_v = off_k + d_att
    off_f = off_v + d_att
    t = batch * seq
    tm_in = min(512, t)
    tm_out = min(512, t)
    moe_sub = 2 if t % (2 * MOE_BLOCK) == 0 else 1
    tq = min(512, seq)
    tk = tq

    head_of_chan = jnp.arange(d_ssm) // HEAD_DIM
    expand = (jnp.arange(LANES)[:, None] == head_of_chan[None, :]).astype(BF16)

    xf = x.reshape(t, d)
    for l in range(depth):
        wl = w_in[l]
        w_main = jnp.concatenate(
            [wl[:, :off_xbc], wl[:, off_q:off_k], wl[:, off_k:off_v], wl[:, off_v:off_f],
             wl[:, off_xbc:off_dt]], axis=1).astype(BF16)
        p_q, p_k, p_v, p_xbc = d_ssm, d_ssm + d_att, d_ssm + 2 * d_att, d_ssm + 3 * d_att
        w_small = jnp.zeros((d, LANES), F32)
        w_small = w_small.at[:, 0:h_ssm].set(wl[:, off_dt:off_q])
        w_small = w_small.at[:, 16:16 + h_att].set(wl[:, off_f:off_f + h_att]).astype(BF16)
        proj, small = _inproj(xf, norm_mix_w[l][None, :], w_main, w_small, tm_in, 512)

        bias_row = _pad_lanes(dt_bias[l]) + _pad_lanes(fox_f_bias[l], 16)
        a_row = _pad_lanes(-jnp.exp(a_log[l].astype(F32)))
        dskip_row = jnp.repeat(d_skip[l].astype(F32), HEAD_DIM)[None, :]
        y_ssm, c, ct = _ssd(proj, 0, p_xbc, conv_dim, d_ssm, small, conv_w[l].astype(F32),
                        conv_b[l].astype(F32)[None, :], bias_row, a_row, dskip_row,
                        ssm_norm_w[l][None, :], expand, batch, seq)

        qw_row = jnp.tile(q_norm_w[l].astype(F32), 2)[None, :]
        kw_row = jnp.tile(k_norm_w[l].astype(F32), 2)[None, :]
        y_att = _attn(proj, p_q, p_k, p_v, d_att, c, ct, qw_row, kw_row, batch, seq, tq, tk, 2)

        wr = jnp.zeros((d, LANES), F32)
        wr = wr.at[:, :N_EXPERTS].set(w_router_expert[l])
        wr = wr.at[:, N_EXPERTS:N_EXPERTS + N_EXPERT_GROUPS].set(w_router_group[l])
        br = _pad_lanes(b_router_expert[l]) + _pad_lanes(b_router_group[l], N_EXPERTS)
        wo = w_out[l].astype(BF16)
        x1, xn, gates = _outproj(y_ssm, y_att, xf, wo[:d_ssm], wo[d_ssm:],
                                 att_out_norm_w[l][None, :], norm_ffn_w[l][None, :], wr, br, tm_out)

        xf = _moe(xn, gates, x1, w_expert_gate[l].astype(BF16), w_expert_up[l].astype(BF16),
                  w_expert_down[l].astype(BF16), moe_sub)
    return xf.reshape(batch, seq, d)
```

```python
import functools

import jax
import jax.numpy as jnp
import numpy as np
from jax import lax
from jax.experimental import pallas as pl
from jax.experimental.pallas import tpu as pltpu

F32 = jnp.float32
BF16 = jnp.bfloat16

HEAD_DIM = 64
D_STATE = 128
SSM_GROUPS = 2
CONV_K = 4
SSD_CHUNK = 128
N_EXPERT_GROUPS = 4
EXPERTS_PER_GROUP = 8
N_EXPERTS = N_EXPERT_GROUPS * EXPERTS_PER_GROUP
EPS = 1e-6
LANES = 128
NEG_BIG = -1e30
LOG2E = 1.4426950408889634
VMEM_LIMIT = 56 * 1024 * 1024


def _cparams(sem):
    return pltpu.CompilerParams(dimension_semantics=sem, vmem_limit_bytes=VMEM_LIMIT)


def _split3(a):
    a1 = a.astype(BF16)
    r1 = a - a1.astype(F32)
    a2 = r1.astype(BF16)
    a3 = (r1 - a2.astype(F32)).astype(BF16)
    return a1, a2, a3


def _dot(a, b):
    return jnp.dot(a, b, preferred_element_type=F32)


def _dot3(a_f32, b_bf16):
    a1, a2, a3 = _split3(a_f32)
    return _dot(a1, b_bf16) + _dot(a2, b_bf16) + _dot(a3, b_bf16)


def _dot2(a_f32, b_bf16):
    a1 = a_f32.astype(BF16)
    a2 = (a_f32 - a1.astype(F32)).astype(BF16)
    return _dot(a1, b_bf16) + _dot(a2, b_bf16)


def _dot3_tn(b_bf16_t, a_f32):
    a1, a2, a3 = _split3(a_f32)
    return _dot(b_bf16_t, a1) + _dot(b_bf16_t, a2) + _dot(b_bf16_t, a3)


def _silu(x):
    return x / (1.0 + jnp.exp(-x))


def _inproj_kernel(tn, x_ref, nw_ref, w_ref, ws_ref, proj_ref, small_ref):
    x = x_ref[...]
    ms = jnp.mean(x * x, axis=-1, keepdims=True)
    hb = (x * lax.rsqrt(ms + EPS) * nw_ref[...]).astype(BF16)
    small_ref[...] = _dot(hb, ws_ref[...])
    for j in range(w_ref.shape[1] // tn):
        cols = slice(j * tn, (j + 1) * tn)
        proj_ref[:, cols] = _dot(hb, w_ref[:, cols]).astype(BF16)


def _inproj(x2, nw, w_main, w_small, tm, tn):
    t, d = x2.shape
    n = w_main.shape[1]
    assert n % tn == 0 and t % tm == 0
    return pl.pallas_call(
        functools.partial(_inproj_kernel, tn),
        grid=(t // tm,),
        in_specs=[
            pl.BlockSpec((tm, d), lambda i: (i, 0)),
            pl.BlockSpec((1, d), lambda i: (0, 0)),
            pl.BlockSpec((d, n), lambda i: (0, 0)),
            pl.BlockSpec((d, LANES), lambda i: (0, 0)),
        ],
        out_specs=[
            pl.BlockSpec((tm, n), lambda i: (i, 0)),
            pl.BlockSpec((tm, LANES), lambda i: (i, 0)),
        ],
        out_shape=[
            jax.ShapeDtypeStruct((t, n), BF16),
            jax.ShapeDtypeStruct((t, LANES), F32),
        ],
        compiler_params=_cparams(("parallel",)),
        name="inproj",
    )(x2, nw, w_main, w_small)


def _ssd_kernel(d_ssm, n_xbc, n_seq, *refs):
    ext_ref, state_ref, carry_ref = refs[-3:]
    L = SSD_CHUNK

    @pl.when(pl.program_id(1) == 0)
    def _():
        ext_ref[:, L:2 * L, :] = jnp.zeros((n_seq, L, ext_ref.shape[2]), BF16)
        state_ref[...] = jnp.zeros(state_ref.shape, F32)
        carry_ref[...] = jnp.zeros(carry_ref.shape, F32)

    for bb in range(n_seq):
        _ssd_chunk(bb, d_ssm, n_xbc, refs)


def _ssd_chunk(bb, d_ssm, n_xbc, refs):
    xbc_refs = refs[:n_xbc]
    (z_ref, small_ref, convw_ref, convb_ref, bias_ref, arow_ref, dskip_ref, nw_ref, expand_ref,
     shift_ref, y_ref, c_ref, ct_ref, ext_ref, state_ref, carry_ref) = refs[n_xbc:]
    L = SSD_CHUNK
    n_pairs = d_ssm // LANES
    gw = d_ssm // SSM_GROUPS

    u = jnp.concatenate([r[bb] for r in xbc_refs], axis=1)
    ext_ref[bb, 0:L, :] = ext_ref[bb, L:2 * L, :]
    ext_ref[bb, L:2 * L, :] = u
    n_delay = CONV_K - 1
    delayed = _dot(shift_ref[...], ext_ref[bb])
    acc = convb_ref[...] + convw_ref[n_delay:CONV_K, :] * u.astype(F32)
    for k in range(n_delay):
        acc = acc + convw_ref[k:k + 1, :] * delayed[k * L:(k + 1) * L, :]
    xbc = _silu(acc)
    xs = xbc[:, :d_ssm]
    b_all = xbc[:, d_ssm:d_ssm + SSM_GROUPS * D_STATE].astype(BF16)
    c_all = xbc[:, d_ssm + SSM_GROUPS * D_STATE:].astype(BF16)

    row = lax.broadcasted_iota(jnp.int32, (L, L), 0)
    col = lax.broadcasted_iota(jnp.int32, (L, L), 1)
    causal = row >= col
    tril = jnp.where(causal, 1.0, 0.0).astype(BF16)
    pre = small_ref[bb] + bias_ref[...]
    sp = jnp.maximum(pre, 0.0) + jnp.log(1.0 + jnp.exp(-jnp.abs(pre)))
    lane = lax.broadcasted_iota(jnp.int32, (L, LANES), 1)
    is_dt = lane < 16
    dt = jnp.where(is_dt, sp, 0.0)
    logf = jnp.where((lane >= 16) & (lane < 32), pre - sp, 0.0)
    a = dt * arow_ref[...]
    a_and_f = a + logf
    cum_all = _dot3_tn(tril, a_and_f)
    cum = jnp.where(is_dt, cum_all, 0.0)
    cum_t = cum.T
    c_run = cum_all + carry_ref[bb, 0:1, :]
    c_ref[bb] = c_run
    ct_ref[bb] = c_run.T
    carry_ref[bb, 0:1, :] = c_run[L - 1:L, :]

    expand = expand_ref[...]
    dt_x = _dot2(dt, expand)
    ecum_x = _dot2(jnp.exp(cum), expand)
    dte_x = _dot2(jnp.exp(cum[L - 1:L, :] - cum), expand)
    xdt = xs * dt_x
    xdt_b = xdt.astype(BF16)
    w_b = (xdt * dte_x).astype(BF16)

    y_parts = []
    for g in range(SSM_GROUPS):
        bg = b_all[:, g * D_STATE:(g + 1) * D_STATE]
        cg = c_all[:, g * D_STATE:(g + 1) * D_STATE]
        cb = lax.dot_general(cg, bg, (((1,), (1,)), ((), ())), preferred_element_type=F32)
        s_old = state_ref[bb, g]
        y_off = _dot(cg, s_old.astype(BF16)) * ecum_x[:, g * gw:(g + 1) * gw]
        upd = lax.dot_general(bg, w_b[:, g * gw:(g + 1) * gw], (((0,), (0,)), ((), ())),
                              preferred_element_type=F32)
        state_ref[bb, g] = s_old * ecum_x[L - 1:L, g * gw:(g + 1) * gw] + upd
        pairs_per_group = n_pairs // SSM_GROUPS
        for pp in range(pairs_per_group):
            p = g * pairs_per_group + pp
            xp = xdt_b[:, p * LANES:(p + 1) * LANES]
            ys = []
            for hh in range(2):
                h = 2 * p + hh
                seg = cum[:, h:h + 1] - cum_t[h:h + 1, :]
                decay = jnp.exp(jnp.where(causal, seg, NEG_BIG))
                ys.append(_dot((cb * decay).astype(BF16), xp))
            first = lax.broadcasted_iota(jnp.int32, (L, LANES), 1) < HEAD_DIM
            y_parts.append(jnp.where(first, ys[0], ys[1])
                           + y_off[:, pp * LANES:(pp + 1) * LANES])
    y = jnp.concatenate(y_parts, axis=1) + xs * dskip_ref[...]
    gated = y * _silu(z_ref[bb].astype(F32))
    ms = jnp.mean(gated * gated, axis=-1, keepdims=True)
    y_ref[bb] = (gated * lax.rsqrt(ms + EPS) * nw_ref[...]).astype(BF16)


def _ssd(proj, z_off, xbc_off, cdim, d_ssm, small, convw, convb, bias_row, a_row, dskip_row, nw,
         expand, batch, seq):
    t = proj.shape[0]
    L = SSD_CHUNK
    nc = seq // L
    wx = 512
    assert cdim % wx == 0 and xbc_off % wx == 0 and z_off % d_ssm == 0
    n_xbc = cdim // wx
    n_seq = max(n for n in (4, 2, 1) if batch % n == 0)
    proj3 = proj.reshape(batch, seq, proj.shape[1])
    small3 = small.reshape(batch, seq, LANES)
    rowmap = lambda b, c: (b, c, 0)
    const = lambda b, c: (0, 0)
    n_delay = CONV_K - 1
    r = jnp.arange(n_delay * L)
    target = L + r % L - (n_delay - r // L)
    shift = (jnp.arange(2 * L)[None, :] == target[:, None]).astype(BF16)
    xbc_specs = [pl.BlockSpec((n_seq, L, wx), lambda b, c, o=xbc_off // wx + i: (b, c, o))
                 for i in range(n_xbc)]
    y, c_out, ct = pl.pallas_call(
        functools.partial(_ssd_kernel, d_ssm, n_xbc, n_seq),
        grid=(batch // n_seq, nc),
        in_specs=xbc_specs + [
            pl.BlockSpec((n_seq, L, d_ssm), lambda b, c: (b, c, z_off // d_ssm)),
            pl.BlockSpec((n_seq, L, LANES), rowmap),
            pl.BlockSpec((CONV_K, cdim), const),
            pl.BlockSpec((1, cdim), const),
            pl.BlockSpec((1, LANES), const),
            pl.BlockSpec((1, LANES), const),
            pl.BlockSpec((1, d_ssm), const),
            pl.BlockSpec((1, d_ssm), const),
            pl.BlockSpec((LANES, d_ssm), const),
            pl.BlockSpec((n_delay * L, 2 * L), const),
        ],
        out_specs=[
            pl.BlockSpec((n_seq, L, d_ssm), rowmap),
            pl.BlockSpec((n_seq, L, LANES), rowmap),
            pl.BlockSpec((n_seq, LANES, L), lambda b, c: (b, 0, c)),
        ],
        out_shape=[
            jax.ShapeDtypeStruct((batch, seq, d_ssm), BF16),
            jax.ShapeDtypeStruct((batch, seq, LANES), F32),
            jax.ShapeDtypeStruct((batch, LANES, seq), F32),
        ],
        scratch_shapes=[
            pltpu.VMEM((n_seq, 2 * L, cdim), BF16),
            pltpu.VMEM((n_seq, SSM_GROUPS, D_STATE, d_ssm // SSM_GROUPS), F32),
            pltpu.VMEM((n_seq, 8, LANES), F32),
        ],
        compiler_params=_cparams(("parallel", "arbitrary")),
        name="ssd",
    )(*([proj3] * n_xbc), proj3, small3, convw, convb, bias_row, a_row, dskip_row, nw, expand, shift)
    return y.reshape(t, d_ssm), c_out.reshape(t, LANES), ct


def _headnorm(x, w_row, same_head):
    ms = _dot((x * x).astype(BF16), same_head) * (1.0 / HEAD_DIM)
    return x * lax.rsqrt(ms + EPS) * w_row


def _place(parts, moves, rr, cc):
    out = None
    for i, part in enumerate(parts):
        sel = None
        for src, dst in moves:
            one = jnp.where((rr == src) & (cc == dst + i), 1.0, 0.0)
            sel = one if sel is None else sel + one
        term = _dot(part, sel.astype(BF16))
        out = term if out is None else out + term
    return out


def _attn_kernel(tq, tk, npair, q_ref, k_ref, v_ref, c_ref, ct_ref, qw_ref, kw_ref, o_ref,
                 kx_ref, vx_ref, qx_ref, sa_ref, sb_ref, ta_ref, tb_ref, m_ref, l_ref, acc_ref):
    grp = pl.program_id(1)
    qi = pl.program_id(2)
    seq = k_ref.shape[1]
    lane = lax.broadcasted_iota(jnp.int32, (1, LANES), 1)
    first = lane < HEAD_DIM
    rr = lax.broadcasted_iota(jnp.int32, (LANES, LANES), 0)
    cc = lax.broadcasted_iota(jnp.int32, (LANES, LANES), 1)
    same_head = jnp.where((rr < HEAD_DIM) == (cc < HEAD_DIM), 1.0, 0.0).astype(BF16)
    pairs = range(npair)
    bases = [16 + 2 * (grp * npair + p) for p in pairs]

    @pl.when(qi == 0)
    def _():
        k_const = jnp.where((lane >= 6) & (lane < 9), 1.0, 0.0)

        def body(r, carry):
            rows = pl.ds(pl.multiple_of(r * tq, tq), tq)
            parts = _split3(c_ref[0, rows, :] * LOG2E)
            for p in pairs:
                lanes = slice(p * LANES, (p + 1) * LANES)
                kn = _headnorm(k_ref[0, rows, lanes].astype(F32), kw_ref[...], same_head)
                kext = _place(parts, [(bases[p], 0), (bases[p] + 1, 3)], rr, cc) + k_const
                kx_ref[p, rows, 0:LANES] = kn.astype(BF16)
                kx_ref[p, rows, LANES:2 * LANES] = kext.astype(BF16)
                v = v_ref[0, rows, lanes].astype(F32)
                vt = v.T.astype(BF16)
                vx_ref[p, 0, :, rows] = vt[0:HEAD_DIM]
                vx_ref[p, 1, :, rows] = vt[HEAD_DIM:LANES]
            return carry
        lax.fori_loop(0, seq // tq, body, 0)

    scale = (1.0 / np.sqrt(HEAD_DIM)) * LOG2E
    sub = lax.broadcasted_iota(jnp.int32, (LANES, tq), 0)
    for p in pairs:
        qn = _headnorm(q_ref[0, :, p * LANES:(p + 1) * LANES].astype(F32), qw_ref[...],
                       same_head) * scale
        qt = qn.T
        for hh in range(2):
            c_hi, c_mid, c_lo = _split3(ct_ref[0, pl.ds(bases[p] + hh, 1), :] * LOG2E)
            minus = (sub >= 3 * hh) & (sub < 3 * hh + 3)
            ext = jnp.where(sub == 6, c_hi.astype(F32), jnp.where(sub == 7, c_mid.astype(F32),
                  jnp.where(sub == 8, c_lo.astype(F32), jnp.where(minus, -1.0, 0.0))))
            own = (sub < HEAD_DIM) if hh == 0 else (sub >= HEAD_DIM)
            qx_ref[p, 0:LANES, hh * tq:(hh + 1) * tq] = jnp.where(own, qt, 0.0).astype(BF16)
            qx_ref[p, LANES:2 * LANES, hh * tq:(hh + 1) * tq] = ext.astype(BF16)
        m_ref[p] = jnp.full((1, 2 * tq), NEG_BIG, F32)
        l_ref[p] = jnp.zeros((1, 2 * tq), F32)
        acc_ref[p] = jnp.zeros((2, HEAD_DIM, tq), F32)

    n_full = (qi * tq) // tk
    q_first = qi * tq - n_full * tk
    key_i = lax.broadcasted_iota(jnp.int32, (tk, 2 * tq), 0)
    qry_i = lax.broadcasted_iota(jnp.int32, (tk, 2 * tq), 1)
    keep = jnp.where(qry_i >= tq, qry_i - tq, qry_i) + q_first >= key_i

    def scores(j, buf):
        s_ref, top_ref = buf
        rows = pl.ds(pl.multiple_of(j * tk, tk), tk)
        for p in pairs:
            s = _dot(kx_ref[p, rows, :], qx_ref[p])
            s_ref[p] = s
            top_ref[p] = jnp.max(s, axis=0, keepdims=True)

    def softmax_pv(j, buf, masked):
        s_ref, top_ref = buf
        cols = pl.ds(pl.multiple_of(j * tk, tk), tk)
        for p in pairs:
            s = s_ref[p]
            if masked:
                s = jnp.where(keep, s, NEG_BIG)
                top = jnp.max(s, axis=0, keepdims=True)
            else:
                top = top_ref[p]
            m = m_ref[p]
            m_new = jnp.maximum(m, top)
            alpha = jnp.exp2(m - m_new)
            pw = jnp.exp2(s - m_new)
            l_ref[p] = alpha * l_ref[p] + jnp.sum(pw, axis=0, keepdims=True)
            pb = pw.astype(BF16)
            acc_ref[p, 0] = alpha[:, :tq] * acc_ref[p, 0] + _dot(vx_ref[p, 0, :, cols], pb[:, :tq])
            acc_ref[p, 1] = alpha[:, tq:] * acc_ref[p, 1] + _dot(vx_ref[p, 1, :, cols], pb[:, tq:])
            m_ref[p] = m_new

    buf_a = (sa_ref, ta_ref)
    buf_b = (sb_ref, tb_ref)
    scores(0, buf_a)

    def two_blocks(i, carry):
        j = 2 * i
        scores(j + 1, buf_b)
        softmax_pv(j, buf_a, False)
        scores(j + 2, buf_a)
        softmax_pv(j + 1, buf_b, False)
        return carry
    lax.fori_loop(0, n_full // 2, two_blocks, 0)

    @pl.when(n_full % 2 == 0)
    def _():
        softmax_pv(n_full, buf_a, True)

    @pl.when(n_full % 2 == 1)
    def _():
        scores(n_full, buf_b)
        softmax_pv(n_full - 1, buf_a, False)
        softmax_pv(n_full, buf_b, True)

    for p in pairs:
        out_t = jnp.concatenate(
            [acc_ref[p, 0] / l_ref[p, :, 0:tq], acc_ref[p, 1] / l_ref[p, :, tq:2 * tq]], axis=0)
        o_ref[0, :, p * LANES:(p + 1) * LANES] = out_t.T.astype(BF16)


def _attn(proj, q_off, k_off, v_off, d_att, c, ct, qw_row, kw_row, batch, seq, tq, tk, npair):
    width = npair * LANES
    assert d_att % width == 0 and seq % tq == 0 and seq % tk == 0 and tk % tq == 0
    assert q_off % width == 0 and k_off % width == 0 and v_off % width == 0
    qo, ko, vo = q_off // width, k_off // width, v_off // width
    q3 = k3 = v3 = proj.reshape(batch, seq, proj.shape[-1])
    c3 = c.reshape(batch, seq, LANES)
    return pl.pallas_call(
        functools.partial(_attn_kernel, tq, tk, npair),
        grid=(batch, d_att // width, seq // tq),
        in_specs=[
            pl.BlockSpec((1, tq, width), lambda b, h, i: (b, i, qo + h)),
            pl.BlockSpec((1, seq, width), lambda b, h, i: (b, 0, ko + h)),
            pl.BlockSpec((1, seq, width), lambda b, h, i: (b, 0, vo + h)),
            pl.BlockSpec((1, seq, LANES), lambda b, h, i: (b, 0, 0)),
            pl.BlockSpec((1, LANES, tq), lambda b, h, i: (b, 0, i)),
            pl.BlockSpec((1, LANES), lambda b, h, i: (0, 0)),
            pl.BlockSpec((1, LANES), lambda b, h, i: (0, 0)),
        ],
        out_specs=pl.BlockSpec((1, tq, width), lambda b, h, i: (b, i, h)),
        out_shape=jax.ShapeDtypeStruct((batch, seq, d_att), BF16),
        scratch_shapes=[
            pltpu.VMEM((npair, seq, 2 * LANES), BF16),
            pltpu.VMEM((npair, 2, HEAD_DIM, seq), BF16),
            pltpu.VMEM((npair, 2 * LANES, 2 * tq), BF16),
            pltpu.VMEM((npair, tk, 2 * tq), F32),
            pltpu.VMEM((npair, tk, 2 * tq), F32),
            pltpu.VMEM((npair, 1, 2 * tq), F32),
            pltpu.VMEM((npair, 1, 2 * tq), F32),
            pltpu.VMEM((npair, 1, 2 * tq), F32),
            pltpu.VMEM((npair, 1, 2 * tq), F32),
            pltpu.VMEM((npair, 2, HEAD_DIM, tq), F32),
        ],
        compiler_params=_cparams(("parallel", "parallel", "arbitrary")),
        name="attn",
    )(q3, k3, v3, c3, ct, qw_row, kw_row).reshape(batch * seq, d_att)


def _outproj_kernel(ys_ref, ya_ref, x_ref, wos_ref, woa_ref, aw_ref, fw_ref, wr_ref, br_ref,
                    x1_ref, xn_ref, gates_ref):
    ya = ya_ref[...].astype(F32)
    ya = ya * lax.rsqrt(jnp.mean(ya * ya, axis=-1, keepdims=True) + EPS) * aw_ref[...]
    x1 = x_ref[...] + _dot(ys_ref[...], wos_ref[...]) + _dot(ya.astype(BF16), woa_ref[...])
    x1_ref[...] = x1
    xn = x1 * lax.rsqrt(jnp.mean(x1 * x1, axis=-1, keepdims=True) + EPS) * fw_ref[...]
    xn_ref[...] = xn.astype(BF16)

    x_hi = xn.astype(BF16)
    x_lo = (xn - x_hi.astype(F32)).astype(BF16)
    w = wr_ref[...]
    w_hi = w.astype(BF16)
    w_lo = (w - w_hi.astype(F32)).astype(BF16)
    hi_terms = _dot(x_hi, jnp.concatenate([w_hi, w_lo], axis=1))
    logits = hi_terms[:, :LANES] + _dot(x_lo, w_hi) + hi_terms[:, LANES:] + br_ref[...]
    lane = lax.broadcasted_iota(jnp.int32, logits.shape, 1).astype(F32)

    def rmax(mask):
        return jnp.max(jnp.where(mask, logits, NEG_BIG), axis=-1, keepdims=True)

    def first_at(mask, val):
        return jnp.min(jnp.where(mask & (logits == val), lane, 1e9), axis=-1, keepdims=True)

    gmask = (lane >= N_EXPERTS) & (lane < N_EXPERTS + N_EXPERT_GROUPS)
    gmax = rmax(gmask)
    p_g = 1.0 / jnp.sum(jnp.where(gmask, jnp.exp(logits - gmax), 0.0), axis=-1, keepdims=True)
    g_star = first_at(gmask, gmax) - N_EXPERTS
    lo = g_star * EXPERTS_PER_GROUP
    emask = (lane >= lo) & (lane < lo + EXPERTS_PER_GROUP)
    v1 = rmax(emask)
    i1 = first_at(emask, v1)
    mask2 = emask & (lane != i1)
    v2 = rmax(mask2)
    i2 = first_at(mask2, v2)
    e2 = jnp.exp(v2 - v1)
    w1 = p_g / (1.0 + e2)
    w2 = p_g * e2 / (1.0 + e2)
    gates_ref[...] = (jnp.where(lane == i1, w1, 0.0) + jnp.where(lane == i2, w2, 0.0)
                      + jnp.where(lane == N_EXPERTS, g_star, 0.0))


def _outproj(ys, ya, x, wos, woa, aw, fw, wr, br, tm):
    t, d = x.shape
    dm = ys.shape[1]
    rows = lambda i: (i, 0)
    const = lambda i: (0, 0)
    return pl.pallas_call(
        _outproj_kernel,
        grid=(t // tm,),
        in_specs=[
            pl.BlockSpec((tm, dm), rows),
            pl.BlockSpec((tm, dm), rows),
            pl.BlockSpec((tm, d), rows),
            pl.BlockSpec((dm, d), const),
            pl.BlockSpec((dm, d), const),
            pl.BlockSpec((1, dm), const),
            pl.BlockSpec((1, d), const),
            pl.BlockSpec((d, LANES), const),
            pl.BlockSpec((1, LANES), const),
        ],
        out_specs=[
            pl.BlockSpec((tm, d), rows),
            pl.BlockSpec((tm, d), rows),
            pl.BlockSpec((tm, LANES), rows),
        ],
        out_shape=[
            jax.ShapeDtypeStruct((t, d), F32),
            jax.ShapeDtypeStruct((t, d), BF16),
            jax.ShapeDtypeStruct((t, LANES), F32),
        ],
        compiler_params=_cparams(("parallel",)),
        name="outproj",
    )(ys, ya, x, wos, woa, aw, fw, wr, br)


MOE_BLOCK = 1024
MOE_CHUNK = 128
MOE_ROWS = MOE_BLOCK + N_EXPERT_GROUPS * MOE_CHUNK
GROUP_LANE = N_EXPERTS


def _group_runs(count):
    nch = jnp.floor((count + (MOE_CHUNK - 1)) * (1.0 / MOE_CHUNK))
    return nch, nch * MOE_CHUNK


def _moe_dispatch_kernel(xn_ref, route_ref, before_ref, xs_ref, gs_ref, meta_ref):
    pt = xn_ref.shape[0]
    route = route_ref[...]
    rt = route.T
    g_row = rt[GROUP_LANE:GROUP_LANE + 1, :]
    grp = lax.broadcasted_iota(jnp.int32, (8, pt), 0).astype(F32)
    onehot = jnp.where(grp == g_row, 1.0, 0.0)
    rank = _dot(onehot.astype(BF16), before_ref[...])
    nch, padded = _group_runs(jnp.sum(onehot, axis=1, keepdims=True))
    gi = lax.broadcasted_iota(jnp.int32, (8, 1), 0)
    start = jnp.zeros((8, 1), F32)
    for g in range(N_EXPERT_GROUPS - 1):
        start = start + jnp.where(gi > g, padded[g:g + 1, :], 0.0)
    pos = jnp.sum(onehot * (start + rank), axis=0, keepdims=True)
    dest = lax.broadcasted_iota(jnp.int32, (MOE_ROWS, pt), 0).astype(F32)
    perm = jnp.where(dest == pos, 1.0, 0.0).astype(BF16)
    g_hi = route.astype(BF16)
    g_lo = (route - g_hi.astype(F32)).astype(BF16)
    moved = _dot(perm, jnp.concatenate([xn_ref[...], g_hi, g_lo], axis=1))
    d = xn_ref.shape[1]
    xs_ref[0] = moved[:, :d].astype(BF16)
    gs_ref[0] = moved[:, d:d + LANES] + moved[:, d + LANES:]
    lane = lax.broadcasted_iota(jnp.int32, (8, LANES), 1)
    meta_ref[0] = jnp.where(lane == 0, start, jnp.where(lane == 1, nch, 0.0))


def _moe_expert_kernel(n_sub, n_exp, meta_ref, xs_ref, gs_ref, wg_ref, wu_ref, wd_ref, ys_ref):
    st = pl.program_id(0)
    eb = pl.program_id(1)
    g = (eb * n_exp) // EXPERTS_PER_GROUP

    @pl.when(eb == 0)
    def _():
        ys_ref[...] = jnp.zeros(ys_ref.shape, F32)

    def run(s, first_row, n_rows):
        rows = pl.ds(pl.multiple_of(first_row, MOE_CHUNK), n_rows)
        lane = lax.broadcasted_iota(jnp.int32, (n_rows, LANES), 1)
        xc = xs_ref[s, rows, :]
        gates = gs_ref[s, rows, :]
        y = ys_ref[s, rows, :]
        for k in range(n_exp):
            gate = jnp.sum(jnp.where(lane == eb * n_exp + k, gates, 0.0), axis=-1, keepdims=True)
            hid = _silu(_dot(xc, wg_ref[k])) * _dot(xc, wu_ref[k]) * gate
            y = y + _dot(hid.astype(BF16), wd_ref[k])
        ys_ref[s, rows, :] = y

    for s in range(n_sub):
        base = ((st * n_sub + s) * N_EXPERT_GROUPS + g) * 2
        start = meta_ref[base]
        n_chunks = meta_ref[base + 1]

        def two_chunks(c, carry, s=s, start=start):
            run(s, start + c * (2 * MOE_CHUNK), 2 * MOE_CHUNK)
            return carry
        lax.fori_loop(0, n_chunks // 2, two_chunks, 0)

        @pl.when(n_chunks % 2 == 1)
        def _(s=s, start=start, n_chunks=n_chunks):
            run(s, start + (n_chunks - 1) * MOE_CHUNK, MOE_CHUNK)


def _moe_combine_kernel(ys_ref, route_ref, x1_ref, before_ref, o_ref):
    pt = x1_ref.shape[0]
    route = route_ref[...]
    lane = lax.broadcasted_iota(jnp.int32, (pt, LANES), 1).astype(F32)
    onehot = jnp.where(lane == route[:, GROUP_LANE:GROUP_LANE + 1], 1.0, 0.0)
    rank = _dot(before_ref[...], onehot.astype(BF16))
    _, padded = _group_runs(jnp.sum(onehot, axis=0, keepdims=True))
    lane1 = lax.broadcasted_iota(jnp.int32, (1, LANES), 1)
    start = jnp.zeros((1, LANES), F32)
    for g in range(N_EXPERT_GROUPS - 1):
        start = start + jnp.where(lane1 > g, padded[:, g:g + 1], 0.0)
    pos = jnp.sum(onehot * (start + rank), axis=-1, keepdims=True)
    src = lax.broadcasted_iota(jnp.int32, (pt, MOE_ROWS), 1).astype(F32)
    perm_t = jnp.where(src == pos, 1.0, 0.0).astype(BF16)
    o_ref[...] = x1_ref[...] + _dot(perm_t, ys_ref[0].astype(BF16))


def _moe(xn, route, x1, wg, wu, wd, n_sub):
    t, d = x1.shape
    n_e, _, f = wg.shape
    pt = MOE_BLOCK
    assert t % (pt * n_sub) == 0
    nb = t // pt
    idx = jnp.arange(pt)
    before_row = (idx[:, None] < idx[None, :]).astype(BF16)
    before_col = (idx[None, :] < idx[:, None]).astype(BF16)
    rows = lambda i: (i, 0)
    const = lambda i: (0, 0)
    xs, gs, meta = pl.pallas_call(
        _moe_dispatch_kernel,
        grid=(nb,),
        in_specs=[
            pl.BlockSpec((pt, d), rows),
            pl.BlockSpec((pt, LANES), rows),
            pl.BlockSpec((pt, pt), const),
        ],
        out_specs=[
            pl.BlockSpec((1, MOE_ROWS, d), lambda i: (i, 0, 0)),
            pl.BlockSpec((1, MOE_ROWS, LANES), lambda i: (i, 0, 0)),
            pl.BlockSpec((1, 8, LANES), lambda i: (i, 0, 0)),
        ],
        out_shape=[
            jax.ShapeDtypeStruct((nb, MOE_ROWS, d), BF16),
            jax.ShapeDtypeStruct((nb, MOE_ROWS, LANES), F32),
            jax.ShapeDtypeStruct((nb, 8, LANES), F32),
        ],
        compiler_params=_cparams(("parallel",)),
        name="moe_dispatch",
    )(xn, route, before_row)

    meta_i = meta[:, :N_EXPERT_GROUPS, :2].astype(jnp.int32).reshape(-1)
    n_exp = 4
    assert EXPERTS_PER_GROUP % n_exp == 0
    ys = pl.pallas_call(
        functools.partial(_moe_expert_kernel, n_sub, n_exp),
        grid_spec=pltpu.PrefetchScalarGridSpec(
            num_scalar_prefetch=1,
            grid=(nb // n_sub, n_e // n_exp),
            in_specs=[
                pl.BlockSpec((n_sub, MOE_ROWS, d), lambda i, e, m: (i, 0, 0)),
                pl.BlockSpec((n_sub, MOE_ROWS, LANES), lambda i, e, m: (i, 0, 0)),
                pl.BlockSpec((n_exp, d, f), lambda i, e, m: (e, 0, 0)),
                pl.BlockSpec((n_exp, d, f), lambda i, e, m: (e, 0, 0)),
                pl.BlockSpec((n_exp, f, d), lambda i, e, m: (e, 0, 0)),
            ],
            out_specs=pl.BlockSpec((n_sub, MOE_ROWS, d), lambda i, e, m: (i, 0, 0)),
        ),
        out_shape=jax.ShapeDtypeStruct((nb, MOE_ROWS, d), F32),
        compiler_params=_cparams(("parallel", "arbitrary")),
        name="moe_experts",
    )(meta_i, xs, gs, wg, wu, wd)

    return pl.pallas_call(
        _moe_combine_kernel,
        grid=(nb,),
        in_specs=[
            pl.BlockSpec((1, MOE_ROWS, d), lambda i: (i, 0, 0)),
            pl.BlockSpec((pt, LANES), rows),
            pl.BlockSpec((pt, d), rows),
            pl.BlockSpec((pt, pt), const),
        ],
        out_specs=pl.BlockSpec((pt, d), rows),
        out_shape=jax.ShapeDtypeStruct((t, d), F32),
        compiler_params=_cparams(("parallel",)),
        name="moe_combine",
    )(ys, route, x1, before_col)


def _pad_lanes(v, offset=0):
    out = jnp.zeros((1, LANES), F32)
    return out.at[0, offset:offset + v.shape[0]].set(v.astype(F32))


def kernel(x, norm_mix_w, w_in, conv_w, conv_b, dt_bias, a_log, d_skip, ssm_norm_w, q_norm_w,
           k_norm_w, fox_f_bias, att_out_norm_w, w_out, norm_ffn_w, w_router_group,
           b_router_group, w_router_expert, b_router_expert, w_expert_gate, w_expert_up,
           w_expert_down):
    batch, seq, d = x.shape
    depth = w_in.shape[0]
    d_ssm = ssm_norm_w.shape[1]
    d_att = att_out_norm_w.shape[1]
    h_ssm = dt_bias.shape[1]
    h_att = fox_f_bias.shape[1]
    conv_dim = conv_w.shape[2]
    assert h_ssm == 16 and h_att == 16 and d_ssm // h_ssm == HEAD_DIM and d_att // h_att == HEAD_DIM
    assert conv_dim == d_ssm + 2 * SSM_GROUPS * D_STATE
    off_xbc = d_ssm
    off_dt = off_xbc + conv_dim
    off_q = off_dt + h_ssm
    off_k = off_q + d_att
    off_v = off_k + d_att
    off_f = off_v + d_att
    t = batch * seq
    tm_in = min(512, t)
    tm_out = min(512, t)
    moe_sub = 2 if t % (2 * MOE_BLOCK) == 0 else 1
    tq = min(512, seq)
    tk = tq

    head_of_chan = jnp.arange(d_ssm) // HEAD_DIM
    expand = (jnp.arange(LANES)[:, None] == head_of_chan[None, :]).astype(BF16)

    xf = x.reshape(t, d)
    for l in range(depth):
        wl = w_in[l]
        w_main = jnp.concatenate(
            [wl[:, :off_xbc], wl[:, off_q:off_k], wl[:, off_k:off_v], wl[:, off_v:off_f],
             wl[:, off_xbc:off_dt]], axis=1).astype(BF16)
        p_q, p_k, p_v, p_xbc = d_ssm, d_ssm + d_att, d_ssm + 2 * d_att, d_ssm + 3 * d_att
        w_small = jnp.zeros((d, LANES), F32)
        w_small = w_small.at[:, 0:h_ssm].set(wl[:, off_dt:off_q])
        w_small = w_small.at[:, 16:16 + h_att].set(wl[:, off_f:off_f + h_att]).astype(BF16)
        proj, small = _inproj(xf, norm_mix_w[l][None, :], w_main, w_small, tm_in, 512)

        bias_row = _pad_lanes(dt_bias[l]) + _pad_lanes(fox_f_bias[l], 16)
        a_row = _pad_lanes(-jnp.exp(a_log[l].astype(F32)))
        dskip_row = jnp.repeat(d_skip[l].astype(F32), HEAD_DIM)[None, :]
        y_ssm, c, ct = _ssd(proj, 0, p_xbc, conv_dim, d_ssm, small, conv_w[l].astype(F32),
                        conv_b[l].astype(F32)[None, :], bias_row, a_row, dskip_row,
                        ssm_norm_w[l][None, :], expand, batch, seq)

        qw_row = jnp.tile(q_norm_w[l].astype(F32), 2)[None, :]
        kw_row = jnp.tile(k_norm_w[l].astype(F32), 2)[None, :]
        y_att = _attn(proj, p_q, p_k, p_v, d_att, c, ct, qw_row, kw_row, batch, seq, tq, tk, 2)

        wr = jnp.zeros((d, LANES), F32)
        wr = wr.at[:, :N_EXPERTS].set(w_router_expert[l])
        wr = wr.at[:, N_EXPERTS:N_EXPERTS + N_EXPERT_GROUPS].set(w_router_group[l])
        br = _pad_lanes(b_router_expert[l]) + _pad_lanes(b_router_group[l], N_EXPERTS)
        wo = w_out[l].astype(BF16)
        x1, xn, gates = _outproj(y_ssm, y_att, xf, wo[:d_ssm], wo[d_ssm:],
                                 att_out_norm_w[l][None, :], norm_ffn_w[l][None, :], wr, br, tm_out)

        xf = _moe(xn, gates, x1, w_expert_gate[l].astype(BF16), w_expert_up[l].astype(BF16),
                  w_expert_down[l].astype(BF16), moe_sub)
    return xf.reshape(batch, seq, d)
```

```python
import functools

import jax
import jax.numpy as jnp
import numpy as np
from jax import lax
from jax.experimental import pallas as pl
from jax.experimental.pallas import tpu as pltpu

F32 = jnp.float32
BF16 = jnp.bfloat16

HEAD_DIM = 64
D_STATE = 128
SSM_GROUPS = 2
CONV_K = 4
SSD_CHUNK = 128
N_EXPERT_GROUPS = 4
EXPERTS_PER_GROUP = 8
N_EXPERTS = N_EXPERT_GROUPS * EXPERTS_PER_GROUP
EPS = 1e-6
LANES = 128
NEG_BIG = -1e30
LOG2E = 1.4426950408889634
VMEM_LIMIT = 56 * 1024 * 1024


def _cparams(sem):
    return pltpu.CompilerParams(dimension_semantics=sem, vmem_limit_bytes=VMEM_LIMIT)


def _split3(a):
    a1 = a.astype(BF16)
    r1 = a - a1.astype(F32)
    a2 = r1.astype(BF16)
    a3 = (r1 - a2.astype(F32)).astype(BF16)
    return a1, a2, a3


def _dot(a, b):
    return jnp.dot(a, b, preferred_element_type=F32)


def _dot3(a_f32, b_bf16):
    a1, a2, a3 = _split3(a_f32)
    return _dot(a1, b_bf16) + _dot(a2, b_bf16) + _dot(a3, b_bf16)


def _dot2(a_f32, b_bf16):
    a1 = a_f32.astype(BF16)
    a2 = (a_f32 - a1.astype(F32)).astype(BF16)
    return _dot(a1, b_bf16) + _dot(a2, b_bf16)


def _dot3_tn(b_bf16_t, a_f32):
    a1, a2, a3 = _split3(a_f32)
    return _dot(b_bf16_t, a1) + _dot(b_bf16_t, a2) + _dot(b_bf16_t, a3)


def _silu(x):
    return x / (1.0 + jnp.exp(-x))


def _inproj_kernel(tn, x_ref, nw_ref, w_ref, ws_ref, proj_ref, small_ref):
    x = x_ref[...]
    ms = jnp.mean(x * x, axis=-1, keepdims=True)
    hb = (x * lax.rsqrt(ms + EPS) * nw_ref[...]).astype(BF16)
    small_ref[...] = _dot(hb, ws_ref[...])
    for j in range(w_ref.shape[1] // tn):
        cols = slice(j * tn, (j + 1) * tn)
        proj_ref[:, cols] = _dot(hb, w_ref[:, cols]).astype(BF16)


def _inproj(x2, nw, w_main, w_small, tm, tn):
    t, d = x2.shape
    n = w_main.shape[1]
    assert n % tn == 0 and t % tm == 0
    return pl.pallas_call(
        functools.partial(_inproj_kernel, tn),
        grid=(t // tm,),
        in_specs=[
            pl.BlockSpec((tm, d), lambda i: (i, 0)),
            pl.BlockSpec((1, d), lambda i: (0, 0)),
            pl.BlockSpec((d, n), lambda i: (0, 0)),
            pl.BlockSpec((d, LANES), lambda i: (0, 0)),
        ],
        out_specs=[
            pl.BlockSpec((tm, n), lambda i: (i, 0)),
            pl.BlockSpec((tm, LANES), lambda i: (i, 0)),
        ],
        out_shape=[
            jax.ShapeDtypeStruct((t, n), BF16),
            jax.ShapeDtypeStruct((t, LANES), F32),
        ],
        compiler_params=_cparams(("parallel",)),
        name="inproj",
    )(x2, nw, w_main, w_small)


def _ssd_kernel(d_ssm, n_xbc, n_seq, *refs):
    ext_ref, state_ref, carry_ref = refs[-3:]
    L = SSD_CHUNK

    @pl.when(pl.program_id(1) == 0)
    def _():
        ext_ref[:, L:2 * L, :] = jnp.zeros((n_seq, L, ext_ref.shape[2]), BF16)
        state_ref[...] = jnp.zeros(state_ref.shape, F32)
        carry_ref[...] = jnp.zeros(carry_ref.shape, F32)

    for bb in range(n_seq):
        _ssd_chunk(bb, d_ssm, n_xbc, refs)


def _ssd_chunk(bb, d_ssm, n_xbc, refs):
    xbc_refs = refs[:n_xbc]
    (z_ref, small_ref, convw_ref, convb_ref, bias_ref, arow_ref, dskip_ref, nw_ref, expand_ref,
     shift_ref, y_ref, c_ref, ct_ref, ext_ref, state_ref, carry_ref) = refs[n_xbc:]
    L = SSD_CHUNK
    n_pairs = d_ssm // LANES
    gw = d_ssm // SSM_GROUPS

    u = jnp.concatenate([r[bb] for r in xbc_refs], axis=1)
    ext_ref[bb, 0:L, :] = ext_ref[bb, L:2 * L, :]
    ext_ref[bb, L:2 * L, :] = u
    n_delay = CONV_K - 1
    delayed = _dot(shift_ref[...], ext_ref[bb])
    acc = convb_ref[...] + convw_ref[n_delay:CONV_K, :] * u.astype(F32)
    for k in range(n_delay):
        acc = acc + convw_ref[k:k + 1, :] * delayed[k * L:(k + 1) * L, :]
    xbc = _silu(acc)
    xs = xbc[:, :d_ssm]
    b_all = xbc[:, d_ssm:d_ssm + SSM_GROUPS * D_STATE].astype(BF16)
    c_all = xbc[:, d_ssm + SSM_GROUPS * D_STATE:].astype(BF16)

    row = lax.broadcasted_iota(jnp.int32, (L, L), 0)
    col = lax.broadcasted_iota(jnp.int32, (L, L), 1)
    causal = row >= col
    tril = jnp.where(causal, 1.0, 0.0).astype(BF16)
    pre = small_ref[bb] + bias_ref[...]
    sp = jnp.maximum(pre, 0.0) + jnp.log(1.0 + jnp.exp(-jnp.abs(pre)))
    lane = lax.broadcasted_iota(jnp.int32, (L, LANES), 1)
    is_dt = lane < 16
    dt = jnp.where(is_dt, sp, 0.0)
    logf = jnp.where((lane >= 16) & (lane < 32), pre - sp, 0.0)
    a = dt * arow_ref[...]
    a_and_f = a + logf
    cum_all = _dot3_tn(tril, a_and_f)
    cum = jnp.where(is_dt, cum_all, 0.0)
    cum_t = cum.T
    c_run = cum_all + carry_ref[bb, 0:1, :]
    c_ref[bb] = c_run
    ct_ref[bb] = c_run.T
    carry_ref[bb, 0:1, :] = c_run[L - 1:L, :]

    expand = expand_ref[...]
    dt_x = _dot2(dt, expand)
    ecum_x = _dot2(jnp.exp(cum), expand)
    dte_x = _dot2(jnp.exp(cum[L - 1:L, :] - cum), expand)
    xdt = xs * dt_x
    xdt_b = xdt.astype(BF16)
    w_b = (xdt * dte_x).astype(BF16)

    y_parts = []
    for g in range(SSM_GROUPS):
        bg = b_all[:, g * D_STATE:(g + 1) * D_STATE]
        cg = c_all[:, g * D_STATE:(g + 1) * D_STATE]
        cb = lax.dot_general(cg, bg, (((1,), (1,)), ((), ())), preferred_element_type=F32)
        s_old = state_ref[bb, g]
        y_off = _dot(cg, s_old.astype(BF16)) * ecum_x[:, g * gw:(g + 1) * gw]
        upd = lax.dot_general(bg, w_b[:, g * gw:(g + 1) * gw], (((0,), (0,)), ((), ())),
                              preferred_element_type=F32)
        state_ref[bb, g] = s_old * ecum_x[L - 1:L, g * gw:(g + 1) * gw] + upd
        pairs_per_group = n_pairs // SSM_GROUPS
        for pp in range(pairs_per_group):
            p = g * pairs_per_group + pp
            xp = xdt_b[:, p * LANES:(p + 1) * LANES]
            ys = []
            for hh in range(2):
                h = 2 * p + hh
                seg = cum[:, h:h + 1] - cum_t[h:h + 1, :]
                decay = jnp.exp(jnp.where(causal, seg, NEG_BIG))
                ys.append(_dot((cb * decay).astype(BF16), xp))
            first = lax.broadcasted_iota(jnp.int32, (L, LANES), 1) < HEAD_DIM
            y_parts.append(jnp.where(first, ys[0], ys[1])
                           + y_off[:, pp * LANES:(pp + 1) * LANES])
    y = jnp.concatenate(y_parts, axis=1) + xs * dskip_ref[...]
    gated = y * _silu(z_ref[bb].astype(F32))
    ms = jnp.mean(gated * gated, axis=-1, keepdims=True)
    y_ref[bb] = (gated * lax.rsqrt(ms + EPS) * nw_ref[...]).astype(BF16)


def _ssd(proj, z_off, xbc_off, cdim, d_ssm, small, convw, convb, bias_row, a_row, dskip_row, nw,
         expand, batch, seq):
    t = proj.shape[0]
    L = SSD_CHUNK
    nc = seq // L
    wx = 512
    assert cdim % wx == 0 and xbc_off % wx == 0 and z_off % d_ssm == 0
    n_xbc = cdim // wx
    n_seq = max(n for n in (4, 2, 1) if batch % n == 0)
    proj3 = proj.reshape(batch, seq, proj.shape[1])
    small3 = small.reshape(batch, seq, LANES)
    rowmap = lambda b, c: (b, c, 0)
    const = lambda b, c: (0, 0)
    n_delay = CONV_K - 1
    r = jnp.arange(n_delay * L)
    target = L + r % L - (n_delay - r // L)
    shift = (jnp.arange(2 * L)[None, :] == target[:, None]).astype(BF16)
    xbc_specs = [pl.BlockSpec((n_seq, L, wx), lambda b, c, o=xbc_off // wx + i: (b, c, o))
                 for i in range(n_xbc)]
    y, c_out, ct = pl.pallas_call(
        functools.partial(_ssd_kernel, d_ssm, n_xbc, n_seq),
        grid=(batch // n_seq, nc),
        in_specs=xbc_specs + [
            pl.BlockSpec((n_seq, L, d_ssm), lambda b, c: (b, c, z_off // d_ssm)),
            pl.BlockSpec((n_seq, L, LANES), rowmap),
            pl.BlockSpec((CONV_K, cdim), const),
            pl.BlockSpec((1, cdim), const),
            pl.BlockSpec((1, LANES), const),
            pl.BlockSpec((1, LANES), const),
            pl.BlockSpec((1, d_ssm), const),
            pl.BlockSpec((1, d_ssm), const),
            pl.BlockSpec((LANES, d_ssm), const),
            pl.BlockSpec((n_delay * L, 2 * L), const),
        ],
        out_specs=[
            pl.BlockSpec((n_seq, L, d_ssm), rowmap),
            pl.BlockSpec((n_seq, L, LANES), rowmap),
            pl.BlockSpec((n_seq, LANES, L), lambda b, c: (b, 0, c)),
        ],
        out_shape=[
            jax.ShapeDtypeStruct((batch, seq, d_ssm), BF16),
            jax.ShapeDtypeStruct((batch, seq, LANES), F32),
            jax.ShapeDtypeStruct((batch, LANES, seq), F32),
        ],
        scratch_shapes=[
            pltpu.VMEM((n_seq, 2 * L, cdim), BF16),
            pltpu.VMEM((n_seq, SSM_GROUPS, D_STATE, d_ssm // SSM_GROUPS), F32),
            pltpu.VMEM((n_seq, 8, LANES), F32),
        ],
        compiler_params=_cparams(("parallel", "arbitrary")),
        name="ssd",
    )(*([proj3] * n_xbc), proj3, small3, convw, convb, bias_row, a_row, dskip_row, nw, expand, shift)
    return y.reshape(t, d_ssm), c_out.reshape(t, LANES), ct


def _headnorm(x, w_row, same_head):
    ms = _dot((x * x).astype(BF16), same_head) * (1.0 / HEAD_DIM)
    return x * lax.rsqrt(ms + EPS) * w_row


def _place(parts, moves, rr, cc):
    out = None
    for i, part in enumerate(parts):
        sel = None
        for src, dst in moves:
            one = jnp.where((rr == src) & (cc == dst + i), 1.0, 0.0)
            sel = one if sel is None else sel + one
        term = _dot(part, sel.astype(BF16))
        out = term if out is None else out + term
    return out


def _attn_kernel(tq, tk, npair, q_ref, k_ref, v_ref, c_ref, ct_ref, qw_ref, kw_ref, o_ref,
                 kx_ref, vx_ref, qx_ref, sa_ref, sb_ref, ta_ref, tb_ref, m_ref, l_ref, acc_ref):
    grp = pl.program_id(1)
    qi = pl.program_id(2)
    seq = k_ref.shape[1]
    lane = lax.broadcasted_iota(jnp.int32, (1, LANES), 1)
    first = lane < HEAD_DIM
    rr = lax.broadcasted_iota(jnp.int32, (LANES, LANES), 0)
    cc = lax.broadcasted_iota(jnp.int32, (LANES, LANES), 1)
    same_head = jnp.where((rr < HEAD_DIM) == (cc < HEAD_DIM), 1.0, 0.0).astype(BF16)
    pairs = range(npair)
    bases = [16 + 2 * (grp * npair + p) for p in pairs]

    @pl.when(qi == 0)
    def _():
        k_const = jnp.where((lane >= 6) & (lane < 9), 1.0, 0.0)

        def body(r, carry):
            rows = pl.ds(pl.multiple_of(r * tq, tq), tq)
            parts = _split3(c_ref[0, rows, :] * LOG2E)
            for p in pairs:
                lanes = slice(p * LANES, (p + 1) * LANES)
                kn = _headnorm(k_ref[0, rows, lanes].astype(F32), kw_ref[...], same_head)
                kext = _place(parts, [(bases[p], 0), (bases[p] + 1, 3)], rr, cc) + k_const
                kx_ref[p, rows, 0:LANES] = kn.astype(BF16)
                kx_ref[p, rows, LANES:2 * LANES] = kext.astype(BF16)
                v = v_ref[0, rows, lanes].astype(F32)
                vt = v.T.astype(BF16)
                vx_ref[p, 0, :, rows] = vt[0:HEAD_DIM]
                vx_ref[p, 1, :, rows] = vt[HEAD_DIM:LANES]
            return carry
        lax.fori_loop(0, seq // tq, body, 0)

    scale = (1.0 / np.sqrt(HEAD_DIM)) * LOG2E
    sub = lax.broadcasted_iota(jnp.int32, (LANES, tq), 0)
    for p in pairs:
        qn = _headnorm(q_ref[0, :, p * LANES:(p + 1) * LANES].astype(F32), qw_ref[...],
                       same_head) * scale
        qt = qn.T
        for hh in range(2):
            c_hi, c_mid, c_lo = _split3(ct_ref[0, pl.ds(bases[p] + hh, 1), :] * LOG2E)
            minus = (sub >= 3 * hh) & (sub < 3 * hh + 3)
            ext = jnp.where(sub == 6, c_hi.astype(F32), jnp.where(sub == 7, c_mid.astype(F32),
                  jnp.where(sub == 8, c_lo.astype(F32), jnp.where(minus, -1.0, 0.0))))
            own = (sub < HEAD_DIM) if hh == 0 else (sub >= HEAD_DIM)
            qx_ref[p, 0:LANES, hh * tq:(hh + 1) * tq] = jnp.where(own, qt, 0.0).astype(BF16)
            qx_ref[p, LANES:2 * LANES, hh * tq:(hh + 1) * tq] = ext.astype(BF16)
        m_ref[p] = jnp.full((1, 2 * tq), NEG_BIG, F32)
        l_ref[p] = jnp.zeros((1, 2 * tq), F32)
        acc_ref[p] = jnp.zeros((2, HEAD_DIM, tq), F32)

    n_full = (qi * tq) // tk
    q_first = qi * tq - n_full * tk
    key_i = lax.broadcasted_iota(jnp.int32, (tk, 2 * tq), 0)
    qry_i = lax.broadcasted_iota(jnp.int32, (tk, 2 * tq), 1)
    keep = jnp.where(qry_i >= tq, qry_i - tq, qry_i) + q_first >= key_i

    def scores(j, buf):
        s_ref, top_ref = buf
        rows = pl.ds(pl.multiple_of(j * tk, tk), tk)
        for p in pairs:
            s = _dot(kx_ref[p, rows, :], qx_ref[p])
            s_ref[p] = s
            top_ref[p] = jnp.max(s, axis=0, keepdims=True)

    def softmax_pv(j, buf, masked):
        s_ref, top_ref = buf
        cols = pl.ds(pl.multiple_of(j * tk, tk), tk)
        for p in pairs:
            s = s_ref[p]
            if masked:
                s = jnp.where(keep, s, NEG_BIG)
                top = jnp.max(s, axis=0, keepdims=True)
            else:
                top = top_ref[p]
            m = m_ref[p]
            m_new = jnp.maximum(m, top)
            alpha = jnp.exp2(m - m_new)
            pw = jnp.exp2(s - m_new)
            l_ref[p] = alpha * l_ref[p] + jnp.sum(pw, axis=0, keepdims=True)
            pb = pw.astype(BF16)
            acc_ref[p, 0] = alpha[:, :tq] * acc_ref[p, 0] + _dot(vx_ref[p, 0, :, cols], pb[:, :tq])
            acc_ref[p, 1] = alpha[:, tq:] * acc_ref[p, 1] + _dot(vx_ref[p, 1, :, cols], pb[:, tq:])
            m_ref[p] = m_new

    buf_a = (sa_ref, ta_ref)
    buf_b = (sb_ref, tb_ref)
    scores(0, buf_a)

    def two_blocks(i, carry):
        j = 2 * i
        scores(j + 1, buf_b)
        softmax_pv(j, buf_a, False)
        scores(j + 2, buf_a)
        softmax_pv(j + 1, buf_b, False)
        return carry
    lax.fori_loop(0, n_full // 2, two_blocks, 0)

    @pl.when(n_full % 2 == 0)
    def _():
        softmax_pv(n_full, buf_a, True)

    @pl.when(n_full % 2 == 1)
    def _():
        scores(n_full, buf_b)
        softmax_pv(n_full - 1, buf_a, False)
        softmax_pv(n_full, buf_b, True)

    for p in pairs:
        out_t = jnp.concatenate(
            [acc_ref[p, 0] / l_ref[p, :, 0:tq], acc_ref[p, 1] / l_ref[p, :, tq:2 * tq]], axis=0)
        o_ref[0, :, p * LANES:(p + 1) * LANES] = out_t.T.astype(BF16)


def _attn(proj, q_off, k_off, v_off, d_att, c, ct, qw_row, kw_row, batch, seq, tq, tk, npair):
    width = npair * LANES
    assert d_att % width == 0 and seq % tq == 0 and seq % tk == 0 and tk % tq == 0
    assert q_off % width == 0 and k_off % width == 0 and v_off % width == 0
    qo, ko, vo = q_off // width, k_off // width, v_off // width
    q3 = k3 = v3 = proj.reshape(batch, seq, proj.shape[-1])
    c3 = c.reshape(batch, seq, LANES)
    return pl.pallas_call(
        functools.partial(_attn_kernel, tq, tk, npair),
        grid=(batch, d_att // width, seq // tq),
        in_specs=[
            pl.BlockSpec((1, tq, width), lambda b, h, i: (b, i, qo + h)),
            pl.BlockSpec((1, seq, width), lambda b, h, i: (b, 0, ko + h), pipeline_mode=pl.Buffered(1)),
            pl.BlockSpec((1, seq, width), lambda b, h, i: (b, 0, vo + h), pipeline_mode=pl.Buffered(1)),
            pl.BlockSpec((1, seq, LANES), lambda b, h, i: (b, 0, 0), pipeline_mode=pl.Buffered(1)),
            pl.BlockSpec((1, LANES, tq), lambda b, h, i: (b, 0, i)),
            pl.BlockSpec((1, LANES), lambda b, h, i: (0, 0)),
            pl.BlockSpec((1, LANES), lambda b, h, i: (0, 0)),
        ],
        out_specs=pl.BlockSpec((1, tq, width), lambda b, h, i: (b, i, h)),
        out_shape=jax.ShapeDtypeStruct((batch, seq, d_att), BF16),
        scratch_shapes=[
            pltpu.VMEM((npair, seq, 2 * LANES), BF16),
            pltpu.VMEM((npair, 2, HEAD_DIM, seq), BF16),
            pltpu.VMEM((npair, 2 * LANES, 2 * tq), BF16),
            pltpu.VMEM((npair, tk, 2 * tq), F32),
            pltpu.VMEM((npair, tk, 2 * tq), F32),
            pltpu.VMEM((npair, 1, 2 * tq), F32),
            pltpu.VMEM((npair, 1, 2 * tq), F32),
            pltpu.VMEM((npair, 1, 2 * tq), F32),
            pltpu.VMEM((npair, 1, 2 * tq), F32),
            pltpu.VMEM((npair, 2, HEAD_DIM, tq), F32),
        ],
        compiler_params=_cparams(("parallel", "parallel", "arbitrary")),
        name="attn",
    )(q3, k3, v3, c3, ct, qw_row, kw_row).reshape(batch * seq, d_att)


def _outproj_kernel(ys_ref, ya_ref, x_ref, wos_ref, woa_ref, aw_ref, fw_ref, wr_ref, br_ref,
                    x1_ref, xn_ref, gates_ref):
    ya = ya_ref[...].astype(F32)
    ya = ya * lax.rsqrt(jnp.mean(ya * ya, axis=-1, keepdims=True) + EPS) * aw_ref[...]
    x1 = x_ref[...] + _dot(ys_ref[...], wos_ref[...]) + _dot(ya.astype(BF16), woa_ref[...])
    x1_ref[...] = x1
    xn = x1 * lax.rsqrt(jnp.mean(x1 * x1, axis=-1, keepdims=True) + EPS) * fw_ref[...]
    xn_ref[...] = xn.astype(BF16)

    x_hi = xn.astype(BF16)
    x_lo = (xn - x_hi.astype(F32)).astype(BF16)
    w = wr_ref[...]
    w_hi = w.astype(BF16)
    w_lo = (w - w_hi.astype(F32)).astype(BF16)
    hi_terms = _dot(x_hi, jnp.concatenate([w_hi, w_lo], axis=1))
    logits = hi_terms[:, :LANES] + _dot(x_lo, w_hi) + hi_terms[:, LANES:] + br_ref[...]
    lane = lax.broadcasted_iota(jnp.int32, logits.shape, 1).astype(F32)

    def rmax(mask):
        return jnp.max(jnp.where(mask, logits, NEG_BIG), axis=-1, keepdims=True)

    def first_at(mask, val):
        return jnp.min(jnp.where(mask & (logits == val), lane, 1e9), axis=-1, keepdims=True)

    gmask = (lane >= N_EXPERTS) & (lane < N_EXPERTS + N_EXPERT_GROUPS)
    gmax = rmax(gmask)
    p_g = 1.0 / jnp.sum(jnp.where(gmask, jnp.exp(logits - gmax), 0.0), axis=-1, keepdims=True)
    g_star = first_at(gmask, gmax) - N_EXPERTS
    lo = g_star * EXPERTS_PER_GROUP
    emask = (lane >= lo) & (lane < lo + EXPERTS_PER_GROUP)
    v1 = rmax(emask)
    i1 = first_at(emask, v1)
    mask2 = emask & (lane != i1)
    v2 = rmax(mask2)
    i2 = first_at(mask2, v2)
    e2 = jnp.exp(v2 - v1)
    w1 = p_g / (1.0 + e2)
    w2 = p_g * e2 / (1.0 + e2)
    gates_ref[...] = (jnp.where(lane == i1, w1, 0.0) + jnp.where(lane == i2, w2, 0.0)
                      + jnp.where(lane == N_EXPERTS, g_star, 0.0))


def _outproj(ys, ya, x, wos, woa, aw, fw, wr, br, tm):
    t, d = x.shape
    dm = ys.shape[1]
    rows = lambda i: (i, 0)
    const = lambda i: (0, 0)
    return pl.pallas_call(
        _outproj_kernel,
        grid=(t // tm,),
        in_specs=[
            pl.BlockSpec((tm, dm), rows),
            pl.BlockSpec((tm, dm), rows),
            pl.BlockSpec((tm, d), rows),
            pl.BlockSpec((dm, d), const),
            pl.BlockSpec((dm, d), const),
            pl.BlockSpec((1, dm), const),
            pl.BlockSpec((1, d), const),
            pl.BlockSpec((d, LANES), const),
            pl.BlockSpec((1, LANES), const),
        ],
        out_specs=[
            pl.BlockSpec((tm, d), rows),
            pl.BlockSpec((tm, d), rows),
            pl.BlockSpec((tm, LANES), rows),
        ],
        out_shape=[
            jax.ShapeDtypeStruct((t, d), F32),
            jax.ShapeDtypeStruct((t, d), BF16),
            jax.ShapeDtypeStruct((t, LANES), F32),
        ],
        compiler_params=_cparams(("parallel",)),
        name="outproj",
    )(ys, ya, x, wos, woa, aw, fw, wr, br)


MOE_BLOCK = 1024
MOE_CHUNK = 128
MOE_ROWS = MOE_BLOCK + N_EXPERT_GROUPS * MOE_CHUNK
GROUP_LANE = N_EXPERTS


def _group_runs(count):
    nch = jnp.floor((count + (MOE_CHUNK - 1)) * (1.0 / MOE_CHUNK))
    return nch, nch * MOE_CHUNK


def _moe_dispatch_kernel(xn_ref, route_ref, before_ref, xs_ref, gs_ref, meta_ref):
    pt = xn_ref.shape[0]
    route = route_ref[...]
    rt = route.T
    g_row = rt[GROUP_LANE:GROUP_LANE + 1, :]
    grp = lax.broadcasted_iota(jnp.int32, (8, pt), 0).astype(F32)
    onehot = jnp.where(grp == g_row, 1.0, 0.0)
    rank = _dot(onehot.astype(BF16), before_ref[...])
    nch, padded = _group_runs(jnp.sum(onehot, axis=1, keepdims=True))
    gi = lax.broadcasted_iota(jnp.int32, (8, 1), 0)
    start = jnp.zeros((8, 1), F32)
    for g in range(N_EXPERT_GROUPS - 1):
        start = start + jnp.where(gi > g, padded[g:g + 1, :], 0.0)
    pos = jnp.sum(onehot * (start + rank), axis=0, keepdims=True)
    dest = lax.broadcasted_iota(jnp.int32, (MOE_ROWS, pt), 0).astype(F32)
    perm = jnp.where(dest == pos, 1.0, 0.0).astype(BF16)
    g_hi = route.astype(BF16)
    g_lo = (route - g_hi.astype(F32)).astype(BF16)
    moved = _dot(perm, jnp.concatenate([xn_ref[...], g_hi, g_lo], axis=1))
    d = xn_ref.shape[1]
    xs_ref[0] = moved[:, :d].astype(BF16)
    gs_ref[0] = moved[:, d:d + LANES] + moved[:, d + LANES:]
    lane = lax.broadcasted_iota(jnp.int32, (8, LANES), 1)
    meta_ref[0] = jnp.where(lane == 0, start, jnp.where(lane == 1, nch, 0.0))


def _moe_expert_kernel(n_sub, n_exp, meta_ref, xs_ref, gs_ref, wg_ref, wu_ref, wd_ref, ys_ref):
    st = pl.program_id(0)
    eb = pl.program_id(1)
    g = (eb * n_exp) // EXPERTS_PER_GROUP

    @pl.when(eb == 0)
    def _():
        ys_ref[...] = jnp.zeros(ys_ref.shape, F32)

    def run(s, first_row, n_rows):
        rows = pl.ds(pl.multiple_of(first_row, MOE_CHUNK), n_rows)
        lane = lax.broadcasted_iota(jnp.int32, (n_rows, LANES), 1)
        xc = xs_ref[s, rows, :]
        gates = gs_ref[s, rows, :]
        y = ys_ref[s, rows, :]
        for k in range(n_exp):
            gate = jnp.sum(jnp.where(lane == eb * n_exp + k, gates, 0.0), axis=-1, keepdims=True)
            hid = _silu(_dot(xc, wg_ref[k])) * _dot(xc, wu_ref[k]) * gate
            y = y + _dot(hid.astype(BF16), wd_ref[k])
        ys_ref[s, rows, :] = y

    for s in range(n_sub):
        base = ((st * n_sub + s) * N_EXPERT_GROUPS + g) * 2
        start = meta_ref[base]
        n_chunks = meta_ref[base + 1]

        def two_chunks(c, carry, s=s, start=start):
            run(s, start + c * (2 * MOE_CHUNK), 2 * MOE_CHUNK)
            return carry
        lax.fori_loop(0, n_chunks // 2, two_chunks, 0)

        @pl.when(n_chunks % 2 == 1)
        def _(s=s, start=start, n_chunks=n_chunks):
            run(s, start + (n_chunks - 1) * MOE_CHUNK, MOE_CHUNK)


def _moe_combine_kernel(ys_ref, route_ref, x1_ref, before_ref, o_ref):
    pt = x1_ref.shape[0]
    route = route_ref[...]
    lane = lax.broadcasted_iota(jnp.int32, (pt, LANES), 1).astype(F32)
    onehot = jnp.where(lane == route[:, GROUP_LANE:GROUP_LANE + 1], 1.0, 0.0)
    rank = _dot(before_ref[...], onehot.astype(BF16))
    _, padded = _group_runs(jnp.sum(onehot, axis=0, keepdims=True))
    lane1 = lax.broadcasted_iota(jnp.int32, (1, LANES), 1)
    start = jnp.zeros((1, LANES), F32)
    for g in range(N_EXPERT_GROUPS - 1):
        start = start + jnp.where(lane1 > g, padded[:, g:g + 1], 0.0)
    pos = jnp.sum(onehot * (start + rank), axis=-1, keepdims=True)
    src = lax.broadcasted_iota(jnp.int32, (pt, MOE_ROWS), 1).astype(F32)
    perm_t = jnp.where(src == pos, 1.0, 0.0).astype(BF16)
    o_ref[...] = x1_ref[...] + _dot(perm_t, ys_ref[0].astype(BF16))


def _moe(xn, route, x1, wg, wu, wd, n_sub):
    t, d = x1.shape
    n_e, _, f = wg.shape
    pt = MOE_BLOCK
    assert t % (pt * n_sub) == 0
    nb = t // pt
    idx = jnp.arange(pt)
    before_row = (idx[:, None] < idx[None, :]).astype(BF16)
    before_col = (idx[None, :] < idx[:, None]).astype(BF16)
    rows = lambda i: (i, 0)
    const = lambda i: (0, 0)
    xs, gs, meta = pl.pallas_call(
        _moe_dispatch_kernel,
        grid=(nb,),
        in_specs=[
            pl.BlockSpec((pt, d), rows),
            pl.BlockSpec((pt, LANES), rows),
            pl.BlockSpec((pt, pt), const),
        ],
        out_specs=[
            pl.BlockSpec((1, MOE_ROWS, d), lambda i: (i, 0, 0)),
            pl.BlockSpec((1, MOE_ROWS, LANES), lambda i: (i, 0, 0)),
            pl.BlockSpec((1, 8, LANES), lambda i: (i, 0, 0)),
        ],
        out_shape=[
            jax.ShapeDtypeStruct((nb, MOE_ROWS, d), BF16),
            jax.ShapeDtypeStruct((nb, MOE_ROWS, LANES), F32),
            jax.ShapeDtypeStruct((nb, 8, LANES), F32),
        ],
        compiler_params=_cparams(("parallel",)),
        name="moe_dispatch",
    )(xn, route, before_row)

    meta_i = meta[:, :N_EXPERT_GROUPS, :2].astype(jnp.int32).reshape(-1)
    n_exp = 4
    assert EXPERTS_PER_GROUP % n_exp == 0
    ys = pl.pallas_call(
        functools.partial(_moe_expert_kernel, n_sub, n_exp),
        grid_spec=pltpu.PrefetchScalarGridSpec(
            num_scalar_prefetch=1,
            grid=(nb // n_sub, n_e // n_exp),
            in_specs=[
                pl.BlockSpec((n_sub, MOE_ROWS, d), lambda i, e, m: (i, 0, 0)),
                pl.BlockSpec((n_sub, MOE_ROWS, LANES), lambda i, e, m: (i, 0, 0)),
                pl.BlockSpec((n_exp, d, f), lambda i, e, m: (e, 0, 0)),
                pl.BlockSpec((n_exp, d, f), lambda i, e, m: (e, 0, 0)),
                pl.BlockSpec((n_exp, f, d), lambda i, e, m: (e, 0, 0)),
            ],
            out_specs=pl.BlockSpec((n_sub, MOE_ROWS, d), lambda i, e, m: (i, 0, 0)),
        ),
        out_shape=jax.ShapeDtypeStruct((nb, MOE_ROWS, d), F32),
        compiler_params=_cparams(("parallel", "arbitrary")),
        name="moe_experts",
    )(meta_i, xs, gs, wg, wu, wd)

    return pl.pallas_call(
        _moe_combine_kernel,
        grid=(nb,),
        in_specs=[
            pl.BlockSpec((1, MOE_ROWS, d), lambda i: (i, 0, 0)),
            pl.BlockSpec((pt, LANES), rows),
            pl.BlockSpec((pt, d), rows),
            pl.BlockSpec((pt, pt), const),
        ],
        out_specs=pl.BlockSpec((pt, d), rows),
        out_shape=jax.ShapeDtypeStruct((t, d), F32),
        compiler_params=_cparams(("parallel",)),
        name="moe_combine",
    )(ys, route, x1, before_col)


def _pad_lanes(v, offset=0):
    out = jnp.zeros((1, LANES), F32)
    return out.at[0, offset:offset + v.shape[0]].set(v.astype(F32))


def kernel(x, norm_mix_w, w_in, conv_w, conv_b, dt_bias, a_log, d_skip, ssm_norm_w, q_norm_w,
           k_norm_w, fox_f_bias, att_out_norm_w, w_out, norm_ffn_w, w_router_group,
           b_router_group, w_router_expert, b_router_expert, w_expert_gate, w_expert_up,
           w_expert_down):
    batch, seq, d = x.shape
    depth = w_in.shape[0]
    d_ssm = ssm_norm_w.shape[1]
    d_att = att_out_norm_w.shape[1]
    h_ssm = dt_bias.shape[1]
    h_att = fox_f_bias.shape[1]
    conv_dim = conv_w.shape[2]
    assert h_ssm == 16 and h_att == 16 and d_ssm // h_ssm == HEAD_DIM and d_att // h_att == HEAD_DIM
    assert conv_dim == d_ssm + 2 * SSM_GROUPS * D_STATE
    off_xbc = d_ssm
    off_dt = off_xbc + conv_dim
    off_q = off_dt + h_ssm
    off_k = off_q + d_att
    off_v = off_k + d_att
    off_f = off_v + d_att
    t = batch * seq
    tm_in = min(512, t)
    tm_out = min(512, t)
    moe_sub = 2 if t % (2 * MOE_BLOCK) == 0 else 1
    tq = min(512, seq)
    tk = tq

    head_of_chan = jnp.arange(d_ssm) // HEAD_DIM
    expand = (jnp.arange(LANES)[:, None] == head_of_chan[None, :]).astype(BF16)

    xf = x.reshape(t, d)
    for l in range(depth):
        wl = w_in[l]
        w_main = jnp.concatenate(
            [wl[:, :off_xbc], wl[:, off_q:off_k], wl[:, off_k:off_v], wl[:, off_v:off_f],
             wl[:, off_xbc:off_dt]], axis=1).astype(BF16)
        p_q, p_k, p_v, p_xbc = d_ssm, d_ssm + d_att, d_ssm + 2 * d_att, d_ssm + 3 * d_att
        w_small = jnp.zeros((d, LANES), F32)
        w_small = w_small.at[:, 0:h_ssm].set(wl[:, off_dt:off_q])
        w_small = w_small.at[:, 16:16 + h_att].set(wl[:, off_f:off_f + h_att]).astype(BF16)
        proj, small = _inproj(xf, norm_mix_w[l][None, :], w_main, w_small, tm_in, 512)

        bias_row = _pad_lanes(dt_bias[l]) + _pad_lanes(fox_f_bias[l], 16)
        a_row = _pad_lanes(-jnp.exp(a_log[l].astype(F32)))
        dskip_row = jnp.repeat(d_skip[l].astype(F32), HEAD_DIM)[None, :]
        y_ssm, c, ct = _ssd(proj, 0, p_xbc, conv_dim, d_ssm, small, conv_w[l].astype(F32),
                        conv_b[l].astype(F32)[None, :], bias_row, a_row, dskip_row,
                        ssm_norm_w[l][None, :], expand, batch, seq)

        qw_row = jnp.tile(q_norm_w[l].astype(F32), 2)[None, :]
        kw_row = jnp.tile(k_norm_w[l].astype(F32), 2)[None, :]
        y_att = _attn(proj, p_q, p_k, p_v, d_att, c, ct, qw_row, kw_row, batch, seq, tq, tk, 4)

        wr = jnp.zeros((d, LANES), F32)
        wr = wr.at[:, :N_EXPERTS].set(w_router_expert[l])
        wr = wr.at[:, N_EXPERTS:N_EXPERTS + N_EXPERT_GROUPS].set(w_router_group[l])
        br = _pad_lanes(b_router_expert[l]) + _pad_lanes(b_router_group[l], N_EXPERTS)
        wo = w_out[l].astype(BF16)
        x1, xn, gates = _outproj(y_ssm, y_att, xf, wo[:d_ssm], wo[d_ssm:],
                                 att_out_norm_w[l][None, :], norm_ffn_w[l][None, :], wr, br, tm_out)

        xf = _moe(xn, gates, x1, w_expert_gate[l].astype(BF16), w_expert_up[l].astype(BF16),
                  w_expert_down[l].astype(BF16), moe_sub)
    return xf.reshape(batch, seq, d)
```

```python
import functools

import jax
import jax.numpy as jnp
import numpy as np
from jax import lax
from jax.experimental import pallas as pl
from jax.experimental.pallas import tpu as pltpu

F32 = jnp.float32
BF16 = jnp.bfloat16

HEAD_DIM = 64
D_STATE = 128
SSM_GROUPS = 2
CONV_K = 4
SSD_CHUNK = 128
N_EXPERT_GROUPS = 4
EXPERTS_PER_GROUP = 8
N_EXPERTS = N_EXPERT_GROUPS * EXPERTS_PER_GROUP
EPS = 1e-6
LANES = 128
NEG_BIG = -1e30
LOG2E = 1.4426950408889634
VMEM_LIMIT = 56 * 1024 * 1024


def _cparams(sem):
    return pltpu.CompilerParams(dimension_semantics=sem, vmem_limit_bytes=VMEM_LIMIT)


def _split3(a):
    a1 = a.astype(BF16)
    r1 = a - a1.astype(F32)
    a2 = r1.astype(BF16)
    a3 = (r1 - a2.astype(F32)).astype(BF16)
    return a1, a2, a3


def _dot(a, b):
    return jnp.dot(a, b, preferred_element_type=F32)


def _dot3(a_f32, b_bf16):
    a1, a2, a3 = _split3(a_f32)
    return _dot(a1, b_bf16) + _dot(a2, b_bf16) + _dot(a3, b_bf16)


def _dot2(a_f32, b_bf16):
    a1 = a_f32.astype(BF16)
    a2 = (a_f32 - a1.astype(F32)).astype(BF16)
    return _dot(a1, b_bf16) + _dot(a2, b_bf16)


def _dot3_tn(b_bf16_t, a_f32):
    a1, a2, a3 = _split3(a_f32)
    return _dot(b_bf16_t, a1) + _dot(b_bf16_t, a2) + _dot(b_bf16_t, a3)


def _silu(x):
    return x / (1.0 + jnp.exp(-x))


def _inproj_kernel(tn, x_ref, nw_ref, w_ref, ws_ref, proj_ref, small_ref):
    x = x_ref[...]
    ms = jnp.mean(x * x, axis=-1, keepdims=True)
    hb = (x * lax.rsqrt(ms + EPS) * nw_ref[...]).astype(BF16)
    small_ref[...] = _dot(hb, ws_ref[...])
    for j in range(w_ref.shape[1] // tn):
        cols = slice(j * tn, (j + 1) * tn)
        proj_ref[:, cols] = _dot(hb, w_ref[:, cols]).astype(BF16)


def _inproj(x2, nw, w_main, w_small, tm, tn):
    t, d = x2.shape
    n = w_main.shape[1]
    assert n % tn == 0 and t % tm == 0
    return pl.pallas_call(
        functools.partial(_inproj_kernel, tn),
        grid=(t // tm,),
        in_specs=[
            pl.BlockSpec((tm, d), lambda i: (i, 0)),
            pl.BlockSpec((1, d), lambda i: (0, 0)),
            pl.BlockSpec((d, n), lambda i: (0, 0)),
            pl.BlockSpec((d, LANES), lambda i: (0, 0)),
        ],
        out_specs=[
            pl.BlockSpec((tm, n), lambda i: (i, 0)),
            pl.BlockSpec((tm, LANES), lambda i: (i, 0)),
        ],
        out_shape=[
            jax.ShapeDtypeStruct((t, n), BF16),
            jax.ShapeDtypeStruct((t, LANES), F32),
        ],
        compiler_params=_cparams(("parallel",)),
        name="inproj",
    )(x2, nw, w_main, w_small)


def _ssd_kernel(d_ssm, n_xbc, n_seq, *refs):
    ext_ref, state_ref, carry_ref = refs[-3:]
    L = SSD_CHUNK

    @pl.when(pl.program_id(1) == 0)
    def _():
        ext_ref[:, L:2 * L, :] = jnp.zeros((n_seq, L, ext_ref.shape[2]), BF16)
        state_ref[...] = jnp.zeros(state_ref.shape, F32)
        carry_ref[...] = jnp.zeros(carry_ref.shape, F32)

    for bb in range(n_seq):
        _ssd_chunk(bb, d_ssm, n_xbc, refs)


def _ssd_chunk(bb, d_ssm, n_xbc, refs):
    xbc_refs = refs[:n_xbc]
    (z_ref, small_ref, convw_ref, convb_ref, bias_ref, arow_ref, dskip_ref, nw_ref, expand_ref,
     shift_ref, y_ref, c_ref, ct_ref, ext_ref, state_ref, carry_ref) = refs[n_xbc:]
    L = SSD_CHUNK
    n_pairs = d_ssm // LANES
    gw = d_ssm // SSM_GROUPS

    u = jnp.concatenate([r[bb] for r in xbc_refs], axis=1)
    ext_ref[bb, 0:L, :] = ext_ref[bb, L:2 * L, :]
    ext_ref[bb, L:2 * L, :] = u
    n_delay = CONV_K - 1
    delayed = _dot(shift_ref[...], ext_ref[bb])
    acc = convb_ref[...] + convw_ref[n_delay:CONV_K, :] * u.astype(F32)
    for k in range(n_delay):
        acc = acc + convw_ref[k:k + 1, :] * delayed[k * L:(k + 1) * L, :]
    xbc = _silu(acc)
    xs = xbc[:, :d_ssm]
    b_all = xbc[:, d_ssm:d_ssm + SSM_GROUPS * D_STATE].astype(BF16)
    c_all = xbc[:, d_ssm + SSM_GROUPS * D_STATE:].astype(BF16)

    row = lax.broadcasted_iota(jnp.int32, (L, L), 0)
    col = lax.broadcasted_iota(jnp.int32, (L, L), 1)
    causal = row >= col
    tril = jnp.where(causal, 1.0, 0.0).astype(BF16)
    pre = small_ref[bb] + bias_ref[...]
    sp = jnp.maximum(pre, 0.0) + jnp.log(1.0 + jnp.exp(-jnp.abs(pre)))
    lane = lax.broadcasted_iota(jnp.int32, (L, LANES), 1)
    is_dt = lane < 16
    dt = jnp.where(is_dt, sp, 0.0)
    logf = jnp.where((lane >= 16) & (lane < 32), pre - sp, 0.0)
    a = dt * arow_ref[...]
    a_and_f = a + logf
    cum_all = _dot3_tn(tril, a_and_f)
    cum = jnp.where(is_dt, cum_all, 0.0)
    cum_t = cum.T
    c_run = cum_all + carry_ref[bb, 0:1, :]
    c_ref[bb] = c_run
    ct_ref[bb] = c_run.T
    carry_ref[bb, 0:1, :] = c_run[L - 1:L, :]

    expand = expand_ref[...]
    dt_x = _dot2(dt, expand)
    ecum_x = _dot2(jnp.exp(cum), expand)
    dte_x = _dot2(jnp.exp(cum[L - 1:L, :] - cum), expand)
    xdt = xs * dt_x
    xdt_b = xdt.astype(BF16)
    w_b = (xdt * dte_x).astype(BF16)

    y_parts = []
    for g in range(SSM_GROUPS):
        bg = b_all[:, g * D_STATE:(g + 1) * D_STATE]
        cg = c_all[:, g * D_STATE:(g + 1) * D_STATE]
        cb = lax.dot_general(cg, bg, (((1,), (1,)), ((), ())), preferred_element_type=F32)
        s_old = state_ref[bb, g]
        y_off = _dot(cg, s_old.astype(BF16)) * ecum_x[:, g * gw:(g + 1) * gw]
        upd = lax.dot_general(bg, w_b[:, g * gw:(g + 1) * gw], (((0,), (0,)), ((), ())),
                              preferred_element_type=F32)
        state_ref[bb, g] = s_old * ecum_x[L - 1:L, g * gw:(g + 1) * gw] + upd
        pairs_per_group = n_pairs // SSM_GROUPS
        for pp in range(pairs_per_group):
            p = g * pairs_per_group + pp
            xp = xdt_b[:, p * LANES:(p + 1) * LANES]
            ys = []
            for hh in range(2):
                h = 2 * p + hh
                seg = cum[:, h:h + 1] - cum_t[h:h + 1, :]
                decay = jnp.exp(jnp.where(causal, seg, NEG_BIG))
                ys.append(_dot((cb * decay).astype(BF16), xp))
            first = lax.broadcasted_iota(jnp.int32, (L, LANES), 1) < HEAD_DIM
            y_parts.append(jnp.where(first, ys[0], ys[1])
                           + y_off[:, pp * LANES:(pp + 1) * LANES])
    y = jnp.concatenate(y_parts, axis=1) + xs * dskip_ref[...]
    gated = y * _silu(z_ref[bb].astype(F32))
    ms = jnp.mean(gated * gated, axis=-1, keepdims=True)
    y_ref[bb] = (gated * lax.rsqrt(ms + EPS) * nw_ref[...]).astype(BF16)


def _ssd(proj, z_off, xbc_off, cdim, d_ssm, small, convw, convb, bias_row, a_row, dskip_row, nw,
         expand, batch, seq):
    t = proj.shape[0]
    L = SSD_CHUNK
    nc = seq // L
    wx = 512
    assert cdim % wx == 0 and xbc_off % wx == 0 and z_off % d_ssm == 0
    n_xbc = cdim // wx
    n_seq = max(n for n in (4, 2, 1) if batch % n == 0)
    proj3 = proj.reshape(batch, seq, proj.shape[1])
    small3 = small.reshape(batch, seq, LANES)
    rowmap = lambda b, c: (b, c, 0)
    const = lambda b, c: (0, 0)
    n_delay = CONV_K - 1
    r = jnp.arange(n_delay * L)
    target = L + r % L - (n_delay - r // L)
    shift = (jnp.arange(2 * L)[None, :] == target[:, None]).astype(BF16)
    xbc_specs = [pl.BlockSpec((n_seq, L, wx), lambda b, c, o=xbc_off // wx + i: (b, c, o))
                 for i in range(n_xbc)]
    y, c_out, ct = pl.pallas_call(
        functools.partial(_ssd_kernel, d_ssm, n_xbc, n_seq),
        grid=(batch // n_seq, nc),
        in_specs=xbc_specs + [
            pl.BlockSpec((n_seq, L, d_ssm), lambda b, c: (b, c, z_off // d_ssm)),
            pl.BlockSpec((n_seq, L, LANES), rowmap),
            pl.BlockSpec((CONV_K, cdim), const),
            pl.BlockSpec((1, cdim), const),
            pl.BlockSpec((1, LANES), const),
            pl.BlockSpec((1, LANES), const),
            pl.BlockSpec((1, d_ssm), const),
            pl.BlockSpec((1, d_ssm), const),
            pl.BlockSpec((LANES, d_ssm), const),
            pl.BlockSpec((n_delay * L, 2 * L), const),
        ],
        out_specs=[
            pl.BlockSpec((n_seq, L, d_ssm), rowmap),
            pl.BlockSpec((n_seq, L, LANES), rowmap),
            pl.BlockSpec((n_seq, LANES, L), lambda b, c: (b, 0, c)),
        ],
        out_shape=[
            jax.ShapeDtypeStruct((batch, seq, d_ssm), BF16),
            jax.ShapeDtypeStruct((batch, seq, LANES), F32),
            jax.ShapeDtypeStruct((batch, LANES, seq), F32),
        ],
        scratch_shapes=[
            pltpu.VMEM((n_seq, 2 * L, cdim), BF16),
            pltpu.VMEM((n_seq, SSM_GROUPS, D_STATE, d_ssm // SSM_GROUPS), F32),
            pltpu.VMEM((n_seq, 8, LANES), F32),
        ],
        compiler_params=_cparams(("parallel", "arbitrary")),
        name="ssd",
    )(*([proj3] * n_xbc), proj3, small3, convw, convb, bias_row, a_row, dskip_row, nw, expand, shift)
    return y.reshape(t, d_ssm), c_out.reshape(t, LANES), ct


def _headnorm(x, w_row, same_head):
    ms = _dot((x * x).astype(BF16), same_head) * (1.0 / HEAD_DIM)
    return x * lax.rsqrt(ms + EPS) * w_row


def _place(parts, moves, rr, cc):
    out = None
    for i, part in enumerate(parts):
        sel = None
        for src, dst in moves:
            one = jnp.where((rr == src) & (cc == dst + i), 1.0, 0.0)
            sel = one if sel is None else sel + one
        term = _dot(part, sel.astype(BF16))
        out = term if out is None else out + term
    return out


def _attn_kernel(tq, tk, npair, q_ref, k_ref, v_ref, c_ref, ct_ref, qw_ref, kw_ref, o_ref,
                 kx_ref, vx_ref, qx_ref, sa_ref, sb_ref, ta_ref, tb_ref, m_ref, l_ref, acc_ref):
    grp = pl.program_id(1)
    qi = pl.program_id(2)
    seq = k_ref.shape[1]
    lane = lax.broadcasted_iota(jnp.int32, (1, LANES), 1)
    first = lane < HEAD_DIM
    rr = lax.broadcasted_iota(jnp.int32, (LANES, LANES), 0)
    cc = lax.broadcasted_iota(jnp.int32, (LANES, LANES), 1)
    same_head = jnp.where((rr < HEAD_DIM) == (cc < HEAD_DIM), 1.0, 0.0).astype(BF16)
    pairs = range(npair)
    bases = [16 + 2 * (grp * npair + p) for p in pairs]

    @pl.when(qi == 0)
    def _():
        k_const = jnp.where((lane >= 6) & (lane < 9), 1.0, 0.0)

        def body(r, carry):
            rows = pl.ds(pl.multiple_of(r * tq, tq), tq)
            parts = _split3(c_ref[0, rows, :] * LOG2E)
            for p in pairs:
                lanes = slice(p * LANES, (p + 1) * LANES)
                kn = _headnorm(k_ref[0, rows, lanes].astype(F32), kw_ref[...], same_head)
                kext = _place(parts, [(bases[p], 0), (bases[p] + 1, 3)], rr, cc) + k_const
                kx_ref[p, rows, 0:LANES] = kn.astype(BF16)
                kx_ref[p, rows, LANES:2 * LANES] = kext.astype(BF16)
                v = v_ref[0, rows, lanes].astype(F32)
                vt = v.T.astype(BF16)
                vx_ref[p, 0, :, rows] = vt[0:HEAD_DIM]
                vx_ref[p, 1, :, rows] = vt[HEAD_DIM:LANES]
            return carry
        lax.fori_loop(0, seq // tq, body, 0)

    scale = (1.0 / np.sqrt(HEAD_DIM)) * LOG2E
    sub = lax.broadcasted_iota(jnp.int32, (LANES, tq), 0)
    for p in pairs:
        qn = _headnorm(q_ref[0, :, p * LANES:(p + 1) * LANES].astype(F32), qw_ref[...],
                       same_head) * scale
        qt = qn.T
        for hh in range(2):
            c_hi, c_mid, c_lo = _split3(ct_ref[0, pl.ds(bases[p] + hh, 1), :] * LOG2E)
            minus = (sub >= 3 * hh) & (sub < 3 * hh + 3)
            ext = jnp.where(sub == 6, c_hi.astype(F32), jnp.where(sub == 7, c_mid.astype(F32),
                  jnp.where(sub == 8, c_lo.astype(F32), jnp.where(minus, -1.0, 0.0))))
            own = (sub < HEAD_DIM) if hh == 0 else (sub >= HEAD_DIM)
            qx_ref[p, 0:LANES, hh * tq:(hh + 1) * tq] = jnp.where(own, qt, 0.0).astype(BF16)
            qx_ref[p, LANES:2 * LANES, hh * tq:(hh + 1) * tq] = ext.astype(BF16)
        m_ref[p] = jnp.full((1, 2 * tq), NEG_BIG, F32)
        l_ref[p] = jnp.zeros((1, 2 * tq), F32)
        acc_ref[p] = jnp.zeros((2, HEAD_DIM, tq), F32)

    n_full = (qi * tq) // tk
    q_first = qi * tq - n_full * tk
    key_i = lax.broadcasted_iota(jnp.int32, (tk, 2 * tq), 0)
    qry_i = lax.broadcasted_iota(jnp.int32, (tk, 2 * tq), 1)
    keep = jnp.where(qry_i >= tq, qry_i - tq, qry_i) + q_first >= key_i

    def scores(j, buf):
        s_ref, top_ref = buf
        rows = pl.ds(pl.multiple_of(j * tk, tk), tk)
        for p in pairs:
            s = _dot(kx_ref[p, rows, :], qx_ref[p])
            s_ref[p] = s
            top_ref[p] = jnp.max(s, axis=0, keepdims=True)

    def softmax_pv(j, buf, masked):
        s_ref, top_ref = buf
        cols = pl.ds(pl.multiple_of(j * tk, tk), tk)
        for p in pairs:
            s = s_ref[p]
            if masked:
                s = jnp.where(keep, s, NEG_BIG)
                top = jnp.max(s, axis=0, keepdims=True)
            else:
                top = top_ref[p]
            m = m_ref[p]
            m_new = jnp.maximum(m, top)
            alpha = jnp.exp2(m - m_new)
            pw = jnp.exp2(s - m_new)
            l_ref[p] = alpha * l_ref[p] + jnp.sum(pw, axis=0, keepdims=True)
            pb = pw.astype(BF16)
            acc_ref[p, 0] = alpha[:, :tq] * acc_ref[p, 0] + _dot(vx_ref[p, 0, :, cols], pb[:, :tq])
            acc_ref[p, 1] = alpha[:, tq:] * acc_ref[p, 1] + _dot(vx_ref[p, 1, :, cols], pb[:, tq:])
            m_ref[p] = m_new

    buf_a = (sa_ref, ta_ref)
    buf_b = (sb_ref, tb_ref)
    scores(0, buf_a)

    def two_blocks(i, carry):
        j = 2 * i
        scores(j + 1, buf_b)
        softmax_pv(j, buf_a, False)
        scores(j + 2, buf_a)
        softmax_pv(j + 1, buf_b, False)
        return carry
    lax.fori_loop(0, n_full // 2, two_blocks, 0)

    @pl.when(n_full % 2 == 0)
    def _():
        softmax_pv(n_full, buf_a, True)

    @pl.when(n_full % 2 == 1)
    def _():
        scores(n_full, buf_b)
        softmax_pv(n_full - 1, buf_a, False)
        softmax_pv(n_full, buf_b, True)

    for p in pairs:
        out_t = jnp.concatenate(
            [acc_ref[p, 0] / l_ref[p, :, 0:tq], acc_ref[p, 1] / l_ref[p, :, tq:2 * tq]], axis=0)
        o_ref[0, :, p * LANES:(p + 1) * LANES] = out_t.T.astype(BF16)


def _attn(proj, q_off, k_off, v_off, d_att, c, ct, qw_row, kw_row, batch, seq, tq, tk, npair):
    width = npair * LANES
    assert d_att % width == 0 and seq % tq == 0 and seq % tk == 0 and tk % tq == 0
    assert q_off % width == 0 and k_off % width == 0 and v_off % width == 0
    qo, ko, vo = q_off // width, k_off // width, v_off // width
    q3 = k3 = v3 = proj.reshape(batch, seq, proj.shape[-1])
    c3 = c.reshape(batch, seq, LANES)
    return pl.pallas_call(
        functools.partial(_attn_kernel, tq, tk, npair),
        grid=(batch, d_att // width, seq // tq),
        in_specs=[
            pl.BlockSpec((1, tq, width), lambda b, h, i: (b, i, qo + h)),
            pl.BlockSpec((1, seq, width), lambda b, h, i: (b, 0, ko + h), pipeline_mode=pl.Buffered(1)),
            pl.BlockSpec((1, seq, width), lambda b, h, i: (b, 0, vo + h), pipeline_mode=pl.Buffered(1)),
            pl.BlockSpec((1, seq, LANES), lambda b, h, i: (b, 0, 0), pipeline_mode=pl.Buffered(1)),
            pl.BlockSpec((1, LANES, tq), lambda b, h, i: (b, 0, i)),
            pl.BlockSpec((1, LANES), lambda b, h, i: (0, 0)),
            pl.BlockSpec((1, LANES), lambda b, h, i: (0, 0)),
        ],
        out_specs=pl.BlockSpec((1, tq, width), lambda b, h, i: (b, i, h)),
        out_shape=jax.ShapeDtypeStruct((batch, seq, d_att), BF16),
        scratch_shapes=[
            pltpu.VMEM((npair, seq, 2 * LANES), BF16),
            pltpu.VMEM((npair, 2, HEAD_DIM, seq), BF16),
            pltpu.VMEM((npair, 2 * LANES, 2 * tq), BF16),
            pltpu.VMEM((npair, tk, 2 * tq), F32),
            pltpu.VMEM((npair, tk, 2 * tq), F32),
            pltpu.VMEM((npair, 1, 2 * tq), F32),
            pltpu.VMEM((npair, 1, 2 * tq), F32),
            pltpu.VMEM((npair, 1, 2 * tq), F32),
            pltpu.VMEM((npair, 1, 2 * tq), F32),
            pltpu.VMEM((npair, 2, HEAD_DIM, tq), F32),
        ],
        compiler_params=_cparams(("parallel", "parallel", "arbitrary")),
        name="attn",
    )(q3, k3, v3, c3, ct, qw_row, kw_row).reshape(batch * seq, d_att)


def _outproj_kernel(ys_ref, ya_ref, x_ref, wos_ref, woa_ref, aw_ref, fw_ref, wr_ref, br_ref,
                    x1_ref, xn_ref, gates_ref):
    ya = ya_ref[...].astype(F32)
    ya = ya * lax.rsqrt(jnp.mean(ya * ya, axis=-1, keepdims=True) + EPS) * aw_ref[...]
    x1 = x_ref[...] + _dot(ys_ref[...], wos_ref[...]) + _dot(ya.astype(BF16), woa_ref[...])
    x1_ref[...] = x1
    xn = x1 * lax.rsqrt(jnp.mean(x1 * x1, axis=-1, keepdims=True) + EPS) * fw_ref[...]
    xn_ref[...] = xn.astype(BF16)

    x_hi = xn.astype(BF16)
    x_lo = (xn - x_hi.astype(F32)).astype(BF16)
    w = wr_ref[...]
    w_hi = w.astype(BF16)
    w_lo = (w - w_hi.astype(F32)).astype(BF16)
    hi_terms = _dot(x_hi, jnp.concatenate([w_hi, w_lo], axis=1))
    logits = hi_terms[:, :LANES] + _dot(x_lo, w_hi) + hi_terms[:, LANES:] + br_ref[...]
    lane = lax.broadcasted_iota(jnp.int32, logits.shape, 1).astype(F32)

    def rmax(mask):
        return jnp.max(jnp.where(mask, logits, NEG_BIG), axis=-1, keepdims=True)

    def first_at(mask, val):
        return jnp.min(jnp.where(mask & (logits == val), lane, 1e9), axis=-1, keepdims=True)

    gmask = (lane >= N_EXPERTS) & (lane < N_EXPERTS + N_EXPERT_GROUPS)
    gmax = rmax(gmask)
    p_g = 1.0 / jnp.sum(jnp.where(gmask, jnp.exp(logits - gmax), 0.0), axis=-1, keepdims=True)
    g_star = first_at(gmask, gmax) - N_EXPERTS
    lo = g_star * EXPERTS_PER_GROUP
    emask = (lane >= lo) & (lane < lo + EXPERTS_PER_GROUP)
    v1 = rmax(emask)
    i1 = first_at(emask, v1)
    mask2 = emask & (lane != i1)
    v2 = rmax(mask2)
    i2 = first_at(mask2, v2)
    e2 = jnp.exp(v2 - v1)
    w1 = p_g / (1.0 + e2)
    w2 = p_g * e2 / (1.0 + e2)
    gates_ref[...] = (jnp.where(lane == i1, w1, 0.0) + jnp.where(lane == i2, w2, 0.0)
                      + jnp.where(lane == N_EXPERTS, g_star, 0.0))


def _outproj(ys, ya, x, wos, woa, aw, fw, wr, br, tm):
    t, d = x.shape
    dm = ys.shape[1]
    rows = lambda i: (i, 0)
    const = lambda i: (0, 0)
    return pl.pallas_call(
        _outproj_kernel,
        grid=(t // tm,),
        in_specs=[
            pl.BlockSpec((tm, dm), rows),
            pl.BlockSpec((tm, dm), rows),
            pl.BlockSpec((tm, d), rows),
            pl.BlockSpec((dm, d), const),
            pl.BlockSpec((dm, d), const),
            pl.BlockSpec((1, dm), const),
            pl.BlockSpec((1, d), const),
            pl.BlockSpec((d, LANES), const),
            pl.BlockSpec((1, LANES), const),
        ],
        out_specs=[
            pl.BlockSpec((tm, d), rows),
            pl.BlockSpec((tm, d), rows),
            pl.BlockSpec((tm, LANES), rows),
        ],
        out_shape=[
            jax.ShapeDtypeStruct((t, d), F32),
            jax.ShapeDtypeStruct((t, d), BF16),
            jax.ShapeDtypeStruct((t, LANES), F32),
        ],
        compiler_params=_cparams(("parallel",)),
        name="outproj",
    )(ys, ya, x, wos, woa, aw, fw, wr, br)


MOE_BLOCK = 1024
MOE_CHUNK = 128
MOE_ROWS = MOE_BLOCK + N_EXPERT_GROUPS * MOE_CHUNK
GROUP_LANE = N_EXPERTS


def _group_runs(count):
    nch = jnp.floor((count + (MOE_CHUNK - 1)) * (1.0 / MOE_CHUNK))
    return nch, nch * MOE_CHUNK


def _moe_dispatch_kernel(xn_ref, route_ref, before_ref, xs_ref, gs_ref, meta_ref):
    pt = xn_ref.shape[0]
    route = route_ref[...]
    rt = route.T
    g_row = rt[GROUP_LANE:GROUP_LANE + 1, :]
    grp = lax.broadcasted_iota(jnp.int32, (8, pt), 0).astype(F32)
    onehot = jnp.where(grp == g_row, 1.0, 0.0)
    rank = _dot(onehot.astype(BF16), before_ref[...])
    nch, padded = _group_runs(jnp.sum(onehot, axis=1, keepdims=True))
    gi = lax.broadcasted_iota(jnp.int32, (8, 1), 0)
    start = jnp.zeros((8, 1), F32)
    for g in range(N_EXPERT_GROUPS - 1):
        start = start + jnp.where(gi > g, padded[g:g + 1, :], 0.0)
    pos = jnp.sum(onehot * (start + rank), axis=0, keepdims=True)
    dest = lax.broadcasted_iota(jnp.int32, (MOE_ROWS, pt), 0).astype(F32)
    perm = jnp.where(dest == pos, 1.0, 0.0).astype(BF16)
    g_hi = route.astype(BF16)
    g_lo = (route - g_hi.astype(F32)).astype(BF16)
    moved = _dot(perm, jnp.concatenate([xn_ref[...], g_hi, g_lo], axis=1))
    d = xn_ref.shape[1]
    xs_ref[0] = moved[:, :d].astype(BF16)
    gs_ref[0] = moved[:, d:d + LANES] + moved[:, d + LANES:]
    lane = lax.broadcasted_iota(jnp.int32, (8, LANES), 1)
    meta_ref[0] = jnp.where(lane == 0, start, jnp.where(lane == 1, nch, 0.0))


def _moe_expert_kernel(n_sub, n_exp, meta_ref, xs_ref, gs_ref, wg_ref, wu_ref, wd_ref, ys_ref):
    st = pl.program_id(0)
    eb = pl.program_id(1)
    g = (eb * n_exp) // EXPERTS_PER_GROUP

    @pl.when(eb == 0)
    def _():
        ys_ref[...] = jnp.zeros(ys_ref.shape, F32)

    def run(s, first_row, n_rows):
        rows = pl.ds(pl.multiple_of(first_row, MOE_CHUNK), n_rows)
        lane = lax.broadcasted_iota(jnp.int32, (n_rows, LANES), 1)
        xc = xs_ref[s, rows, :]
        gates = gs_ref[s, rows, :]
        y = ys_ref[s, rows, :]
        for k in range(n_exp):
            gate = jnp.sum(jnp.where(lane == eb * n_exp + k, gates, 0.0), axis=-1, keepdims=True)
            hid = _silu(_dot(xc, wg_ref[k])) * _dot(xc, wu_ref[k]) * gate
            y = y + _dot(hid.astype(BF16), wd_ref[k])
        ys_ref[s, rows, :] = y

    for s in range(n_sub):
        base = ((st * n_sub + s) * N_EXPERT_GROUPS + g) * 2
        start = meta_ref[base]
        n_chunks = meta_ref[base + 1]

        odd = n_chunks % 2
        n_two = jnp.maximum(n_chunks // 2 - odd, 0)

        def two_chunks(c, carry, s=s, start=start):
            run(s, start + c * (2 * MOE_CHUNK), 2 * MOE_CHUNK)
            return carry
        lax.fori_loop(0, n_two, two_chunks, 0)

        @pl.when((odd == 1) & (n_chunks >= 3))
        def _(s=s, start=start, n_two=n_two):
            run(s, start + n_two * (2 * MOE_CHUNK), 3 * MOE_CHUNK)

        @pl.when(n_chunks == 1)
        def _(s=s, start=start):
            run(s, start, MOE_CHUNK)


def _moe_combine_kernel(ys_ref, route_ref, x1_ref, before_ref, o_ref):
    pt = x1_ref.shape[0]
    route = route_ref[...]
    lane = lax.broadcasted_iota(jnp.int32, (pt, LANES), 1).astype(F32)
    onehot = jnp.where(lane == route[:, GROUP_LANE:GROUP_LANE + 1], 1.0, 0.0)
    rank = _dot(before_ref[...], onehot.astype(BF16))
    _, padded = _group_runs(jnp.sum(onehot, axis=0, keepdims=True))
    lane1 = lax.broadcasted_iota(jnp.int32, (1, LANES), 1)
    start = jnp.zeros((1, LANES), F32)
    for g in range(N_EXPERT_GROUPS - 1):
        start = start + jnp.where(lane1 > g, padded[:, g:g + 1], 0.0)
    pos = jnp.sum(onehot * (start + rank), axis=-1, keepdims=True)
    src = lax.broadcasted_iota(jnp.int32, (pt, MOE_ROWS), 1).astype(F32)
    perm_t = jnp.where(src == pos, 1.0, 0.0).astype(BF16)
    o_ref[...] = x1_ref[...] + _dot(perm_t, ys_ref[0].astype(BF16))


def _moe(xn, route, x1, wg, wu, wd, n_sub):
    t, d = x1.shape
    n_e, _, f = wg.shape
    pt = MOE_BLOCK
    assert t % (pt * n_sub) == 0
    nb = t // pt
    idx = jnp.arange(pt)
    before_row = (idx[:, None] < idx[None, :]).astype(BF16)
    before_col = (idx[None, :] < idx[:, None]).astype(BF16)
    rows = lambda i: (i, 0)
    const = lambda i: (0, 0)
    xs, gs, meta = pl.pallas_call(
        _moe_dispatch_kernel,
        grid=(nb,),
        in_specs=[
            pl.BlockSpec((pt, d), rows),
            pl.BlockSpec((pt, LANES), rows),
            pl.BlockSpec((pt, pt), const),
        ],
        out_specs=[
            pl.BlockSpec((1, MOE_ROWS, d), lambda i: (i, 0, 0)),
            pl.BlockSpec((1, MOE_ROWS, LANES), lambda i: (i, 0, 0)),
            pl.BlockSpec((1, 8, LANES), lambda i: (i, 0, 0)),
        ],
        out_shape=[
            jax.ShapeDtypeStruct((nb, MOE_ROWS, d), BF16),
            jax.ShapeDtypeStruct((nb, MOE_ROWS, LANES), F32),
            jax.ShapeDtypeStruct((nb, 8, LANES), F32),
        ],
        compiler_params=_cparams(("parallel",)),
        name="moe_dispatch",
    )(xn, route, before_row)

    meta_i = meta[:, :N_EXPERT_GROUPS, :2].astype(jnp.int32).reshape(-1)
    n_exp = 4
    assert EXPERTS_PER_GROUP % n_exp == 0
    ys = pl.pallas_call(
        functools.partial(_moe_expert_kernel, n_sub, n_exp),
        grid_spec=pltpu.PrefetchScalarGridSpec(
            num_scalar_prefetch=1,
            grid=(nb // n_sub, n_e // n_exp),
            in_specs=[
                pl.BlockSpec((n_sub, MOE_ROWS, d), lambda i, e, m: (i, 0, 0)),
                pl.BlockSpec((n_sub, MOE_ROWS, LANES), lambda i, e, m: (i, 0, 0)),
                pl.BlockSpec((n_exp, d, f), lambda i, e, m: (e, 0, 0)),
                pl.BlockSpec((n_exp, d, f), lambda i, e, m: (e, 0, 0)),
                pl.BlockSpec((n_exp, f, d), lambda i, e, m: (e, 0, 0)),
            ],
            out_specs=pl.BlockSpec((n_sub, MOE_ROWS, d), lambda i, e, m: (i, 0, 0)),
        ),
        out_shape=jax.ShapeDtypeStruct((nb, MOE_ROWS, d), F32),
        compiler_params=_cparams(("parallel", "arbitrary")),
        name="moe_experts",
    )(meta_i, xs, gs, wg, wu, wd)

    return pl.pallas_call(
        _moe_combine_kernel,
        grid=(nb,),
        in_specs=[
            pl.BlockSpec((1, MOE_ROWS, d), lambda i: (i, 0, 0)),
            pl.BlockSpec((pt, LANES), rows),
            pl.BlockSpec((pt, d), rows),
            pl.BlockSpec((pt, pt), const),
        ],
        out_specs=pl.BlockSpec((pt, d), rows),
        out_shape=jax.ShapeDtypeStruct((t, d), F32),
        compiler_params=_cparams(("parallel",)),
        name="moe_combine",
    )(ys, route, x1, before_col)


def _pad_lanes(v, offset=0):
    out = jnp.zeros((1, LANES), F32)
    return out.at[0, offset:offset + v.shape[0]].set(v.astype(F32))


def kernel(x, norm_mix_w, w_in, conv_w, conv_b, dt_bias, a_log, d_skip, ssm_norm_w, q_norm_w,
           k_norm_w, fox_f_bias, att_out_norm_w, w_out, norm_ffn_w, w_router_group,
           b_router_group, w_router_expert, b_router_expert, w_expert_gate, w_expert_up,
           w_expert_down):
    batch, seq, d = x.shape
    depth = w_in.shape[0]
    d_ssm = ssm_norm_w.shape[1]
    d_att = att_out_norm_w.shape[1]
    h_ssm = dt_bias.shape[1]
    h_att = fox_f_bias.shape[1]
    conv_dim = conv_w.shape[2]
    assert h_ssm == 16 and h_att == 16 and d_ssm // h_ssm == HEAD_DIM and d_att // h_att == HEAD_DIM
    assert conv_dim == d_ssm + 2 * SSM_GROUPS * D_STATE
    off_xbc = d_ssm
    off_dt = off_xbc + conv_dim
    off_q = off_dt + h_ssm
    off_k = off_q + d_att
    off_v = off_k + d_att
    off_f = off_v + d_att
    t = batch * seq
    tm_in = min(512, t)
    tm_out = min(512, t)
    moe_sub = 2 if t % (2 * MOE_BLOCK) == 0 else 1
    tq = min(512, seq)
    tk = tq

    head_of_chan = jnp.arange(d_ssm) // HEAD_DIM
    expand = (jnp.arange(LANES)[:, None] == head_of_chan[None, :]).astype(BF16)

    xf = x.reshape(t, d)
    for l in range(depth):
        wl = w_in[l]
        w_main = jnp.concatenate(
            [wl[:, :off_xbc], wl[:, off_q:off_k], wl[:, off_k:off_v], wl[:, off_v:off_f],
             wl[:, off_xbc:off_dt]], axis=1).astype(BF16)
        p_q, p_k, p_v, p_xbc = d_ssm, d_ssm + d_att, d_ssm + 2 * d_att, d_ssm + 3 * d_att
        w_small = jnp.zeros((d, LANES), F32)
        w_small = w_small.at[:, 0:h_ssm].set(wl[:, off_dt:off_q])
        w_small = w_small.at[:, 16:16 + h_att].set(wl[:, off_f:off_f + h_att]).astype(BF16)
        proj, small = _inproj(xf, norm_mix_w[l][None, :], w_main, w_small, tm_in, 512)

        bias_row = _pad_lanes(dt_bias[l]) + _pad_lanes(fox_f_bias[l], 16)
        a_row = _pad_lanes(-jnp.exp(a_log[l].astype(F32)))
        dskip_row = jnp.repeat(d_skip[l].astype(F32), HEAD_DIM)[None, :]
        y_ssm, c, ct = _ssd(proj, 0, p_xbc, conv_dim, d_ssm, small, conv_w[l].astype(F32),
                        conv_b[l].astype(F32)[None, :], bias_row, a_row, dskip_row,
                        ssm_norm_w[l][None, :], expand, batch, seq)

        qw_row = jnp.tile(q_norm_w[l].astype(F32), 2)[None, :]
        kw_row = jnp.tile(k_norm_w[l].astype(F32), 2)[None, :]
        y_att = _attn(proj, p_q, p_k, p_v, d_att, c, ct, qw_row, kw_row, batch, seq, tq, tk, 4)

        wr = jnp.zeros((d, LANES), F32)
        wr = wr.at[:, :N_EXPERTS].set(w_router_expert[l])
        wr = wr.at[:, N_EXPERTS:N_EXPERTS + N_EXPERT_GROUPS].set(w_router_group[l])
        br = _pad_lanes(b_router_expert[l]) + _pad_lanes(b_router_group[l], N_EXPERTS)
        wo = w_out[l].astype(BF16)
        x1, xn, gates = _outproj(y_ssm, y_att, xf, wo[:d_ssm], wo[d_ssm:],
                                 att_out_norm_w[l][None, :], norm_ffn_w[l][None, :], wr, br, tm_out)

        xf = _moe(xn, gates, x1, w_expert_gate[l].astype(BF16), w_expert_up[l].astype(BF16),
                  w_expert_down[l].astype(BF16), moe_sub)
    return xf.reshape(batch, seq, d)
```

```python
import functools

import jax
import jax.numpy as jnp
import numpy as np
from jax import lax
from jax.experimental import pallas as pl
from jax.experimental.pallas import tpu as pltpu

F32 = jnp.float32
BF16 = jnp.bfloat16

HEAD_DIM = 64
D_STATE = 128
SSM_GROUPS = 2
CONV_K = 4
SSD_CHUNK = 128
N_EXPERT_GROUPS = 4
EXPERTS_PER_GROUP = 8
N_EXPERTS = N_EXPERT_GROUPS * EXPERTS_PER_GROUP
EPS = 1e-6
LANES = 128
NEG_BIG = -1e30
LOG2E = 1.4426950408889634
VMEM_LIMIT = 56 * 1024 * 1024


def _cparams(sem):
    return pltpu.CompilerParams(dimension_semantics=sem, vmem_limit_bytes=VMEM_LIMIT)


def _split3(a):
    a1 = a.astype(BF16)
    r1 = a - a1.astype(F32)
    a2 = r1.astype(BF16)
    a3 = (r1 - a2.astype(F32)).astype(BF16)
    return a1, a2, a3


def _dot(a, b):
    return jnp.dot(a, b, preferred_element_type=F32)


def _dot2(a_f32, b_bf16):
    a1 = a_f32.astype(BF16)
    a2 = (a_f32 - a1.astype(F32)).astype(BF16)
    return _dot(a1, b_bf16) + _dot(a2, b_bf16)


def _dot3_tn(b_bf16_t, a_f32):
    a1, a2, a3 = _split3(a_f32)
    return _dot(b_bf16_t, a1) + _dot(b_bf16_t, a2) + _dot(b_bf16_t, a3)


def _silu(x):
    return x / (1.0 + jnp.exp(-x))


def _inproj_kernel(tn, x_ref, nw_ref, w_ref, ws_ref, proj_ref, small_ref):
    x = x_ref[...]
    ms = jnp.mean(x * x, axis=-1, keepdims=True)
    hb = (x * lax.rsqrt(ms + EPS) * nw_ref[...]).astype(BF16)
    small_ref[...] = _dot(hb, ws_ref[...])
    for j in range(w_ref.shape[1] // tn):
        cols = slice(j * tn, (j + 1) * tn)
        proj_ref[:, cols] = _dot(hb, w_ref[:, cols]).astype(BF16)


def _inproj(x2, nw, w_main, w_small, tm, tn):
    t, d = x2.shape
    n = w_main.shape[1]
    assert n % tn == 0 and t % tm == 0
    return pl.pallas_call(
        functools.partial(_inproj_kernel, tn),
        grid=(t // tm,),
        in_specs=[
            pl.BlockSpec((tm, d), lambda i: (i, 0)),
            pl.BlockSpec((1, d), lambda i: (0, 0)),
            pl.BlockSpec((d, n), lambda i: (0, 0)),
            pl.BlockSpec((d, LANES), lambda i: (0, 0)),
        ],
        out_specs=[
            pl.BlockSpec((tm, n), lambda i: (i, 0)),
            pl.BlockSpec((tm, LANES), lambda i: (i, 0)),
        ],
        out_shape=[
            jax.ShapeDtypeStruct((t, n), BF16),
            jax.ShapeDtypeStruct((t, LANES), F32),
        ],
        compiler_params=_cparams(("parallel",)),
        name="inproj",
    )(x2, nw, w_main, w_small)


def _ssd_kernel(d_ssm, n_xbc, n_seq, *refs):
    ext_ref, state_ref, carry_ref = refs[-3:]
    L = SSD_CHUNK

    @pl.when(pl.program_id(1) == 0)
    def _():
        ext_ref[:, L:2 * L, :] = jnp.zeros((n_seq, L, ext_ref.shape[2]), BF16)
        state_ref[...] = jnp.zeros(state_ref.shape, F32)
        carry_ref[...] = jnp.zeros(carry_ref.shape, F32)

    for bb in range(n_seq):
        _ssd_chunk(bb, d_ssm, n_xbc, refs)


def _ssd_chunk(bb, d_ssm, n_xbc, refs):
    xbc_refs = refs[:n_xbc]
    (z_ref, small_ref, convw_ref, convb_ref, bias_ref, arow_ref, dskip_ref, nw_ref, expand_ref,
     shift_ref, y_ref, c_ref, ct_ref, ext_ref, state_ref, carry_ref) = refs[n_xbc:]
    L = SSD_CHUNK
    n_pairs = d_ssm // LANES
    gw = d_ssm // SSM_GROUPS

    u = jnp.concatenate([r[bb] for r in xbc_refs], axis=1)
    ext_ref[bb, 0:L, :] = ext_ref[bb, L:2 * L, :]
    ext_ref[bb, L:2 * L, :] = u
    n_delay = CONV_K - 1
    delayed = _dot(shift_ref[...], ext_ref[bb])
    acc = convb_ref[...] + convw_ref[n_delay:CONV_K, :] * u.astype(F32)
    for k in range(n_delay):
        acc = acc + convw_ref[k:k + 1, :] * delayed[k * L:(k + 1) * L, :]
    xbc = _silu(acc)
    xs = xbc[:, :d_ssm]
    b_all = xbc[:, d_ssm:d_ssm + SSM_GROUPS * D_STATE].astype(BF16)
    c_all = xbc[:, d_ssm + SSM_GROUPS * D_STATE:].astype(BF16)

    row = lax.broadcasted_iota(jnp.int32, (L, L), 0)
    col = lax.broadcasted_iota(jnp.int32, (L, L), 1)
    causal = row >= col
    tril = jnp.where(causal, 1.0, 0.0).astype(BF16)
    pre = small_ref[bb] + bias_ref[...]
    sp = jnp.maximum(pre, 0.0) + jnp.log(1.0 + jnp.exp(-jnp.abs(pre)))
    lane = lax.broadcasted_iota(jnp.int32, (L, LANES), 1)
    is_dt = lane < 16
    dt = jnp.where(is_dt, sp, 0.0)
    logf = jnp.where((lane >= 16) & (lane < 32), pre - sp, 0.0)
    a = dt * arow_ref[...]
    a_and_f = a + logf
    cum_all = _dot3_tn(tril, a_and_f)
    cum = jnp.where(is_dt, cum_all, 0.0)
    cum_t = cum.T
    c_run = cum_all + carry_ref[bb, 0:1, :]
    c_ref[bb] = c_run
    ct_ref[bb] = c_run.T
    carry_ref[bb, 0:1, :] = c_run[L - 1:L, :]

    expand = expand_ref[...]
    dt_x = _dot2(dt, expand)
    ecum_x = _dot2(jnp.exp(cum), expand)
    dte_x = _dot2(jnp.exp(cum[L - 1:L, :] - cum), expand)
    xdt = xs * dt_x
    xdt_b = xdt.astype(BF16)
    w_b = (xdt * dte_x).astype(BF16)

    y_parts = []
    for g in range(SSM_GROUPS):
        bg = b_all[:, g * D_STATE:(g + 1) * D_STATE]
        cg = c_all[:, g * D_STATE:(g + 1) * D_STATE]
        cb = lax.dot_general(cg, bg, (((1,), (1,)), ((), ())), preferred_element_type=F32)
        s_old = state_ref[bb, g]
        y_off = _dot(cg, s_old.astype(BF16)) * ecum_x[:, g * gw:(g + 1) * gw]
        upd = lax.dot_general(bg, w_b[:, g * gw:(g + 1) * gw], (((0,), (0,)), ((), ())),
                              preferred_element_type=F32)
        state_ref[bb, g] = s_old * ecum_x[L - 1:L, g * gw:(g + 1) * gw] + upd
        pairs_per_group = n_pairs // SSM_GROUPS
        for pp in range(pairs_per_group):
            p = g * pairs_per_group + pp
            xp = xdt_b[:, p * LANES:(p + 1) * LANES]
            ys = []
            for hh in range(2):
                h = 2 * p + hh
                seg = cum[:, h:h + 1] - cum_t[h:h + 1, :]
                decay = jnp.exp(jnp.where(causal, seg, NEG_BIG))
                ys.append(_dot((cb * decay).astype(BF16), xp))
            first = lax.broadcasted_iota(jnp.int32, (L, LANES), 1) < HEAD_DIM
            y_parts.append(jnp.where(first, ys[0], ys[1])
                           + y_off[:, pp * LANES:(pp + 1) * LANES])
    y = jnp.concatenate(y_parts, axis=1) + xs * dskip_ref[...]
    gated = y * _silu(z_ref[bb].astype(F32))
    ms = jnp.mean(gated * gated, axis=-1, keepdims=True)
    y_ref[bb] = (gated * lax.rsqrt(ms + EPS) * nw_ref[...]).astype(BF16)


def _ssd(proj, z_off, xbc_off, cdim, d_ssm, small, convw, convb, bias_row, a_row, dskip_row, nw,
         expand, batch, seq):
    t = proj.shape[0]
    L = SSD_CHUNK
    nc = seq // L
    wx = 512
    assert cdim % wx == 0 and xbc_off % wx == 0 and z_off % d_ssm == 0
    n_xbc = cdim // wx
    n_seq = max(n for n in (4, 2, 1) if batch % n == 0)
    proj3 = proj.reshape(batch, seq, proj.shape[1])
    small3 = small.reshape(batch, seq, LANES)
    rowmap = lambda b, c: (b, c, 0)
    const = lambda b, c: (0, 0)
    n_delay = CONV_K - 1
    r = jnp.arange(n_delay * L)
    target = L + r % L - (n_delay - r // L)
    shift = (jnp.arange(2 * L)[None, :] == target[:, None]).astype(BF16)
    xbc_specs = [pl.BlockSpec((n_seq, L, wx), lambda b, c, o=xbc_off // wx + i: (b, c, o))
                 for i in range(n_xbc)]
    y, c_out, ct = pl.pallas_call(
        functools.partial(_ssd_kernel, d_ssm, n_xbc, n_seq),
        grid=(batch // n_seq, nc),
        in_specs=xbc_specs + [
            pl.BlockSpec((n_seq, L, d_ssm), lambda b, c: (b, c, z_off // d_ssm)),
            pl.BlockSpec((n_seq, L, LANES), rowmap),
            pl.BlockSpec((CONV_K, cdim), const),
            pl.BlockSpec((1, cdim), const),
            pl.BlockSpec((1, LANES), const),
            pl.BlockSpec((1, LANES), const),
            pl.BlockSpec((1, d_ssm), const),
            pl.BlockSpec((1, d_ssm), const),
            pl.BlockSpec((LANES, d_ssm), const),
            pl.BlockSpec((n_delay * L, 2 * L), const),
        ],
        out_specs=[
            pl.BlockSpec((n_seq, L, d_ssm), rowmap),
            pl.BlockSpec((n_seq, L, LANES), rowmap),
            pl.BlockSpec((n_seq, LANES, L), lambda b, c: (b, 0, c)),
        ],
        out_shape=[
            jax.ShapeDtypeStruct((batch, seq, d_ssm), BF16),
            jax.ShapeDtypeStruct((batch, seq, LANES), F32),
            jax.ShapeDtypeStruct((batch, LANES, seq), F32),
        ],
        scratch_shapes=[
            pltpu.VMEM((n_seq, 2 * L, cdim), BF16),
            pltpu.VMEM((n_seq, SSM_GROUPS, D_STATE, d_ssm // SSM_GROUPS), F32),
            pltpu.VMEM((n_seq, 8, LANES), F32),
        ],
        compiler_params=_cparams(("parallel", "arbitrary")),
        name="ssd",
    )(*([proj3] * n_xbc), proj3, small3, convw, convb, bias_row, a_row, dskip_row, nw, expand, shift)
    return y.reshape(t, d_ssm), c_out.reshape(t, LANES), ct


def _headnorm(x, w_row, same_head):
    ms = _dot((x * x).astype(BF16), same_head) * (1.0 / HEAD_DIM)
    return x * lax.rsqrt(ms + EPS) * w_row


def _place(parts, moves, rr, cc):
    out = None
    for i, part in enumerate(parts):
        sel = None
        for src, dst in moves:
            one = jnp.where((rr == src) & (cc == dst + i), 1.0, 0.0)
            sel = one if sel is None else sel + one
        term = _dot(part, sel.astype(BF16))
        out = term if out is None else out + term
    return out


def _attn_kernel(tq, tk, npair, q_ref, k_ref, v_ref, c_ref, ct_ref, qw_ref, kw_ref, o_ref,
                 kx_ref, vx_ref, qx_ref, sa_ref, sb_ref, ta_ref, tb_ref, m_ref, l_ref, acc_ref):
    grp = pl.program_id(1)
    qi = pl.program_id(2)
    seq = k_ref.shape[1]
    lane = lax.broadcasted_iota(jnp.int32, (1, LANES), 1)
    rr = lax.broadcasted_iota(jnp.int32, (LANES, LANES), 0)
    cc = lax.broadcasted_iota(jnp.int32, (LANES, LANES), 1)
    same_head = jnp.where((rr < HEAD_DIM) == (cc < HEAD_DIM), 1.0, 0.0).astype(BF16)
    pairs = range(npair)
    bases = [16 + 2 * (grp * npair + p) for p in pairs]

    @pl.when(qi == 0)
    def _():
        k_const = jnp.where((lane >= 6) & (lane < 9), 1.0, 0.0)

        def body(r, carry):
            rows = pl.ds(pl.multiple_of(r * tq, tq), tq)
            parts = _split3(c_ref[0, rows, :] * LOG2E)
            for p in pairs:
                lanes = slice(p * LANES, (p + 1) * LANES)
                kn = _headnorm(k_ref[0, rows, lanes].astype(F32), kw_ref[...], same_head)
                kext = _place(parts, [(bases[p], 0), (bases[p] + 1, 3)], rr, cc) + k_const
                kx_ref[p, rows, 0:LANES] = kn.astype(BF16)
                kx_ref[p, rows, LANES:2 * LANES] = kext.astype(BF16)
                v = v_ref[0, rows, lanes].astype(F32)
                vt = v.T.astype(BF16)
                vx_ref[p, 0, :, rows] = vt[0:HEAD_DIM]
                vx_ref[p, 1, :, rows] = vt[HEAD_DIM:LANES]
            return carry
        lax.fori_loop(0, seq // tq, body, 0)

    scale = (1.0 / np.sqrt(HEAD_DIM)) * LOG2E
    sub = lax.broadcasted_iota(jnp.int32, (LANES, tq), 0)
    for p in pairs:
        qn = _headnorm(q_ref[0, :, p * LANES:(p + 1) * LANES].astype(F32), qw_ref[...],
                       same_head) * scale
        qt = qn.T
        for hh in range(2):
            c_hi, c_mid, c_lo = _split3(ct_ref[0, pl.ds(bases[p] + hh, 1), :] * LOG2E)
            minus = (sub >= 3 * hh) & (sub < 3 * hh + 3)
            ext = jnp.where(sub == 6, c_hi.astype(F32), jnp.where(sub == 7, c_mid.astype(F32),
                  jnp.where(sub == 8, c_lo.astype(F32), jnp.where(minus, -1.0, 0.0))))
            own = (sub < HEAD_DIM) if hh == 0 else (sub >= HEAD_DIM)
            qx_ref[p, 0:LANES, hh * tq:(hh + 1) * tq] = jnp.where(own, qt, 0.0).astype(BF16)
            qx_ref[p, LANES:2 * LANES, hh * tq:(hh + 1) * tq] = ext.astype(BF16)
        m_ref[p] = jnp.full((1, 2 * tq), NEG_BIG, F32)
        l_ref[p] = jnp.zeros((1, 2 * tq), F32)
        acc_ref[p] = jnp.zeros((2, HEAD_DIM, tq), F32)

    assert tk == tq
    n_full = qi
    half = tq // 2
    key_h = lax.broadcasted_iota(jnp.int32, (half, 2 * tq), 0)
    qry_h = lax.broadcasted_iota(jnp.int32, (half, 2 * tq), 1)
    keep_top = jnp.where(qry_h >= tq, qry_h - tq, qry_h) >= key_h
    key_b = lax.broadcasted_iota(jnp.int32, (half, tq), 0)
    qry_b = lax.broadcasted_iota(jnp.int32, (half, tq), 1)
    keep_bot = jnp.where(qry_b >= half, qry_b - half, qry_b) >= key_b

    def scores(j, buf):
        s_ref, top_ref = buf
        rows = pl.ds(pl.multiple_of(j * tk, tk), tk)
        for p in pairs:
            s = _dot(kx_ref[p, rows, :], qx_ref[p])
            s_ref[p] = s
            top_ref[p] = jnp.max(s, axis=0, keepdims=True)

    def late(x):
        return jnp.concatenate([x[:, half:tq], x[:, tq + half:2 * tq]], axis=1)

    def widen(x, fill):
        pad = jnp.full((x.shape[0], half), fill, x.dtype)
        return jnp.concatenate([pad, x[:, :half], pad, x[:, half:]], axis=1)

    def diag_scores(buf):
        s_ref, _ = buf
        base = pl.multiple_of(qi * tq, tq)
        for p in pairs:
            s_top = _dot(kx_ref[p, pl.ds(base, half), :], qx_ref[p])
            s_bot = _dot(kx_ref[p, pl.ds(base + half, half), :], late(qx_ref[p]))
            s_ref[p, 0:half, :] = jnp.where(keep_top, s_top, NEG_BIG)
            s_ref[p, half:tk, 0:tq] = jnp.where(keep_bot, s_bot, NEG_BIG)

    def diag_update(buf):
        s_ref, _ = buf
        base = pl.multiple_of(qi * tq, tq)
        for p in pairs:
            s_top = s_ref[p, 0:half, :]
            s_bot = s_ref[p, half:tk, 0:tq]
            top = jnp.maximum(jnp.max(s_top, axis=0, keepdims=True),
                              widen(jnp.max(s_bot, axis=0, keepdims=True), NEG_BIG))
            m = m_ref[p]
            m_new = jnp.maximum(m, top)
            alpha = jnp.exp2(m - m_new)
            p_top = jnp.exp2(s_top - m_new)
            p_bot = jnp.exp2(s_bot - late(m_new))
            l_ref[p] = (alpha * l_ref[p] + jnp.sum(p_top, axis=0, keepdims=True)
                        + widen(jnp.sum(p_bot, axis=0, keepdims=True), 0.0))
            w_top = p_top.astype(BF16)
            w_bot = p_bot.astype(BF16)
            for hh in range(2):
                first = _dot(vx_ref[p, hh, :, pl.ds(base, half)], w_top[:, hh * tq:(hh + 1) * tq])
                second = _dot(vx_ref[p, hh, :, pl.ds(base + half, half)],
                              w_bot[:, hh * half:(hh + 1) * half])
                second = jnp.concatenate([jnp.zeros((HEAD_DIM, half), F32), second], axis=1)
                acc_ref[p, hh] = alpha[:, hh * tq:(hh + 1) * tq] * acc_ref[p, hh] + first + second
            m_ref[p] = m_new

    def softmax_pv(j, buf):
        s_ref, top_ref = buf
        cols = pl.ds(pl.multiple_of(j * tk, tk), tk)
        for p in pairs:
            s = s_ref[p]
            top = top_ref[p]
            m = m_ref[p]
            m_new = jnp.maximum(m, top)
            alpha = jnp.exp2(m - m_new)
            pw = jnp.exp2(s - m_new)
            l_ref[p] = alpha * l_ref[p] + jnp.sum(pw, axis=0, keepdims=True)
            pb = pw.astype(BF16)
            acc_ref[p, 0] = alpha[:, :tq] * acc_ref[p, 0] + _dot(vx_ref[p, 0, :, cols], pb[:, :tq])
            acc_ref[p, 1] = alpha[:, tq:] * acc_ref[p, 1] + _dot(vx_ref[p, 1, :, cols], pb[:, tq:])
            m_ref[p] = m_new

    buf_a = (sa_ref, ta_ref)
    buf_b = (sb_ref, tb_ref)

    @pl.when(n_full >= 1)
    def _():
        scores(0, buf_a)

    def two_blocks(i, carry):
        j = 2 * i
        scores(j + 1, buf_b)
        softmax_pv(j, buf_a)
        scores(j + 2, buf_a)
        softmax_pv(j + 1, buf_b)
        return carry
    lax.fori_loop(0, jnp.maximum(n_full - 1, 0) // 2, two_blocks, 0)

    @pl.when(n_full == 0)
    def _():
        diag_scores(buf_a)
        diag_update(buf_a)

    @pl.when(n_full % 2 == 1)
    def _():
        diag_scores(buf_b)
        softmax_pv(n_full - 1, buf_a)
        diag_update(buf_b)

    @pl.when((n_full % 2 == 0) & (n_full > 0))
    def _():
        scores(n_full - 1, buf_b)
        softmax_pv(n_full - 2, buf_a)
        diag_scores(buf_a)
        softmax_pv(n_full - 1, buf_b)
        diag_update(buf_a)

    for p in pairs:
        out_t = jnp.concatenate(
            [acc_ref[p, 0] / l_ref[p, :, 0:tq], acc_ref[p, 1] / l_ref[p, :, tq:2 * tq]], axis=0)
        o_ref[0, :, p * LANES:(p + 1) * LANES] = out_t.T.astype(BF16)


def _attn(proj, q_off, k_off, v_off, d_att, c, ct, qw_row, kw_row, batch, seq, tq, tk, npair):
    width = npair * LANES
    assert d_att % width == 0 and seq % tq == 0 and seq % tk == 0 and tk % tq == 0
    assert q_off % width == 0 and k_off % width == 0 and v_off % width == 0
    qo, ko, vo = q_off // width, k_off // width, v_off // width
    q3 = k3 = v3 = proj.reshape(batch, seq, proj.shape[-1])
    c3 = c.reshape(batch, seq, LANES)
    return pl.pallas_call(
        functools.partial(_attn_kernel, tq, tk, npair),
        grid=(batch, d_att // width, seq // tq),
        in_specs=[
            pl.BlockSpec((1, tq, width), lambda b, h, i: (b, i, qo + h)),
            pl.BlockSpec((1, seq, width), lambda b, h, i: (b, 0, ko + h), pipeline_mode=pl.Buffered(1)),
            pl.BlockSpec((1, seq, width), lambda b, h, i: (b, 0, vo + h), pipeline_mode=pl.Buffered(1)),
            pl.BlockSpec((1, seq, LANES), lambda b, h, i: (b, 0, 0), pipeline_mode=pl.Buffered(1)),
            pl.BlockSpec((1, LANES, tq), lambda b, h, i: (b, 0, i)),
            pl.BlockSpec((1, LANES), lambda b, h, i: (0, 0)),
            pl.BlockSpec((1, LANES), lambda b, h, i: (0, 0)),
        ],
        out_specs=pl.BlockSpec((1, tq, width), lambda b, h, i: (b, i, h)),
        out_shape=jax.ShapeDtypeStruct((batch, seq, d_att), BF16),
        scratch_shapes=[
            pltpu.VMEM((npair, seq, 2 * LANES), BF16),
            pltpu.VMEM((npair, 2, HEAD_DIM, seq), BF16),
            pltpu.VMEM((npair, 2 * LANES, 2 * tq), BF16),
            pltpu.VMEM((npair, tk, 2 * tq), F32),
            pltpu.VMEM((npair, tk, 2 * tq), F32),
            pltpu.VMEM((npair, 1, 2 * tq), F32),
            pltpu.VMEM((npair, 1, 2 * tq), F32),
            pltpu.VMEM((npair, 1, 2 * tq), F32),
            pltpu.VMEM((npair, 1, 2 * tq), F32),
            pltpu.VMEM((npair, 2, HEAD_DIM, tq), F32),
        ],
        compiler_params=_cparams(("parallel", "parallel", "arbitrary")),
        name="attn",
    )(q3, k3, v3, c3, ct, qw_row, kw_row).reshape(batch * seq, d_att)


def _outproj_kernel(ys_ref, ya_ref, x_ref, wos_ref, woa_ref, aw_ref, fw_ref, wr_ref, br_ref,
                    x1_ref, xn_ref, gates_ref):
    ya = ya_ref[...].astype(F32)
    ya = ya * lax.rsqrt(jnp.mean(ya * ya, axis=-1, keepdims=True) + EPS) * aw_ref[...]
    x1 = x_ref[...] + _dot(ys_ref[...], wos_ref[...]) + _dot(ya.astype(BF16), woa_ref[...])
    x1_ref[...] = x1
    xn = x1 * lax.rsqrt(jnp.mean(x1 * x1, axis=-1, keepdims=True) + EPS) * fw_ref[...]
    xn_ref[...] = xn.astype(BF16)

    x_hi = xn.astype(BF16)
    x_lo = (xn - x_hi.astype(F32)).astype(BF16)
    w = wr_ref[...]
    w_hi = w.astype(BF16)
    w_lo = (w - w_hi.astype(F32)).astype(BF16)
    hi_terms = _dot(x_hi, jnp.concatenate([w_hi, w_lo], axis=1))
    logits = hi_terms[:, :LANES] + _dot(x_lo, w_hi) + hi_terms[:, LANES:] + br_ref[...]
    lane = lax.broadcasted_iota(jnp.int32, logits.shape, 1).astype(F32)

    def rmax(mask):
        return jnp.max(jnp.where(mask, logits, NEG_BIG), axis=-1, keepdims=True)

    def first_at(mask, val):
        return jnp.min(jnp.where(mask & (logits == val), lane, 1e9), axis=-1, keepdims=True)

    gmask = (lane >= N_EXPERTS) & (lane < N_EXPERTS + N_EXPERT_GROUPS)
    gmax = rmax(gmask)
    p_g = 1.0 / jnp.sum(jnp.where(gmask, jnp.exp(logits - gmax), 0.0), axis=-1, keepdims=True)
    g_star = first_at(gmask, gmax) - N_EXPERTS
    lo = g_star * EXPERTS_PER_GROUP
    emask = (lane >= lo) & (lane < lo + EXPERTS_PER_GROUP)
    v1 = rmax(emask)
    i1 = first_at(emask, v1)
    mask2 = emask & (lane != i1)
    v2 = rmax(mask2)
    i2 = first_at(mask2, v2)
    e2 = jnp.exp(v2 - v1)
    w1 = p_g / (1.0 + e2)
    w2 = p_g * e2 / (1.0 + e2)
    gates_ref[...] = (jnp.where(lane == i1, w1, 0.0) + jnp.where(lane == i2, w2, 0.0)
                      + jnp.where(lane == N_EXPERTS, g_star, 0.0))


def _outproj(ys, ya, x, wos, woa, aw, fw, wr, br, tm):
    t, d = x.shape
    dm = ys.shape[1]
    rows = lambda i: (i, 0)
    const = lambda i: (0, 0)
    return pl.pallas_call(
        _outproj_kernel,
        grid=(t // tm,),
        in_specs=[
            pl.BlockSpec((tm, dm), rows),
            pl.BlockSpec((tm, dm), rows),
            pl.BlockSpec((tm, d), rows),
            pl.BlockSpec((dm, d), const),
            pl.BlockSpec((dm, d), const),
            pl.BlockSpec((1, dm), const),
            pl.BlockSpec((1, d), const),
            pl.BlockSpec((d, LANES), const),
            pl.BlockSpec((1, LANES), const),
        ],
        out_specs=[
            pl.BlockSpec((tm, d), rows),
            pl.BlockSpec((tm, d), rows),
            pl.BlockSpec((tm, LANES), rows),
        ],
        out_shape=[
            jax.ShapeDtypeStruct((t, d), F32),
            jax.ShapeDtypeStruct((t, d), BF16),
            jax.ShapeDtypeStruct((t, LANES), F32),
        ],
        compiler_params=_cparams(("parallel",)),
        name="outproj",
    )(ys, ya, x, wos, woa, aw, fw, wr, br)


MOE_BLOCK = 1024
MOE_CHUNK = 128
MOE_ROWS = MOE_BLOCK + N_EXPERT_GROUPS * MOE_CHUNK
GROUP_LANE = N_EXPERTS


def _group_runs(count):
    nch = jnp.floor((count + (MOE_CHUNK - 1)) * (1.0 / MOE_CHUNK))
    return nch, nch * MOE_CHUNK


def _moe_dispatch_kernel(xn_ref, route_ref, before_ref, xs_ref, gs_ref, meta_ref):
    pt = xn_ref.shape[0]
    route = route_ref[...]
    rt = route.T
    g_row = rt[GROUP_LANE:GROUP_LANE + 1, :]
    grp = lax.broadcasted_iota(jnp.int32, (8, pt), 0).astype(F32)
    onehot = jnp.where(grp == g_row, 1.0, 0.0)
    rank = _dot(onehot.astype(BF16), before_ref[...])
    nch, padded = _group_runs(jnp.sum(onehot, axis=1, keepdims=True))
    gi = lax.broadcasted_iota(jnp.int32, (8, 1), 0)
    start = jnp.zeros((8, 1), F32)
    for g in range(N_EXPERT_GROUPS - 1):
        start = start + jnp.where(gi > g, padded[g:g + 1, :], 0.0)
    pos = jnp.sum(onehot * (start + rank), axis=0, keepdims=True)
    dest = lax.broadcasted_iota(jnp.int32, (MOE_ROWS, pt), 0).astype(F32)
    perm = jnp.where(dest == pos, 1.0, 0.0).astype(BF16)
    g_hi = route.astype(BF16)
    g_lo = (route - g_hi.astype(F32)).astype(BF16)
    moved = _dot(perm, jnp.concatenate([xn_ref[...], g_hi, g_lo], axis=1))
    d = xn_ref.shape[1]
    xs_ref[0] = moved[:, :d].astype(BF16)
    gs_ref[0] = moved[:, d:d + LANES] + moved[:, d + LANES:]
    lane = lax.broadcasted_iota(jnp.int32, (8, LANES), 1)
    meta_ref[0] = jnp.where(lane == 0, start, jnp.where(lane == 1, nch, 0.0))


def _moe_expert_kernel(n_sub, n_exp, meta_ref, xs_ref, gs_ref, wg_ref, wu_ref, wd_ref, ys_ref):
    st = pl.program_id(0)
    eb = pl.program_id(1)
    g = (eb * n_exp) // EXPERTS_PER_GROUP

    @pl.when(eb == 0)
    def _():
        ys_ref[...] = jnp.zeros(ys_ref.shape, F32)

    def run_parts(parts):
        slices = [(s, pl.ds(pl.multiple_of(first, MOE_CHUNK), n)) for s, first, n in parts]
        cat = lambda ref: jnp.concatenate([ref[s, rows, :] for s, rows in slices], axis=0)
        xc, gates, y = cat(xs_ref), cat(gs_ref), cat(ys_ref)
        lane = lax.broadcasted_iota(jnp.int32, gates.shape, 1)
        for k in range(n_exp):
            gate = jnp.sum(jnp.where(lane == eb * n_exp + k, gates, 0.0), axis=-1, keepdims=True)
            hid = _silu(_dot(xc, wg_ref[k])) * _dot(xc, wu_ref[k]) * gate
            y = y + _dot(hid.astype(BF16), wd_ref[k])
        done = 0
        for (s, rows), (_, _, n) in zip(slices, parts):
            ys_ref[s, rows, :] = y[done:done + n]
            done += n

    def run(s, first_row, n_rows):
        run_parts([(s, first_row, n_rows)])

    runs = []
    for s in range(n_sub):
        base = ((st * n_sub + s) * N_EXPERT_GROUPS + g) * 2
        runs.append((meta_ref[base], meta_ref[base + 1]))

    def one_block(s, start, n_chunks):
        odd = n_chunks % 2
        n_two = jnp.maximum(n_chunks // 2 - odd, 0)

        def two_chunks(c, carry):
            run(s, start + c * (2 * MOE_CHUNK), 2 * MOE_CHUNK)
            return carry
        lax.fori_loop(0, n_two, two_chunks, 0)

        @pl.when((odd == 1) & (n_chunks >= 3))
        def _():
            run(s, start + n_two * (2 * MOE_CHUNK), 3 * MOE_CHUNK)

        @pl.when(n_chunks == 1)
        def _():
            run(s, start, MOE_CHUNK)

    if n_sub == 2:
        (st0, n0), (st1, n1) = runs
        usual = (n0 >= 2) & (n0 <= 3) & (n1 >= 2) & (n1 <= 3)
        for a0 in (2, 3):
            for a1 in (2, 3):
                @pl.when((n0 == a0) & (n1 == a1))
                def _(a0=a0, a1=a1):
                    run_parts([(0, st0, a0 * MOE_CHUNK), (1, st1, a1 * MOE_CHUNK)])

        @pl.when(jnp.logical_not(usual))
        def _():
            for s, (start, n_chunks) in enumerate(runs):
                one_block(s, start, n_chunks)
    else:
        for s, (start, n_chunks) in enumerate(runs):
            one_block(s, start, n_chunks)


def _moe_combine_kernel(ys_ref, route_ref, x1_ref, before_ref, o_ref):
    pt = x1_ref.shape[0]
    route = route_ref[...]
    lane = lax.broadcasted_iota(jnp.int32, (pt, LANES), 1).astype(F32)
    onehot = jnp.where(lane == route[:, GROUP_LANE:GROUP_LANE + 1], 1.0, 0.0)
    rank = _dot(before_ref[...], onehot.astype(BF16))
    _, padded = _group_runs(jnp.sum(onehot, axis=0, keepdims=True))
    lane1 = lax.broadcasted_iota(jnp.int32, (1, LANES), 1)
    start = jnp.zeros((1, LANES), F32)
    for g in range(N_EXPERT_GROUPS - 1):
        start = start + jnp.where(lane1 > g, padded[:, g:g + 1], 0.0)
    pos = jnp.sum(onehot * (start + rank), axis=-1, keepdims=True)
    src = lax.broadcasted_iota(jnp.int32, (pt, MOE_ROWS), 1).astype(F32)
    perm_t = jnp.where(src == pos, 1.0, 0.0).astype(BF16)
    o_ref[...] = x1_ref[...] + _dot(perm_t, ys_ref[0].astype(BF16))


def _moe(xn, route, x1, wg, wu, wd, n_sub):
    t, d = x1.shape
    n_e, _, f = wg.shape
    pt = MOE_BLOCK
    assert t % (pt * n_sub) == 0
    nb = t // pt
    idx = jnp.arange(pt)
    before_row = (idx[:, None] < idx[None, :]).astype(BF16)
    before_col = (idx[None, :] < idx[:, None]).astype(BF16)
    rows = lambda i: (i, 0)
    const = lambda i: (0, 0)
    xs, gs, meta = pl.pallas_call(
        _moe_dispatch_kernel,
        grid=(nb,),
        in_specs=[
            pl.BlockSpec((pt, d), rows),
            pl.BlockSpec((pt, LANES), rows),
            pl.BlockSpec((pt, pt), const),
        ],
        out_specs=[
            pl.BlockSpec((1, MOE_ROWS, d), lambda i: (i, 0, 0)),
            pl.BlockSpec((1, MOE_ROWS, LANES), lambda i: (i, 0, 0)),
            pl.BlockSpec((1, 8, LANES), lambda i: (i, 0, 0)),
        ],
        out_shape=[
            jax.ShapeDtypeStruct((nb, MOE_ROWS, d), BF16),
            jax.ShapeDtypeStruct((nb, MOE_ROWS, LANES), F32),
            jax.ShapeDtypeStruct((nb, 8, LANES), F32),
        ],
        compiler_params=_cparams(("parallel",)),
        name="moe_dispatch",
    )(xn, route, before_row)

    meta_i = meta[:, :N_EXPERT_GROUPS, :2].astype(jnp.int32).reshape(-1)
    n_exp = 4
    assert EXPERTS_PER_GROUP % n_exp == 0
    ys = pl.pallas_call(
        functools.partial(_moe_expert_kernel, n_sub, n_exp),
        grid_spec=pltpu.PrefetchScalarGridSpec(
            num_scalar_prefetch=1,
            grid=(nb // n_sub, n_e // n_exp),
            in_specs=[
                pl.BlockSpec((n_sub, MOE_ROWS, d), lambda i, e, m: (i, 0, 0)),
                pl.BlockSpec((n_sub, MOE_ROWS, LANES), lambda i, e, m: (i, 0, 0)),
                pl.BlockSpec((n_exp, d, f), lambda i, e, m: (e, 0, 0)),
                pl.BlockSpec((n_exp, d, f), lambda i, e, m: (e, 0, 0)),
                pl.BlockSpec((n_exp, f, d), lambda i, e, m: (e, 0, 0)),
            ],
            out_specs=pl.BlockSpec((n_sub, MOE_ROWS, d), lambda i, e, m: (i, 0, 0)),
        ),
        out_shape=jax.ShapeDtypeStruct((nb, MOE_ROWS, d), F32),
        compiler_params=_cparams(("parallel", "arbitrary")),
        name="moe_experts",
    )(meta_i, xs, gs, wg, wu, wd)

    return pl.pallas_call(
        _moe_combine_kernel,
        grid=(nb,),
        in_specs=[
            pl.BlockSpec((1, MOE_ROWS, d), lambda i: (i, 0, 0)),
            pl.BlockSpec((pt, LANES), rows),
            pl.BlockSpec((pt, d), rows),
            pl.BlockSpec((pt, pt), const),
        ],
        out_specs=pl.BlockSpec((pt, d), rows),
        out_shape=jax.ShapeDtypeStruct((t, d), F32),
        compiler_params=_cparams(("parallel",)),
        name="moe_combine",
    )(ys, route, x1, before_col)


def _pad_lanes(v, offset=0):
    out = jnp.zeros((1, LANES), F32)
    return out.at[0, offset:offset + v.shape[0]].set(v.astype(F32))


def kernel(x, norm_mix_w, w_in, conv_w, conv_b, dt_bias, a_log, d_skip, ssm_norm_w, q_norm_w,
           k_norm_w, fox_f_bias, att_out_norm_w, w_out, norm_ffn_w, w_router_group,
           b_router_group, w_router_expert, b_router_expert, w_expert_gate, w_expert_up,
           w_expert_down):
    batch, seq, d = x.shape
    depth = w_in.shape[0]
    d_ssm = ssm_norm_w.shape[1]
    d_att = att_out_norm_w.shape[1]
    h_ssm = dt_bias.shape[1]
    h_att = fox_f_bias.shape[1]
    conv_dim = conv_w.shape[2]
    assert h_ssm == 16 and h_att == 16 and d_ssm // h_ssm == HEAD_DIM and d_att // h_att == HEAD_DIM
    assert conv_dim == d_ssm + 2 * SSM_GROUPS * D_STATE
    off_xbc = d_ssm
    off_dt = off_xbc + conv_dim
    off_q = off_dt + h_ssm
    off_k = off_q + d_att
    off_v = off_k + d_att
    off_f = off_v + d_att
    t = batch * seq
    tm_in = min(512, t)
    tm_out = min(512, t)
    moe_sub = 2 if t % (2 * MOE_BLOCK) == 0 else 1
    tq = min(512, seq)
    tk = tq

    head_of_chan = jnp.arange(d_ssm) // HEAD_DIM
    expand = (jnp.arange(LANES)[:, None] == head_of_chan[None, :]).astype(BF16)

    xf = x.reshape(t, d)
    for l in range(depth):
        wl = w_in[l]
        w_main = jnp.concatenate(
            [wl[:, :off_xbc], wl[:, off_q:off_k], wl[:, off_k:off_v], wl[:, off_v:off_f],
             wl[:, off_xbc:off_dt]], axis=1).astype(BF16)
        p_q, p_k, p_v, p_xbc = d_ssm, d_ssm + d_att, d_ssm + 2 * d_att, d_ssm + 3 * d_att
        w_small = jnp.zeros((d, LANES), F32)
        w_small = w_small.at[:, 0:h_ssm].set(wl[:, off_dt:off_q])
        w_small = w_small.at[:, 16:16 + h_att].set(wl[:, off_f:off_f + h_att]).astype(BF16)
        proj, small = _inproj(xf, norm_mix_w[l][None, :], w_main, w_small, tm_in, 512)

        bias_row = _pad_lanes(dt_bias[l]) + _pad_lanes(fox_f_bias[l], 16)
        a_row = _pad_lanes(-jnp.exp(a_log[l].astype(F32)))
        dskip_row = jnp.repeat(d_skip[l].astype(F32), HEAD_DIM)[None, :]
        y_ssm, c, ct = _ssd(proj, 0, p_xbc, conv_dim, d_ssm, small, conv_w[l].astype(F32),
                        conv_b[l].astype(F32)[None, :], bias_row, a_row, dskip_row,
                        ssm_norm_w[l][None, :], expand, batch, seq)

        qw_row = jnp.tile(q_norm_w[l].astype(F32), 2)[None, :]
        kw_row = jnp.tile(k_norm_w[l].astype(F32), 2)[None, :]
        y_att = _attn(proj, p_q, p_k, p_v, d_att, c, ct, qw_row, kw_row, batch, seq, tq, tk, 4)

        wr = jnp.zeros((d, LANES), F32)
        wr = wr.at[:, :N_EXPERTS].set(w_router_expert[l])
        wr = wr.at[:, N_EXPERTS:N_EXPERTS + N_EXPERT_GROUPS].set(w_router_group[l])
        br = _pad_lanes(b_router_expert[l]) + _pad_lanes(b_router_group[l], N_EXPERTS)
        wo = w_out[l].astype(BF16)
        x1, xn, gates = _outproj(y_ssm, y_att, xf, wo[:d_ssm], wo[d_ssm:],
                                 att_out_norm_w[l][None, :], norm_ffn_w[l][None, :], wr, br, tm_out)

        xf = _moe(xn, gates, x1, w_expert_gate[l].astype(BF16), w_expert_up[l].astype(BF16),
                  w_expert_down[l].astype(BF16), moe_sub)
    return xf.reshape(batch, seq, d)
```
